```python
import math
import jax, jax.numpy as jnp
from jax import lax
import numpy as np

D_MODEL = 1024
BATCH = 16
SEQ = 2048
DEPTH = 1

NSA_HEADS = 8
NSA_KV_HEADS = 2
NSA_GROUP = NSA_HEADS // NSA_KV_HEADS
HEAD_DIM = 64
CMP_BLOCK = 32
CMP_STRIDE = 16
CMP_HIDDEN = 256
SEL_BLOCK = 64
SEL_TOPK = 8
WINDOW = 256
WIN_QBLOCK = 128
FORCE_SCORE = 1.0e4
RET_HEADS = 4
RET_QK_DIM = 64
RET_V_DIM = 128
RET_CHUNK = 128
D_FF = 4 * D_MODEL
ROPE_THETA = 10000.0
EPS = 1e-6
NEG_INF = -1.0e30

NSA_Q_DIM = NSA_HEADS * HEAD_DIM
NSA_KV_DIM = NSA_KV_HEADS * HEAD_DIM
NSA_GATE_DIM = 3 * NSA_HEADS
RET_QK_WIDTH = RET_HEADS * RET_QK_DIM
RET_V_WIDTH = RET_HEADS * RET_V_DIM
MIX_WIDTH = NSA_Q_DIM + RET_V_WIDTH
IN_SIZES = [NSA_Q_DIM, NSA_KV_DIM, NSA_KV_DIM, NSA_KV_DIM, NSA_KV_DIM, NSA_KV_DIM, NSA_KV_DIM,
            NSA_GATE_DIM, RET_QK_WIDTH, RET_QK_WIDTH, RET_V_WIDTH, RET_V_WIDTH]
IN_WIDTH = sum(IN_SIZES)

kernel_name = "hybrid_nsa_retention_sqrelu"


def rms_norm(x, w):
    xf = x.astype(jnp.float32)
    y = xf * lax.rsqrt(jnp.mean(xf * xf, axis=-1, keepdims=True) + EPS)
    return (y * w.astype(jnp.float32)).astype(x.dtype)


def rope(x, pos):
    half = x.shape[-1] // 2
    inv = ROPE_THETA ** (-jnp.arange(half, dtype=jnp.float32) / half)
    ang = pos.astype(jnp.float32)[:, None] * inv[None, :]
    cos, sin = jnp.cos(ang), jnp.sin(ang)
    xf = x.astype(jnp.float32)
    x1, x2 = xf[..., :half], xf[..., half:]
    return jnp.concatenate([x1 * cos - x2 * sin, x2 * cos + x1 * sin], axis=-1).astype(x.dtype)


def masked_softmax(s, mask):
    s = jnp.where(mask, s.astype(jnp.float32), NEG_INF)
    p = jax.nn.softmax(s, axis=-1)
    return jnp.where(mask, p, 0.0)


def compress(blocks, w1, w2):
    b, g, n, l, d = blocks.shape
    return jax.nn.silu(blocks.reshape(b, g, n, l * d) @ w1) @ w2


def nsa_mixer(q, kc, vc, ks, vs, kw, vw, gates, q_norm_w, k_norm_w,
              cmp_pe_k, cmp_pe_v, cmp_wk1, cmp_wk2, cmp_wv1, cmp_wv2):
    B, S, _ = q.shape
    G, R, dh = NSA_KV_HEADS, NSA_GROUP, HEAD_DIM
    pos = jnp.arange(S)
    scale = dh ** -0.5
    q = q.reshape(B, S, G, R, dh).transpose(0, 2, 3, 1, 4)
    q = rope(rms_norm(q, q_norm_w), pos)

    def prep_k(k, w):
        k = k.reshape(B, S, G, dh).transpose(0, 2, 1, 3)
        return rope(rms_norm(k, w), pos)

    def prep_v(v):
        return v.reshape(B, S, G, dh).transpose(0, 2, 1, 3)

    kc, ks, kw = prep_k(kc, k_norm_w[0]), prep_k(ks, k_norm_w[1]), prep_k(kw, k_norm_w[2])
    vc, vs, vw = prep_v(vc), prep_v(vs), prep_v(vw)

    n_cmp = (S - CMP_BLOCK) // CMP_STRIDE + 1
    c_start = jnp.arange(n_cmp) * CMP_STRIDE
    cidx = c_start[:, None] + jnp.arange(CMP_BLOCK)[None, :]
    k_cmp = compress(kc[:, :, cidx] + cmp_pe_k, cmp_wk1, cmp_wk2)
    v_cmp = compress(vc[:, :, cidx] + cmp_pe_v, cmp_wv1, cmp_wv2)
    s_cmp = jnp.einsum('bgrqd,bgcd->bgrqc', q, k_cmp) * scale
    cmask = (c_start + CMP_BLOCK - 1)[None, :] <= pos[:, None]
    p_cmp = masked_softmax(s_cmp, cmask)
    o_cmp = jnp.einsum('bgrqc,bgcd->bgrqd', p_cmp.astype(v_cmp.dtype), v_cmp)

    n_blk = S // SEL_BLOCK
    jb = jnp.arange(n_blk)
    overlap = ((c_start[:, None] < (jb[None, :] + 1) * SEL_BLOCK)
               & (c_start[:, None] + CMP_BLOCK > jb[None, :] * SEL_BLOCK)).astype(jnp.float32)
    p_slc = jnp.einsum('bgrqc,cj->bgqj', p_cmp, overlap)
    cur = pos // SEL_BLOCK
    forced = (jb[None, :] == 0) | (jb[None, :] == cur[:, None]) | (jb[None, :] == cur[:, None] - 1)
    valid = jb[None, :] <= cur[:, None]
    score = jnp.where(forced, FORCE_SCORE, jnp.where(valid, p_slc, -1.0))
    n_sel = min(SEL_TOPK, n_blk)
    _, sel_idx = lax.top_k(score, n_sel)

    k_blocks = ks.reshape(B, G, n_blk, SEL_BLOCK, dh)
    v_blocks = vs.reshape(B, G, n_blk, SEL_BLOCK, dh)
    nq = S // SEL_BLOCK
    q_b = jnp.moveaxis(q.reshape(B, G, R, nq, SEL_BLOCK, dh), 3, 0)
    idx_b = jnp.moveaxis(sel_idx.reshape(B, G, nq, SEL_BLOCK, n_sel), 2, 0)
    t_b = pos.reshape(nq, SEL_BLOCK)
    bi = jnp.arange(B)[:, None, None, None]
    gi = jnp.arange(G)[None, :, None, None]
    n_keys = n_sel * SEL_BLOCK

    def sel_block(args):
        qc, ic, tc = args
        ksel = k_blocks[bi, gi, ic]
        vsel = v_blocks[bi, gi, ic]
        s = jnp.einsum('bgrqd,bgqnkd->bgrqnk', qc, ksel) * scale
        kpos = ic[..., None] * SEL_BLOCK + jnp.arange(SEL_BLOCK)
        mask = (kpos <= tc[None, None, :, None, None])[:, :, None]
        p = masked_softmax(s.reshape(B, G, R, SEL_BLOCK, n_keys),
                           mask.reshape(B, G, 1, SEL_BLOCK, n_keys))
        return jnp.einsum('bgrqk,bgqkd->bgrqd', p.astype(vsel.dtype),
                          vsel.reshape(B, G, SEL_BLOCK, n_keys, dh))

    o_sel = lax.map(sel_block, (q_b, idx_b, t_b))
    o_sel = jnp.moveaxis(o_sel, 0, 3).reshape(B, G, R, S, dh)

    nw = S // WIN_QBLOCK
    n_wk = WIN_QBLOCK + WINDOW
    widx = jnp.arange(nw)[:, None] * WIN_QBLOCK + jnp.arange(n_wk)[None, :]
    pad = ((0, 0), (0, 0), (WINDOW, 0), (0, 0))
    kwin = jnp.pad(kw, pad)[:, :, widx]
    vwin = jnp.pad(vw, pad)[:, :, widx]
    kpos = widx - WINDOW
    qw = q.reshape(B, G, R, nw, WIN_QBLOCK, dh)
    s_w = jnp.einsum('bgrcqd,bgckd->bgrcqk', qw, kwin) * scale
    tq = pos.reshape(nw, WIN_QBLOCK)
    diff = tq[:, :, None] - kpos[:, None, :]
    wmask = (diff >= 0) & (diff < WINDOW) & (kpos[:, None, :] >= 0)
    p_w = masked_softmax(s_w, wmask)
    o_win = jnp.einsum('bgrcqk,bgckd->bgrcqd', p_w.astype(vwin.dtype), vwin).reshape(B, G, R, S, dh)

    g = jax.nn.sigmoid(gates.astype(jnp.float32)).reshape(B, S, 3, G, R)
    g = g.transpose(2, 0, 3, 4, 1)[..., None].astype(q.dtype)
    o = g[0] * o_cmp + g[1] * o_sel + g[2] * o_win
    return o.transpose(0, 3, 1, 2, 4).reshape(B, S, NSA_Q_DIM)


def retention_mixer(q, k, v, gate, ret_norm_w):
    B, S, _ = q.shape
    H, dk, dv, C = RET_HEADS, RET_QK_DIM, RET_V_DIM, RET_CHUNK
    out_dtype = q.dtype
    pos = jnp.arange(S)
    q = rope(q.reshape(B, S, H, dk).transpose(0, 2, 1, 3), pos).astype(jnp.float32)
    k = (rope(k.reshape(B, S, H, dk).transpose(0, 2, 1, 3), pos).astype(jnp.float32)) * (dk ** -0.5)
    v = v.reshape(B, S, H, dv).transpose(0, 2, 1, 3).astype(jnp.float32)

    log_gamma = jnp.log(1.0 - 2.0 ** (-5.0 - jnp.arange(H, dtype=jnp.float32)))
    i = jnp.arange(C)
    d_int = i[:, None] - i[None, :]
    dmat = jnp.where(d_int >= 0,
                     jnp.exp(log_gamma[:, None, None] * jnp.maximum(d_int, 0).astype(jnp.float32)), 0.0)
    xi = jnp.exp(log_gamma[:, None] * (i + 1).astype(jnp.float32))[..., None]
    zeta = jnp.exp(log_gamma[:, None] * (C - 1 - i).astype(jnp.float32))[..., None]
    gamma_c = jnp.exp(log_gamma * C)[:, None, None]

    nc = S // C
    to_chunks = lambda t: jnp.moveaxis(t.reshape(B, H, nc, C, t.shape[-1]), 2, 0)

    def step(state, inp):
        qc, kc, vc = inp
        inner = jnp.einsum('bhqd,bhkd->bhqk', qc, kc) * dmat
        y = (jnp.einsum('bhqk,bhkv->bhqv', inner, vc)
             + jnp.einsum('bhqd,bhdv->bhqv', qc, state) * xi)
        state = gamma_c * state + jnp.einsum('bhkd,bhkv->bhdv', kc * zeta, vc)
        return state, y

    state0 = jnp.zeros((B, H, dk, dv), jnp.float32)
    _, y = lax.scan(step, state0, (to_chunks(q), to_chunks(k), to_chunks(v)))
    y = jnp.moveaxis(y, 0, 2).reshape(B, H, S, dv)
    mu = jnp.mean(y, axis=-1, keepdims=True)
    var = jnp.mean(jnp.square(y - mu), axis=-1, keepdims=True)
    y = (y - mu) * lax.rsqrt(var + EPS) * ret_norm_w.astype(jnp.float32)[None, :, None, :]
    y = y.transpose(0, 2, 1, 3).reshape(B, S, RET_V_WIDTH)
    return (jax.nn.silu(gate.astype(jnp.float32)) * y).astype(out_dtype)


def setup_inputs(seed: int = 0) -> dict:
    key = jax.random.key(seed)
    ks = jax.random.split(key, 16)
    nrm = lambda k, shape, s: jax.random.normal(k, shape, jnp.float32) * s
    L = DEPTH
    return {
        "x": nrm(ks[0], (BATCH, SEQ, D_MODEL), 1.0),
        "ln1_w": 1.0 + nrm(ks[1], (L, D_MODEL), 0.02),
        "w_in": nrm(ks[2], (L, D_MODEL, IN_WIDTH), D_MODEL ** -0.5),
        "q_norm_w": 1.0 + nrm(ks[3], (L, HEAD_DIM), 0.02),
        "k_norm_w": 1.0 + nrm(ks[4], (L, 3, HEAD_DIM), 0.02),
        "cmp_pe_k": nrm(ks[5], (L, CMP_BLOCK, HEAD_DIM), 0.02),
        "cmp_pe_v": nrm(ks[6], (L, CMP_BLOCK, HEAD_DIM), 0.02),
        "cmp_wk1": nrm(ks[7], (L, CMP_BLOCK * HEAD_DIM, CMP_HIDDEN), (CMP_BLOCK * HEAD_DIM) ** -0.5),
        "cmp_wk2": nrm(ks[8], (L, CMP_HIDDEN, HEAD_DIM), CMP_HIDDEN ** -0.5),
        "cmp_wv1": nrm(ks[9], (L, CMP_BLOCK * HEAD_DIM, CMP_HIDDEN), (CMP_BLOCK * HEAD_DIM) ** -0.5),
        "cmp_wv2": nrm(ks[10], (L, CMP_HIDDEN, HEAD_DIM), CMP_HIDDEN ** -0.5),
        "ret_norm_w": 1.0 + nrm(ks[11], (L, RET_HEADS, RET_V_DIM), 0.02),
        "w_out": nrm(ks[12], (L, MIX_WIDTH, D_MODEL), MIX_WIDTH ** -0.5),
        "ln2_w": 1.0 + nrm(ks[13], (L, D_MODEL), 0.02),
        "w_up": nrm(ks[14], (L, D_MODEL, D_FF), D_MODEL ** -0.5),
        "w_down": nrm(ks[15], (L, D_FF, D_MODEL), D_FF ** -0.5),
    }


def reference(x, ln1_w, w_in, q_norm_w, k_norm_w, cmp_pe_k, cmp_pe_v, cmp_wk1, cmp_wk2,
              cmp_wv1, cmp_wv2, ret_norm_w, w_out, ln2_w, w_up, w_down):
    split_points = [int(p) for p in np.cumsum(IN_SIZES)[:-1]]
    h = x
    for l in range(DEPTH):
        xn = rms_norm(h, ln1_w[l])
        proj = xn @ w_in[l]
        (nq_, nkc, nvc, nks, nvs, nkw, nvw, ngate,
         rq, rk, rv, rg) = jnp.split(proj, split_points, axis=-1)
        o_nsa = nsa_mixer(nq_, nkc, nvc, nks, nvs, nkw, nvw, ngate, q_norm_w[l], k_norm_w[l],
                          cmp_pe_k[l], cmp_pe_v[l], cmp_wk1[l], cmp_wk2[l], cmp_wv1[l], cmp_wv2[l])
        o_ret = retention_mixer(rq, rk, rv, rg, ret_norm_w[l])
        mix = jnp.concatenate([o_nsa, o_ret], axis=-1)
        h = h + mix @ w_out[l]
        hn = rms_norm(h, ln2_w[l])
        h = h + jnp.square(jax.nn.relu(hn @ w_up[l])) @ w_down[l]
    return h
```

```python
import functools
import math

import jax
import jax.numpy as jnp
import numpy as np
from jax import lax
from jax.experimental import pallas as pl
from jax.experimental.pallas import tpu as pltpu

F32 = jnp.float32
BF16 = jnp.bfloat16

NSA_HEADS = 8
NSA_KV_HEADS = 2
NSA_GROUP = NSA_HEADS // NSA_KV_HEADS
HEAD_DIM = 64
CMP_BLOCK = 32
CMP_STRIDE = 16
CMP_HIDDEN = 256
SEL_BLOCK = 64
SEL_TOPK = 8
WINDOW = 256
FORCE_SCORE = 1.0e4
RET_HEADS = 4
RET_QK_DIM = 64
RET_V_DIM = 128
RET_CHUNK = 128
ROPE_THETA = 10000.0
EPS = 1e-6
NEG_INF = -1.0e30

NSA_Q_DIM = NSA_HEADS * HEAD_DIM
NSA_KV_DIM = NSA_KV_HEADS * HEAD_DIM
NSA_GATE_DIM = 3 * NSA_HEADS
RET_QK_WIDTH = RET_HEADS * RET_QK_DIM
RET_V_WIDTH = RET_HEADS * RET_V_DIM

LANES = 128
VMEM_LIMIT_BYTES = 56 * 1024 * 1024

TM_PROJ = 512
TQ = 128
TK = 256
TM_FFN = 512
FF_CHUNK = 512
CMP_ROWS = 512


def _dot(a, b):
    return jnp.dot(a, b, preferred_element_type=F32)


def _dot_nt(a, b):
    return lax.dot_general(a, b, (((1,), (1,)), ((), ())), preferred_element_type=F32)


def _sigmoid(x):
    return 1.0 / (1.0 + jnp.exp(-x))


_Q0 = 0
_K0 = _Q0 + NSA_Q_DIM
_V0 = _K0 + 3 * NSA_KV_DIM
_G0 = _V0 + 3 * NSA_KV_DIM
_RQ0 = _G0 + NSA_KV_HEADS * LANES
_RK0 = _RQ0 + RET_QK_WIDTH
_RV0 = _RK0 + RET_QK_WIDTH
_RG0 = _RV0 + RET_V_WIDTH
_W_IN_COLS = _RG0 + RET_V_WIDTH


def _w_in_column_order():
    sizes = [NSA_Q_DIM] + [NSA_KV_DIM] * 6 + [NSA_GATE_DIM, RET_QK_WIDTH, RET_QK_WIDTH, RET_V_WIDTH, RET_V_WIDTH]
    off = np.concatenate([[0], np.cumsum(sizes)])
    (q, kc, vc, ks, vs, kw, vw, gate, rq, rk, rv, rg) = [np.arange(off[i], off[i + 1]) for i in range(12)]
    gates = -np.ones((NSA_KV_HEADS, LANES), np.int64)
    for g in range(NSA_KV_HEADS):
        for br in range(3):
            for r in range(NSA_GROUP):
                gates[g, br * NSA_GROUP + r] = gate[br * NSA_HEADS + g * NSA_GROUP + r]
    order = np.concatenate([q, kc, ks, kw, vc, vs, vw, gates.reshape(-1), rq, rk, rv, rg])
    assert order.shape[0] == _W_IN_COLS
    return order


def _in_proj_kernel(x_ref, ln_ref, w_ref, cos_ref, sin_ref, qn_ref, kn_ref,
                    q_ref, kc_ref, vc_ref, ks_ref, vs_ref, kw_ref, vw_ref, gate_ref,
                    rq_ref, rk_ref, rv_ref, rg_ref):
    x = x_ref[...]
    ms = jnp.mean(x * x, axis=-1, keepdims=True)
    xn = (x * lax.rsqrt(ms + EPS) * ln_ref[...]).astype(BF16)
    cos = cos_ref[...]
    sin = sin_ref[...]
    lane = lax.broadcasted_iota(jnp.int32, (1, LANES), 1)
    low_half = (lane & (HEAD_DIM // 2)) == 0
    first_head = lane < HEAD_DIM

    def proj(a, b):
        return _dot(xn, w_ref[:, a:b])

    def rope(t):
        swapped = jnp.where(low_half, pltpu.roll(t, LANES - HEAD_DIM // 2, 1), pltpu.roll(t, HEAD_DIM // 2, 1))
        return t * cos + swapped * sin

    def head_norm(t, w):
        t2 = t * t
        s0 = jnp.sum(jnp.where(first_head, t2, 0.0), axis=-1, keepdims=True)
        s1 = jnp.sum(jnp.where(first_head, 0.0, t2), axis=-1, keepdims=True)
        msq = jnp.where(first_head, s0, s1) * (1.0 / HEAD_DIM)
        return t * lax.rsqrt(msq + EPS) * w

    def tiles(sec):
        return [sec[:, c * LANES:(c + 1) * LANES] for c in range(sec.shape[1] // LANES)]

    qn = qn_ref[...]
    for c, t in enumerate(tiles(proj(_Q0, _K0))):
        t = rope(head_norm(t, qn)).astype(BF16)
        for j in range(2):
            h = 2 * c + j
            q_ref[0, h // NSA_GROUP, h % NSA_GROUP] = t[:, j * HEAD_DIM:(j + 1) * HEAD_DIM]

    ksec = tiles(proj(_K0, _V0))
    kc = rope(head_norm(ksec[0], kn_ref[0:1, :]))
    ks = rope(head_norm(ksec[1], kn_ref[1:2, :])).astype(BF16)
    kw = rope(head_norm(ksec[2], kn_ref[2:3, :])).astype(BF16)
    vsec = tiles(proj(_V0, _G0))
    for g in range(NSA_KV_HEADS):
        sl = slice(g * HEAD_DIM, (g + 1) * HEAD_DIM)
        kc_ref[0, g] = kc[:, sl]
        vc_ref[0, g] = vsec[0][:, sl]
        ks_ref[0, g] = ks[:, sl]
        vs_ref[0, g] = vsec[1][:, sl].astype(BF16)
        kw_ref[0, g] = kw[:, sl]
        vw_ref[0, g] = vsec[2][:, sl].astype(BF16)

    gate_ref[...] = proj(_G0, _RQ0)

    for c, t in enumerate(tiles(proj(_RQ0, _RK0))):
        rq_ref[:, c * LANES:(c + 1) * LANES] = rope(t).astype(BF16)
    for c, t in enumerate(tiles(proj(_RK0, _RV0))):
        rk_ref[:, c * LANES:(c + 1) * LANES] = (rope(t) * (RET_QK_DIM ** -0.5)).astype(BF16)
    rv_ref[...] = proj(_RV0, _RG0).astype(BF16)
    rg_ref[...] = proj(_RG0, _W_IN_COLS).astype(BF16)


def _in_proj(x2, ln1, w_perm, cos, sin, qn, kn, B, S):
    N, D = x2.shape
    tm = TM_PROJ
    spt = S // tm
    G, R, dh = NSA_KV_HEADS, NSA_GROUP, HEAD_DIM

    def row(i):
        return (i, 0)

    def hm(i):
        return (i // spt, 0, i % spt, 0)

    hm_spec = pl.BlockSpec((1, G, tm, dh), hm)
    out_shape = [
        jax.ShapeDtypeStruct((B, G, R, S, dh), BF16),
        jax.ShapeDtypeStruct((B, G, S, dh), F32),
        jax.ShapeDtypeStruct((B, G, S, dh), F32),
        jax.ShapeDtypeStruct((B, G, S, dh), BF16),
        jax.ShapeDtypeStruct((B, G, S, dh), BF16),
        jax.ShapeDtypeStruct((B, G, S, dh), BF16),
        jax.ShapeDtypeStruct((B, G, S, dh), BF16),
        jax.ShapeDtypeStruct((N, G * LANES), F32),
        jax.ShapeDtypeStruct((N, RET_QK_WIDTH), BF16),
        jax.ShapeDtypeStruct((N, RET_QK_WIDTH), BF16),
        jax.ShapeDtypeStruct((N, RET_V_WIDTH), BF16),
        jax.ShapeDtypeStruct((N, RET_V_WIDTH), BF16),
    ]
    out_specs = [
        pl.BlockSpec((1, G, R, tm, dh), lambda i: (i // spt, 0, 0, i % spt, 0)),
        hm_spec, hm_spec, hm_spec, hm_spec, hm_spec, hm_spec,
        pl.BlockSpec((tm, G * LANES), row),
        pl.BlockSpec((tm, RET_QK_WIDTH), row),
        pl.BlockSpec((tm, RET_QK_WIDTH), row),
        pl.BlockSpec((tm, RET_V_WIDTH), row),
        pl.BlockSpec((tm, RET_V_WIDTH), row),
    ]
    const = lambda i: (0, 0)
    in_specs = [
        pl.BlockSpec((tm, D), row),
        pl.BlockSpec((1, D), const),
        pl.BlockSpec((D, _W_IN_COLS), const, pipeline_mode=pl.Buffered(1)),
        pl.BlockSpec((tm, LANES), lambda i: (i % spt, 0)),
        pl.BlockSpec((tm, LANES), lambda i: (i % spt, 0)),
        pl.BlockSpec((1, LANES), const),
        pl.BlockSpec((3, LANES), const),
    ]
    return pl.pallas_call(
        _in_proj_kernel,
        grid=(N // tm,),
        in_specs=in_specs,
        out_specs=out_specs,
        out_shape=out_shape,
        compiler_params=pltpu.CompilerParams(
            dimension_semantics=("parallel",), vmem_limit_bytes=VMEM_LIMIT_BYTES),
        name="in_proj",
    )(x2, ln1, w_perm, cos, sin, qn, kn)


def _compress_kernel(k_ref, v_ref, pek_ref, pev_ref, wk1_ref, wk2_ref, wv1_ref, wv2_ref, ko_ref, vo_ref):
    half = CMP_STRIDE * HEAD_DIM

    def run(seg_ref, pe_ref, w1_ref, w2_ref, o_ref):
        seg = seg_ref[...]
        rows = seg.shape[0]
        top = (seg + pe_ref[0:1, :]).astype(BF16)
        bot = (seg + pe_ref[1:2, :]).astype(BF16)
        a = _dot(top, w1_ref[0:half, :])
        b = _dot(bot, w1_ref[half:2 * half, :])
        h = a + pltpu.roll(b, rows - 1, 0)
        h = (h * _sigmoid(h)).astype(BF16)
        o_ref[...] = _dot(h, w2_ref[...]).astype(BF16)

    run(k_ref, pek_ref, wk1_ref, wk2_ref, ko_ref)
    run(v_ref, pev_ref, wv1_ref, wv2_ref, vo_ref)


def _compress(kc_seg, vc_seg, pek, pev, wk1, wk2, wv1, wv2):
    rows, width = kc_seg.shape
    tr = min(CMP_ROWS, rows)
    const = lambda i: (0, 0)
    seg_spec = pl.BlockSpec((tr, width), lambda i: (i, 0))
    out_spec = pl.BlockSpec((tr, HEAD_DIM), lambda i: (i, 0))
    return pl.pallas_call(
        _compress_kernel,
        grid=(rows // tr,),
        in_specs=[seg_spec, seg_spec,
                  pl.BlockSpec((2, width), const), pl.BlockSpec((2, width), const),
                  pl.BlockSpec((2 * width, CMP_HIDDEN), const), pl.BlockSpec((CMP_HIDDEN, HEAD_DIM), const),
                  pl.BlockSpec((2 * width, CMP_HIDDEN), const), pl.BlockSpec((CMP_HIDDEN, HEAD_DIM), const)],
        out_specs=[out_spec, out_spec],
        out_shape=[jax.ShapeDtypeStruct((rows, HEAD_DIM), BF16)] * 2,
        compiler_params=pltpu.CompilerParams(
            dimension_semantics=("parallel",), vmem_limit_bytes=VMEM_LIMIT_BYTES),
        name="compress",
    )(kc_seg, vc_seg, pek, pev, wk1, wk2, wv1, wv2)


def _nsa_kernel(q_ref, kcmp_ref, vcmp_ref, ks_ref, vs_ref, kw_ref, vw_ref, gate_ref, expand_ref,
                o_ref, m_ref, l_ref, acc_ref):
    R = NSA_GROUP
    n_cmp_pad = kcmp_ref.shape[2]
    S = ks_ref.shape[2]
    n_cmp = (S - CMP_BLOCK) // CMP_STRIDE + 1
    n_blk = S // SEL_BLOCK
    scale = HEAD_DIM ** -0.5
    q0 = pl.program_id(2) * TQ
    q = q_ref[0, 0].reshape(R * TQ, HEAD_DIM)
    t_q = q0 + lax.broadcasted_iota(jnp.int32, (TQ, 1), 0)

    kcmp = kcmp_ref[0, 0]
    vcmp = vcmp_ref[0, 0]
    s_c = _dot_nt(q, kcmp) * scale
    c_idx = lax.broadcasted_iota(jnp.int32, (1, n_cmp_pad), 1)
    cmask = ((c_idx * CMP_STRIDE + (CMP_BLOCK - 1)) <= t_q) & (c_idx < n_cmp)
    p_heads = []
    for r in range(R):
        s = jnp.where(cmask, s_c[r * TQ:(r + 1) * TQ], NEG_INF)
        e = jnp.exp(s - jnp.max(s, axis=-1, keepdims=True))
        p = e / jnp.sum(e, axis=-1, keepdims=True)
        p_heads.append(jnp.where(cmask, p, 0.0))
    o_cmp = [_dot(p.astype(BF16), vcmp) for p in p_heads]
    p_sum = p_heads[0]
    for r in range(1, R):
        p_sum = p_sum + p_heads[r]

    cc = lax.broadcasted_iota(jnp.int32, (n_cmp_pad, LANES), 0)
    jj = lax.broadcasted_iota(jnp.int32, (n_cmp_pad, LANES), 1)
    ratio = SEL_BLOCK // CMP_STRIDE
    overlap = ((cc * CMP_STRIDE < (jj + 1) * SEL_BLOCK) & (cc * CMP_STRIDE + CMP_BLOCK > jj * SEL_BLOCK)
               & (cc < n_cmp) & (jj < n_blk)).astype(BF16)
    del ratio
    p_hi = p_sum.astype(BF16)
    p_lo = (p_sum - p_hi.astype(F32)).astype(BF16)
    p_slc = _dot(p_hi, overlap) + _dot(p_lo, overlap)
    jb = lax.broadcasted_iota(jnp.int32, (1, LANES), 1)
    cur = t_q // SEL_BLOCK
    forced = (jb == 0) | (jb == cur) | (jb == cur - 1)
    valid = jb <= cur
    score = jnp.where(forced, FORCE_SCORE, jnp.where(valid, p_slc, -1.0))
    score = jnp.where(jb < n_blk, score, -2.0)
    rank = jnp.zeros((TQ, LANES), jnp.int32)
    for i in range(n_blk):
        col = score[:, i:i + 1]
        before = (col > score) | ((col == score) & (jb > i))
        rank = rank + before.astype(jnp.int32)
    sel = ((rank < min(SEL_TOPK, n_blk)) & (jb < n_blk)).astype(BF16)

    m_ref[...] = jnp.full(m_ref.shape, NEG_INF, F32)
    l_ref[...] = jnp.zeros(l_ref.shape, F32)
    acc_ref[...] = jnp.zeros(acc_ref.shape, F32)
    n_kt = (q0 + TQ + TK - 1) // TK

    def sel_step(kt, carry):
        k0 = pl.multiple_of(kt * TK, TK)
        k = ks_ref[0, 0, pl.ds(k0, TK), :]
        v = vs_ref[0, 0, pl.ds(k0, TK), :]
        s_all = _dot_nt(q, k) * scale
        chosen = _dot(sel, expand_ref[kt]) > 0.5
        kpos = k0 + lax.broadcasted_iota(jnp.int32, (1, TK), 1)
        mask = chosen & (kpos <= t_q)
        for r in range(R):
            rows = slice(r * TQ, (r + 1) * TQ)
            s = jnp.where(mask, s_all[rows], NEG_INF)
            m_old = m_ref[rows]
            m_new = jnp.maximum(m_old, jnp.max(s, axis=-1, keepdims=True))
            alpha = jnp.exp(m_old - m_new)
            p = jnp.exp(s - m_new)
            l_ref[rows] = alpha * l_ref[rows] + jnp.sum(p, axis=-1, keepdims=True)
            acc_ref[rows] = alpha * acc_ref[rows] + _dot(p.astype(BF16), v)
            m_ref[rows] = m_new
        return carry

    lax.fori_loop(0, n_kt, sel_step, 0)

    n_wk = TQ + WINDOW
    w0 = pl.multiple_of(jnp.maximum(q0 - WINDOW, 0), TQ)
    kwin = kw_ref[0, 0, pl.ds(w0, n_wk), :]
    vwin = vw_ref[0, 0, pl.ds(w0, n_wk), :]
    s_w = _dot_nt(q, kwin) * scale
    diff = t_q - (w0 + lax.broadcasted_iota(jnp.int32, (1, n_wk), 1))
    wmask = (diff >= 0) & (diff < WINDOW)

    gates = _sigmoid(gate_ref[0])
    for r in range(R):
        rows = slice(r * TQ, (r + 1) * TQ)
        s = jnp.where(wmask, s_w[rows], NEG_INF)
        e = jnp.exp(s - jnp.max(s, axis=-1, keepdims=True))
        p = jnp.where(wmask, e / jnp.sum(e, axis=-1, keepdims=True), 0.0)
        o_win = _dot(p.astype(BF16), vwin)
        o_sel = acc_ref[rows] / l_ref[rows]
        o = (gates[:, r:r + 1] * o_cmp[r] + gates[:, R + r:R + r + 1] * o_sel
             + gates[:, 2 * R + r:2 * R + r + 1] * o_win)
        o_ref[0, :, r * HEAD_DIM:(r + 1) * HEAD_DIM] = o.astype(BF16)


def _nsa(q, kcmp, vcmp, ks, vs, kw, vw, gates, expand):
    B, G, R, S, dh = q.shape
    n_cp = kcmp.shape[2]
    kv_spec = pl.BlockSpec((1, 1, S, dh), lambda b, g, i: (b, g, 0, 0))
    cmp_spec = pl.BlockSpec((1, 1, n_cp, dh), lambda b, g, i: (b, g, 0, 0))
    return pl.pallas_call(
        _nsa_kernel,
        grid=(B, G, S // TQ),
        in_specs=[
            pl.BlockSpec((1, 1, R, TQ, dh), lambda b, g, i: (b, g, 0, i, 0)),
            cmp_spec, cmp_spec, kv_spec, kv_spec, kv_spec, kv_spec,
            pl.BlockSpec((1, TQ, LANES), lambda b, g, i: (b, i, g)),
            pl.BlockSpec(expand.shape, lambda b, g, i: (0, 0, 0)),
        ],
        out_specs=pl.BlockSpec((1, TQ, R * dh), lambda b, g, i: (b, i, g)),
        out_shape=jax.ShapeDtypeStruct((B, S, G * R * dh), BF16),
        scratch_shapes=[pltpu.VMEM((R * TQ, 1), F32), pltpu.VMEM((R * TQ, 1), F32),
                        pltpu.VMEM((R * TQ, dh), F32)],
        compiler_params=pltpu.CompilerParams(
            dimension_semantics=("parallel", "parallel", "arbitrary"), vmem_limit_bytes=VMEM_LIMIT_BYTES),
        name="nsa_attention",
    )(q, kcmp, vcmp, ks, vs, kw, vw, gates, expand)


def _retention_kernel(q_ref, k_ref, v_ref, g_ref, w_ref, o_ref, state_ref):
    C = RET_CHUNK

    @pl.when(pl.program_id(1) == 0)
    def _():
        state_ref[...] = jnp.zeros(state_ref.shape, F32)

    i_col = lax.broadcasted_iota(jnp.int32, (C, 1), 0)
    d_int = lax.broadcasted_iota(jnp.int32, (C, C), 0) - lax.broadcasted_iota(jnp.int32, (C, C), 1)
    for h in range(RET_HEADS):
        log_gamma = math.log(1.0 - 2.0 ** (-5.0 - h))
        dmat = jnp.where(d_int >= 0, jnp.exp(log_gamma * jnp.maximum(d_int, 0).astype(F32)), 0.0)
        xi = jnp.exp(log_gamma * (i_col + 1).astype(F32))
        zeta = jnp.exp(log_gamma * (C - 1 - i_col).astype(F32))
        gamma_c = math.exp(log_gamma * C)

        qs = slice(h * RET_QK_DIM, (h + 1) * RET_QK_DIM)
        vsl = slice(h * RET_V_DIM, (h + 1) * RET_V_DIM)
        q = q_ref[0, :, qs]
        k = k_ref[0, :, qs]
        v = v_ref[0, :, vsl]
        state = state_ref[h]
        inner = _dot_nt(q, k) * dmat
        y = _dot(inner.astype(BF16), v) + _dot(q, state.astype(BF16)) * xi
        kz = (k.astype(F32) * zeta).T.astype(BF16)
        state_ref[h] = gamma_c * state + _dot(kz, v)

        mu = jnp.mean(y, axis=-1, keepdims=True)
        yc = y - mu
        var = jnp.mean(yc * yc, axis=-1, keepdims=True)
        yn = yc * lax.rsqrt(var + EPS) * w_ref[:, vsl]
        gate = g_ref[0, :, vsl].astype(F32)
        o_ref[0, :, vsl] = (gate * _sigmoid(gate) * yn).astype(BF16)


def _retention(rq, rk, rv, rg, w):
    B, S, _ = rq.shape
    C = RET_CHUNK
    qk_spec = pl.BlockSpec((1, C, RET_QK_WIDTH), lambda b, c: (b, c, 0))
    v_spec = pl.BlockSpec((1, C, RET_V_WIDTH), lambda b, c: (b, c, 0))
    return pl.pallas_call(
        _retention_kernel,
        grid=(B, S // C),
        in_specs=[qk_spec, qk_spec, v_spec, v_spec, pl.BlockSpec((1, RET_V_WIDTH), lambda b, c: (0, 0))],
        out_specs=v_spec,
        out_shape=jax.ShapeDtypeStruct((B, S, RET_V_WIDTH), BF16),
        scratch_shapes=[pltpu.VMEM((RET_HEADS, RET_QK_DIM, RET_V_DIM), F32)],
        compiler_params=pltpu.CompilerParams(
            dimension_semantics=("parallel", "arbitrary"), vmem_limit_bytes=VMEM_LIMIT_BYTES),
        name="retention",
    )(rq, rk, rv, rg, w)


def _out_ffn_kernel(x_ref, a_ref, r_ref, wo_ref, ln_ref, wu_ref, wd_ref, o_ref):
    na = a_ref.shape[1]
    mix = _dot(a_ref[...], wo_ref[0:na, :]) + _dot(r_ref[...], wo_ref[na:, :])
    h = x_ref[...] + mix
    ms = jnp.mean(h * h, axis=-1, keepdims=True)
    hn = (h * lax.rsqrt(ms + EPS) * ln_ref[...]).astype(BF16)
    d_ff = wu_ref.shape[1]
    acc = None
    for f in range(d_ff // FF_CHUNK):
        cols = slice(f * FF_CHUNK, (f + 1) * FF_CHUNK)
        u = jnp.maximum(_dot(hn, wu_ref[:, cols]), 0.0)
        d = _dot((u * u).astype(BF16), wd_ref[cols, :])
        acc = d if acc is None else acc + d
    o_ref[...] = h + acc


def _out_ffn(x2, o_nsa, o_ret, w_out, ln2, w_up, w_down):
    N, D = x2.shape
    tm = TM_FFN
    row = lambda i: (i, 0)
    const = lambda i: (0, 0)
    resident = functools.partial(pl.BlockSpec, index_map=const, pipeline_mode=pl.Buffered(1))
    return pl.pallas_call(
        _out_ffn_kernel,
        grid=(N // tm,),
        in_specs=[
            pl.BlockSpec((tm, D), row),
            pl.BlockSpec((tm, o_nsa.shape[1]), row),
            pl.BlockSpec((tm, o_ret.shape[1]), row),
            resident(w_out.shape),
            pl.BlockSpec((1, D), const),
            resident(w_up.shape),
            resident(w_down.shape),
        ],
        out_specs=pl.BlockSpec((tm, D), row),
        out_shape=jax.ShapeDtypeStruct((N, D), F32),
        compiler_params=pltpu.CompilerParams(
            dimension_semantics=("parallel",), vmem_limit_bytes=VMEM_LIMIT_BYTES),
        name="out_ffn",
    )(x2, o_nsa, o_ret, w_out, ln2, w_up, w_down)


def _rope_tables(S):
    half = HEAD_DIM // 2
    inv = ROPE_THETA ** (-jnp.arange(half, dtype=F32) / half)
    ang = jnp.arange(S).astype(F32)[:, None] * inv[None, :]
    cos, sin = jnp.cos(ang), jnp.sin(ang)
    reps = LANES // HEAD_DIM
    cos_t = jnp.tile(jnp.concatenate([cos, cos], axis=-1), (1, reps))
    sin_t = jnp.tile(jnp.concatenate([-sin, sin], axis=-1), (1, reps))
    return cos_t, sin_t


def _layer(h, ln1_w, w_in, q_norm_w, k_norm_w, cmp_pe_k, cmp_pe_v, cmp_wk1, cmp_wk2,
           cmp_wv1, cmp_wv2, ret_norm_w, w_out, ln2_w, w_up, w_down):
    B, S, D = h.shape
    N = B * S
    G = NSA_KV_HEADS
    x2 = h.reshape(N, D)

    order = _w_in_column_order()
    w_perm = jnp.where(order[None, :] >= 0, w_in[:, np.maximum(order, 0)], 0.0).astype(BF16)
    cos_t, sin_t = _rope_tables(S)
    reps = LANES // HEAD_DIM
    qn = jnp.tile(q_norm_w, reps)[None, :]
    kn = jnp.tile(k_norm_w, (1, reps))

    (q, kc, vc, ks, vs, kw, vw, gates, rq, rk, rv, rg) = _in_proj(
        x2, ln1_w[None, :], w_perm, cos_t, sin_t, qn, kn, B, S)

    seg_w = CMP_STRIDE * HEAD_DIM
    n_seg = S // CMP_STRIDE
    kcmp, vcmp = _compress(
        kc.reshape(B * G * n_seg, seg_w), vc.reshape(B * G * n_seg, seg_w),
        cmp_pe_k.reshape(2, seg_w), cmp_pe_v.reshape(2, seg_w),
        cmp_wk1.astype(BF16), cmp_wk2.astype(BF16), cmp_wv1.astype(BF16), cmp_wv2.astype(BF16))
    kcmp = kcmp.reshape(B, G, n_seg, HEAD_DIM)
    vcmp = vcmp.reshape(B, G, n_seg, HEAD_DIM)

    blk = np.arange(LANES)[None, :, None]
    key = (np.arange(S // TK)[:, None, None] * TK + np.arange(TK)[None, None, :]) // SEL_BLOCK
    expand = jnp.asarray((blk == key), dtype=BF16)

    o_nsa = _nsa(q, kcmp, vcmp, ks, vs, kw, vw, gates.reshape(B, S, G * LANES), expand)
    o_ret = _retention(rq.reshape(B, S, -1), rk.reshape(B, S, -1), rv.reshape(B, S, -1),
                       rg.reshape(B, S, -1), ret_norm_w.reshape(1, RET_V_WIDTH))

    out = _out_ffn(x2, o_nsa.reshape(N, -1), o_ret.reshape(N, -1), w_out.astype(BF16),
                   ln2_w[None, :], w_up.astype(BF16), w_down.astype(BF16))
    return out.reshape(B, S, D)


def kernel(x, ln1_w, w_in, q_norm_w, k_norm_w, cmp_pe_k, cmp_pe_v, cmp_wk1, cmp_wk2, cmp_wv1, cmp_wv2,
           ret_norm_w, w_out, ln2_w, w_up, w_down):
    h = x
    for l in range(ln1_w.shape[0]):
        h = _layer(h, ln1_w[l], w_in[l], q_norm_w[l], k_norm_w[l], cmp_pe_k[l], cmp_pe_v[l],
                   cmp_wk1[l], cmp_wk2[l], cmp_wv1[l], cmp_wv2[l], ret_norm_w[l], w_out[l],
                   ln2_w[l], w_up[l], w_down[l])
    return h
```

```python
import functools
import math

import jax
import jax.numpy as jnp
import numpy as np
from jax import lax
from jax.experimental import pallas as pl
from jax.experimental.pallas import tpu as pltpu

F32 = jnp.float32
BF16 = jnp.bfloat16

NSA_HEADS = 8
NSA_KV_HEADS = 2
NSA_GROUP = NSA_HEADS // NSA_KV_HEADS
HEAD_DIM = 64
CMP_BLOCK = 32
CMP_STRIDE = 16
CMP_HIDDEN = 256
SEL_BLOCK = 64
SEL_TOPK = 8
WINDOW = 256
FORCE_SCORE = 1.0e4
RET_HEADS = 4
RET_QK_DIM = 64
RET_V_DIM = 128
RET_CHUNK = 128
ROPE_THETA = 10000.0
EPS = 1e-6
NEG_INF = -1.0e30

NSA_Q_DIM = NSA_HEADS * HEAD_DIM
NSA_KV_DIM = NSA_KV_HEADS * HEAD_DIM
NSA_GATE_DIM = 3 * NSA_HEADS
RET_QK_WIDTH = RET_HEADS * RET_QK_DIM
RET_V_WIDTH = RET_HEADS * RET_V_DIM

LANES = 128
VMEM_LIMIT_BYTES = 56 * 1024 * 1024

TM_PROJ = 512
TQ = 128
TK = 256
TM_FFN = 512
FF_CHUNK = 512
CMP_ROWS = 512


def _dot(a, b):
    return jnp.dot(a, b, preferred_element_type=F32)


def _dot_nt(a, b):
    return lax.dot_general(a, b, (((1,), (1,)), ((), ())), preferred_element_type=F32)


def _sigmoid(x):
    return 1.0 / (1.0 + jnp.exp(-x))


_Q0 = 0
_K0 = _Q0 + NSA_Q_DIM
_V0 = _K0 + 3 * NSA_KV_DIM
_G0 = _V0 + 3 * NSA_KV_DIM
_RQ0 = _G0 + NSA_KV_HEADS * LANES
_RK0 = _RQ0 + RET_QK_WIDTH
_RV0 = _RK0 + RET_QK_WIDTH
_RG0 = _RV0 + RET_V_WIDTH
_W_IN_COLS = _RG0 + RET_V_WIDTH


def _w_in_column_order():
    sizes = [NSA_Q_DIM] + [NSA_KV_DIM] * 6 + [NSA_GATE_DIM, RET_QK_WIDTH, RET_QK_WIDTH, RET_V_WIDTH, RET_V_WIDTH]
    off = np.concatenate([[0], np.cumsum(sizes)])
    (q, kc, vc, ks, vs, kw, vw, gate, rq, rk, rv, rg) = [np.arange(off[i], off[i + 1]) for i in range(12)]
    gates = -np.ones((NSA_KV_HEADS, LANES), np.int64)
    for g in range(NSA_KV_HEADS):
        for br in range(3):
            for r in range(NSA_GROUP):
                gates[g, br * NSA_GROUP + r] = gate[br * NSA_HEADS + g * NSA_GROUP + r]
    order = np.concatenate([q, kc, ks, kw, vc, vs, vw, gates.reshape(-1), rq, rk, rv, rg])
    assert order.shape[0] == _W_IN_COLS
    return order


def _in_proj_kernel(x_ref, ln_ref, w_ref, cos_ref, sin_ref, qn_ref, kn_ref,
                    q_ref, kc_ref, vc_ref, ks_ref, vs_ref, kw_ref, vw_ref, gate_ref,
                    rq_ref, rk_ref, rv_ref, rg_ref):
    x = x_ref[...]
    ms = jnp.mean(x * x, axis=-1, keepdims=True)
    xn = (x * lax.rsqrt(ms + EPS) * ln_ref[...]).astype(BF16)
    cos = cos_ref[...]
    sin = sin_ref[...]
    lane = lax.broadcasted_iota(jnp.int32, (1, LANES), 1)
    low_half = (lane & (HEAD_DIM // 2)) == 0
    first_head = lane < HEAD_DIM

    def proj(a, b):
        return _dot(xn, w_ref[:, a:b])

    def rope(t):
        swapped = jnp.where(low_half, pltpu.roll(t, LANES - HEAD_DIM // 2, 1), pltpu.roll(t, HEAD_DIM // 2, 1))
        return t * cos + swapped * sin

    def head_norm(t, w):
        t2 = t * t
        s0 = jnp.sum(jnp.where(first_head, t2, 0.0), axis=-1, keepdims=True)
        s1 = jnp.sum(jnp.where(first_head, 0.0, t2), axis=-1, keepdims=True)
        msq = jnp.where(first_head, s0, s1) * (1.0 / HEAD_DIM)
        return t * lax.rsqrt(msq + EPS) * w

    def tiles(sec):
        return [sec[:, c * LANES:(c + 1) * LANES] for c in range(sec.shape[1] // LANES)]

    qn = qn_ref[...]
    for c, t in enumerate(tiles(proj(_Q0, _K0))):
        t = (rope(head_norm(t, qn)) * (HEAD_DIM ** -0.5)).astype(BF16)
        for j in range(2):
            h = 2 * c + j
            q_ref[0, h // NSA_GROUP, h % NSA_GROUP] = t[:, j * HEAD_DIM:(j + 1) * HEAD_DIM]

    ksec = tiles(proj(_K0, _V0))
    kc = rope(head_norm(ksec[0], kn_ref[0:1, :]))
    ks = rope(head_norm(ksec[1], kn_ref[1:2, :])).astype(BF16)
    kw = rope(head_norm(ksec[2], kn_ref[2:3, :])).astype(BF16)
    vsec = tiles(proj(_V0, _G0))
    for g in range(NSA_KV_HEADS):
        sl = slice(g * HEAD_DIM, (g + 1) * HEAD_DIM)
        kc_ref[0, g] = kc[:, sl]
        vc_ref[0, g] = vsec[0][:, sl]
        ks_ref[0, g] = ks[:, sl]
        kw_ref[0, g] = kw[:, sl]

    def store_transposed(v, out_ref):
        chunk = out_ref.shape[-1]
        for c in range(v.shape[0] // chunk):
            vt = v[c * chunk:(c + 1) * chunk, :].T.astype(BF16)
            for g in range(NSA_KV_HEADS):
                out_ref[0, g, c] = vt[g * HEAD_DIM:(g + 1) * HEAD_DIM, :]

    store_transposed(vsec[1], vs_ref)
    store_transposed(vsec[2], vw_ref)

    gate_ref[...] = proj(_G0, _RQ0)

    for c, t in enumerate(tiles(proj(_RQ0, _RK0))):
        rq_ref[:, c * LANES:(c + 1) * LANES] = rope(t).astype(BF16)
    for c, t in enumerate(tiles(proj(_RK0, _RV0))):
        rk_ref[:, c * LANES:(c + 1) * LANES] = (rope(t) * (RET_QK_DIM ** -0.5)).astype(BF16)
    rv_ref[...] = proj(_RV0, _RG0).astype(BF16)
    rg_ref[...] = proj(_RG0, _W_IN_COLS).astype(BF16)


def _in_proj(x2, ln1, w_perm, cos, sin, qn, kn, B, S):
    N, D = x2.shape
    tm = TM_PROJ
    spt = S // tm
    G, R, dh = NSA_KV_HEADS, NSA_GROUP, HEAD_DIM

    def row(i):
        return (i, 0)

    def hm(i):
        return (i // spt, 0, i % spt, 0)

    hm_spec = pl.BlockSpec((1, G, tm, dh), hm)
    out_shape = [
        jax.ShapeDtypeStruct((B, G, R, S, dh), BF16),
        jax.ShapeDtypeStruct((B, G, S, dh), F32),
        jax.ShapeDtypeStruct((B, G, S, dh), F32),
        jax.ShapeDtypeStruct((B, G, S, dh), BF16),
        jax.ShapeDtypeStruct((B, G, S // TK, dh, TK), BF16),
        jax.ShapeDtypeStruct((B, G, S, dh), BF16),
        jax.ShapeDtypeStruct((B, G, S // TQ, dh, TQ), BF16),
        jax.ShapeDtypeStruct((N, G * LANES), F32),
        jax.ShapeDtypeStruct((N, RET_QK_WIDTH), BF16),
        jax.ShapeDtypeStruct((N, RET_QK_WIDTH), BF16),
        jax.ShapeDtypeStruct((N, RET_V_WIDTH), BF16),
        jax.ShapeDtypeStruct((N, RET_V_WIDTH), BF16),
    ]
    out_specs = [
        pl.BlockSpec((1, G, R, tm, dh), lambda i: (i // spt, 0, 0, i % spt, 0)),
        hm_spec, hm_spec, hm_spec,
        pl.BlockSpec((1, G, tm // TK, dh, TK), lambda i: (i // spt, 0, i % spt, 0, 0)),
        hm_spec,
        pl.BlockSpec((1, G, tm // TQ, dh, TQ), lambda i: (i // spt, 0, i % spt, 0, 0)),
        pl.BlockSpec((tm, G * LANES), row),
        pl.BlockSpec((tm, RET_QK_WIDTH), row),
        pl.BlockSpec((tm, RET_QK_WIDTH), row),
        pl.BlockSpec((tm, RET_V_WIDTH), row),
        pl.BlockSpec((tm, RET_V_WIDTH), row),
    ]
    const = lambda i: (0, 0)
    in_specs = [
        pl.BlockSpec((tm, D), row),
        pl.BlockSpec((1, D), const),
        pl.BlockSpec((D, _W_IN_COLS), const, pipeline_mode=pl.Buffered(1)),
        pl.BlockSpec((tm, LANES), lambda i: (i % spt, 0)),
        pl.BlockSpec((tm, LANES), lambda i: (i % spt, 0)),
        pl.BlockSpec((1, LANES), const),
        pl.BlockSpec((3, LANES), const),
    ]
    return pl.pallas_call(
        _in_proj_kernel,
        grid=(N // tm,),
        in_specs=in_specs,
        out_specs=out_specs,
        out_shape=out_shape,
        compiler_params=pltpu.CompilerParams(
            dimension_semantics=("parallel",), vmem_limit_bytes=VMEM_LIMIT_BYTES),
        name="in_proj",
    )(x2, ln1, w_perm, cos, sin, qn, kn)


def _compress_kernel(k_ref, v_ref, pek_ref, pev_ref, wk1_ref, wk2_ref, wv1_ref, wv2_ref, ko_ref, vo_ref):
    half = CMP_STRIDE * HEAD_DIM

    def run(seg_ref, pe_ref, w1_ref, w2_ref):
        seg = seg_ref[...]
        rows = seg.shape[0]
        top = (seg + pe_ref[0:1, :]).astype(BF16)
        bot = (seg + pe_ref[1:2, :]).astype(BF16)
        a = _dot(top, w1_ref[0:half, :])
        b = _dot(bot, w1_ref[half:2 * half, :])
        h = a + pltpu.roll(b, rows - 1, 0)
        h = (h * _sigmoid(h)).astype(BF16)
        return _dot(h, w2_ref[...])

    ko_ref[...] = run(k_ref, pek_ref, wk1_ref, wk2_ref).astype(BF16)
    vo_ref[...] = run(v_ref, pev_ref, wv1_ref, wv2_ref).T[0:HEAD_DIM, :].astype(BF16)


def _compress(kc_seg, vc_seg, pek, pev, wk1, wk2, wv1, wv2):
    rows, width = kc_seg.shape
    tr = min(CMP_ROWS, rows)
    const = lambda i: (0, 0)
    seg_spec = pl.BlockSpec((tr, width), lambda i: (i, 0))
    out_spec = pl.BlockSpec((tr, HEAD_DIM), lambda i: (i, 0))
    return pl.pallas_call(
        _compress_kernel,
        grid=(rows // tr,),
        in_specs=[seg_spec, seg_spec,
                  pl.BlockSpec((2, width), const), pl.BlockSpec((2, width), const),
                  pl.BlockSpec((2 * width, CMP_HIDDEN), const), pl.BlockSpec((CMP_HIDDEN, HEAD_DIM), const),
                  pl.BlockSpec((2 * width, CMP_HIDDEN), const), pl.BlockSpec((CMP_HIDDEN, LANES), const)],
        out_specs=[out_spec, pl.BlockSpec((HEAD_DIM, tr), lambda i: (0, i))],
        out_shape=[jax.ShapeDtypeStruct((rows, HEAD_DIM), BF16), jax.ShapeDtypeStruct((HEAD_DIM, rows), BF16)],
        compiler_params=pltpu.CompilerParams(
            dimension_semantics=("parallel",), vmem_limit_bytes=VMEM_LIMIT_BYTES),
        name="compress",
    )(kc_seg, vc_seg, pek, pev, wk1, wk2, wv1, wv2)


def _nsa_kernel(q_ref, kcmp_ref, vcmp_ref, ks_ref, vs_ref, kw_ref, vw_ref, gate_ref, expand_ref,
                o_ref, m_ref, l_ref, acc_ref):
    R = NSA_GROUP
    n_cp = kcmp_ref.shape[2]
    S = ks_ref.shape[2]
    n_cmp = (S - CMP_BLOCK) // CMP_STRIDE + 1
    n_blk = S // SEL_BLOCK
    q0 = pl.program_id(2) * TQ
    q = q_ref[0, 0].reshape(R * TQ, HEAD_DIM)
    t_q = q0 + lax.broadcasted_iota(jnp.int32, (1, TQ), 1)

    def heads(x):
        return [x[:, r * TQ:(r + 1) * TQ] for r in range(R)]

    def masked_softmax(s_heads, mask):
        out = []
        for s in s_heads:
            s = jnp.where(mask, s, NEG_INF)
            e = jnp.exp(s - jnp.max(s, axis=0, keepdims=True))
            inv = 1.0 / jnp.sum(e, axis=0, keepdims=True)
            out.append(jnp.where(mask, e * inv, 0.0))
        return out

    s_c = _dot_nt(kcmp_ref[0, 0], q)
    c_idx = lax.broadcasted_iota(jnp.int32, (n_cp, 1), 0)
    cmask = ((c_idx * CMP_STRIDE + (CMP_BLOCK - 1)) <= t_q) & (c_idx < n_cmp)
    p_c = masked_softmax(heads(s_c), cmask)
    o_cmp = _dot(vcmp_ref[...], jnp.concatenate(p_c, axis=1).astype(BF16))
    p_sum = p_c[0]
    for r in range(1, R):
        p_sum = p_sum + p_c[r]

    jj = lax.broadcasted_iota(jnp.int32, (n_blk, n_cp), 0)
    cc = lax.broadcasted_iota(jnp.int32, (n_blk, n_cp), 1)
    overlap = ((cc * CMP_STRIDE < (jj + 1) * SEL_BLOCK) & (cc * CMP_STRIDE + CMP_BLOCK > jj * SEL_BLOCK)
               & (cc < n_cmp)).astype(BF16)
    p_hi = p_sum.astype(BF16)
    p_lo = (p_sum - p_hi.astype(F32)).astype(BF16)
    p_slc = _dot(overlap, p_hi) + _dot(overlap, p_lo)
    jb = lax.broadcasted_iota(jnp.int32, (n_blk, 1), 0)
    cur = jnp.right_shift(t_q, int(math.log2(SEL_BLOCK)))
    forced = (jb == 0) | (jb == cur) | (jb == cur - 1)
    valid = jb <= cur
    score = jnp.where(forced, FORCE_SCORE, jnp.where(valid, p_slc, -1.0))
    rank = jnp.zeros((n_blk, TQ), jnp.int32)
    for i in range(n_blk):
        row = score[i:i + 1, :]
        before = (row > score) | ((row == score) & (jb > i))
        rank = rank + before.astype(jnp.int32)
    sel = (rank < min(SEL_TOPK, n_blk)).astype(BF16)
    sel = jnp.concatenate([sel, jnp.zeros((LANES - n_blk, TQ), BF16)], axis=0)

    m_ref[...] = jnp.full(m_ref.shape, NEG_INF, F32)
    l_ref[...] = jnp.zeros(l_ref.shape, F32)
    acc_ref[...] = jnp.zeros(acc_ref.shape, F32)
    n_kt = (q0 + TQ + TK - 1) // TK

    def sel_step(kt, carry):
        k0 = pl.multiple_of(kt * TK, TK)
        k = ks_ref[0, 0, pl.ds(k0, TK), :]
        s = _dot_nt(k, q)
        chosen = _dot(expand_ref[kt], sel) > 0.5
        kpos = k0 + lax.broadcasted_iota(jnp.int32, (TK, 1), 0)
        mask = chosen & (kpos <= t_q)
        s = jnp.concatenate([jnp.where(mask, sh, NEG_INF) for sh in heads(s)], axis=1)
        m_old = m_ref[...]
        m_new = jnp.maximum(m_old, jnp.max(s, axis=0, keepdims=True))
        alpha = jnp.exp(m_old - m_new)
        p = jnp.exp(s - m_new)
        l_ref[...] = alpha * l_ref[...] + jnp.sum(p, axis=0, keepdims=True)
        acc_ref[...] = alpha * acc_ref[...] + _dot(vs_ref[0, 0, kt], p.astype(BF16))
        m_ref[...] = m_new
        return carry

    lax.fori_loop(0, n_kt, sel_step, 0)

    n_wc = (TQ + WINDOW) // TQ
    c0 = jnp.maximum(pl.program_id(2) - WINDOW // TQ, 0)
    w0 = pl.multiple_of(c0 * TQ, TQ)
    s_w = _dot_nt(kw_ref[0, 0, pl.ds(w0, n_wc * TQ), :], q)
    diff = t_q - (w0 + lax.broadcasted_iota(jnp.int32, (n_wc * TQ, 1), 0))
    wmask = (diff >= 0) & (diff < WINDOW)
    p_w = jnp.concatenate(masked_softmax(heads(s_w), wmask), axis=1).astype(BF16)
    o_win = _dot(vw_ref[0, 0, c0], p_w[0:TQ])
    for j in range(1, n_wc):
        o_win = o_win + _dot(vw_ref[0, 0, c0 + j], p_w[j * TQ:(j + 1) * TQ])

    gates = _sigmoid(gate_ref[0].T)
    o_sel = acc_ref[...] * (1.0 / l_ref[...])
    outs = []
    for r, (oc, os_, ow) in enumerate(zip(heads(o_cmp), heads(o_sel), heads(o_win))):
        outs.append(gates[r:r + 1] * oc + gates[R + r:R + r + 1] * os_ + gates[2 * R + r:2 * R + r + 1] * ow)
    o_ref[0] = jnp.concatenate(outs, axis=0).T.astype(BF16)


def _nsa(q, kcmp, vcmp, ks, vs, kw, vw, gates, expand):
    B, G, R, S, dh = q.shape
    n_cp = kcmp.shape[2]
    k_spec = pl.BlockSpec((1, 1, S, dh), lambda b, g, i: (b, g, 0, 0))
    vt_spec = lambda a: pl.BlockSpec((1, 1) + a.shape[2:], lambda b, g, i: (b, g, 0, 0, 0))
    return pl.pallas_call(
        _nsa_kernel,
        grid=(B, G, S // TQ),
        in_specs=[
            pl.BlockSpec((1, 1, R, TQ, dh), lambda b, g, i: (b, g, 0, i, 0)),
            pl.BlockSpec((1, 1, n_cp, dh), lambda b, g, i: (b, g, 0, 0)),
            pl.BlockSpec((dh, n_cp), lambda b, g, i: (0, b * G + g)),
            k_spec, vt_spec(vs), k_spec, vt_spec(vw),
            pl.BlockSpec((1, TQ, LANES), lambda b, g, i: (b, i, g)),
            pl.BlockSpec(expand.shape, lambda b, g, i: (0, 0, 0)),
        ],
        out_specs=pl.BlockSpec((1, TQ, R * dh), lambda b, g, i: (b, i, g)),
        out_shape=jax.ShapeDtypeStruct((B, S, G * R * dh), BF16),
        scratch_shapes=[pltpu.VMEM((1, R * TQ), F32), pltpu.VMEM((1, R * TQ), F32),
                        pltpu.VMEM((dh, R * TQ), F32)],
        compiler_params=pltpu.CompilerParams(
            dimension_semantics=("parallel", "parallel", "arbitrary"), vmem_limit_bytes=VMEM_LIMIT_BYTES),
        name="nsa_attention",
    )(q, kcmp, vcmp, ks, vs, kw, vw, gates, expand)


def _retention_kernel(q_ref, k_ref, v_ref, g_ref, w_ref, o_ref, state_ref):
    C = RET_CHUNK

    @pl.when(pl.program_id(1) == 0)
    def _():
        state_ref[...] = jnp.zeros(state_ref.shape, F32)

    i_col = lax.broadcasted_iota(jnp.int32, (C, 1), 0)
    d_int = lax.broadcasted_iota(jnp.int32, (C, C), 0) - lax.broadcasted_iota(jnp.int32, (C, C), 1)
    for h in range(RET_HEADS):
        log_gamma = math.log(1.0 - 2.0 ** (-5.0 - h))
        dmat = jnp.where(d_int >= 0, jnp.exp(log_gamma * jnp.maximum(d_int, 0).astype(F32)), 0.0)
        xi = jnp.exp(log_gamma * (i_col + 1).astype(F32))
        zeta = jnp.exp(log_gamma * (C - 1 - i_col).astype(F32))
        gamma_c = math.exp(log_gamma * C)

        qs = slice(h * RET_QK_DIM, (h + 1) * RET_QK_DIM)
        vsl = slice(h * RET_V_DIM, (h + 1) * RET_V_DIM)
        q = q_ref[0, :, qs]
        k = k_ref[0, :, qs]
        v = v_ref[0, :, vsl]
        state = state_ref[h]
        inner = _dot_nt(q, k) * dmat
        y = _dot(inner.astype(BF16), v) + _dot(q, state.astype(BF16)) * xi
        kz = (k.astype(F32) * zeta).T.astype(BF16)
        state_ref[h] = gamma_c * state + _dot(kz, v)

        mu = jnp.mean(y, axis=-1, keepdims=True)
        yc = y - mu
        var = jnp.mean(yc * yc, axis=-1, keepdims=True)
        yn = yc * lax.rsqrt(var + EPS) * w_ref[:, vsl]
        gate = g_ref[0, :, vsl].astype(F32)
        o_ref[0, :, vsl] = (gate * _sigmoid(gate) * yn).astype(BF16)


def _retention(rq, rk, rv, rg, w):
    B, S, _ = rq.shape
    C = RET_CHUNK
    qk_spec = pl.BlockSpec((1, C, RET_QK_WIDTH), lambda b, c: (b, c, 0))
    v_spec = pl.BlockSpec((1, C, RET_V_WIDTH), lambda b, c: (b, c, 0))
    return pl.pallas_call(
        _retention_kernel,
        grid=(B, S // C),
        in_specs=[qk_spec, qk_spec, v_spec, v_spec, pl.BlockSpec((1, RET_V_WIDTH), lambda b, c: (0, 0))],
        out_specs=v_spec,
        out_shape=jax.ShapeDtypeStruct((B, S, RET_V_WIDTH), BF16),
        scratch_shapes=[pltpu.VMEM((RET_HEADS, RET_QK_DIM, RET_V_DIM), F32)],
        compiler_params=pltpu.CompilerParams(
            dimension_semantics=("parallel", "arbitrary"), vmem_limit_bytes=VMEM_LIMIT_BYTES),
        name="retention",
    )(rq, rk, rv, rg, w)


def _out_ffn_kernel(x_ref, a_ref, r_ref, wo_ref, ln_ref, wu_ref, wd_ref, o_ref):
    na = a_ref.shape[1]
    mix = _dot(a_ref[...], wo_ref[0:na, :]) + _dot(r_ref[...], wo_ref[na:, :])
    h = x_ref[...] + mix
    ms = jnp.mean(h * h, axis=-1, keepdims=True)
    hn = (h * lax.rsqrt(ms + EPS) * ln_ref[...]).astype(BF16)
    d_ff = wu_ref.shape[1]
    acc = None
    for f in range(d_ff // FF_CHUNK):
        cols = slice(f * FF_CHUNK, (f + 1) * FF_CHUNK)
        u = jnp.maximum(_dot(hn, wu_ref[:, cols]), 0.0)
        d = _dot((u * u).astype(BF16), wd_ref[cols, :])
        acc = d if acc is None else acc + d
    o_ref[...] = h + acc


def _out_ffn(x2, o_nsa, o_ret, w_out, ln2, w_up, w_down):
    N, D = x2.shape
    tm = TM_FFN
    row = lambda i: (i, 0)
    const = lambda i: (0, 0)
    resident = functools.partial(pl.BlockSpec, index_map=const, pipeline_mode=pl.Buffered(1))
    return pl.pallas_call(
        _out_ffn_kernel,
        grid=(N // tm,),
        in_specs=[
            pl.BlockSpec((tm, D), row),
            pl.BlockSpec((tm, o_nsa.shape[1]), row),
            pl.BlockSpec((tm, o_ret.shape[1]), row),
            resident(w_out.shape),
            pl.BlockSpec((1, D), const),
            resident(w_up.shape),
            resident(w_down.shape),
        ],
        out_specs=pl.BlockSpec((tm, D), row),
        out_shape=jax.ShapeDtypeStruct((N, D), F32),
        compiler_params=pltpu.CompilerParams(
            dimension_semantics=("parallel",), vmem_limit_bytes=VMEM_LIMIT_BYTES),
        name="out_ffn",
    )(x2, o_nsa, o_ret, w_out, ln2, w_up, w_down)


def _rope_tables(S):
    half = HEAD_DIM // 2
    inv = ROPE_THETA ** (-jnp.arange(half, dtype=F32) / half)
    ang = jnp.arange(S).astype(F32)[:, None] * inv[None, :]
    cos, sin = jnp.cos(ang), jnp.sin(ang)
    reps = LANES // HEAD_DIM
    cos_t = jnp.tile(jnp.concatenate([cos, cos], axis=-1), (1, reps))
    sin_t = jnp.tile(jnp.concatenate([-sin, sin], axis=-1), (1, reps))
    return cos_t, sin_t


def _layer(h, ln1_w, w_in, q_norm_w, k_norm_w, cmp_pe_k, cmp_pe_v, cmp_wk1, cmp_wk2,
           cmp_wv1, cmp_wv2, ret_norm_w, w_out, ln2_w, w_up, w_down):
    B, S, D = h.shape
    N = B * S
    G = NSA_KV_HEADS
    x2 = h.reshape(N, D)

    order = _w_in_column_order()
    w_perm = jnp.where(order[None, :] >= 0, w_in[:, np.maximum(order, 0)], 0.0).astype(BF16)
    cos_t, sin_t = _rope_tables(S)
    reps = LANES // HEAD_DIM
    qn = jnp.tile(q_norm_w, reps)[None, :]
    kn = jnp.tile(k_norm_w, (1, reps))

    (q, kc, vc, ks, vs, kw, vw, gates, rq, rk, rv, rg) = _in_proj(
        x2, ln1_w[None, :], w_perm, cos_t, sin_t, qn, kn, B, S)

    seg_w = CMP_STRIDE * HEAD_DIM
    n_seg = S // CMP_STRIDE
    kcmp, vcmp = _compress(
        kc.reshape(B * G * n_seg, seg_w), vc.reshape(B * G * n_seg, seg_w),
        cmp_pe_k.reshape(2, seg_w), cmp_pe_v.reshape(2, seg_w),
        cmp_wk1.astype(BF16), cmp_wk2.astype(BF16), cmp_wv1.astype(BF16),
        jnp.pad(cmp_wv2, ((0, 0), (0, LANES - HEAD_DIM))).astype(BF16))
    kcmp = kcmp.reshape(B, G, n_seg, HEAD_DIM)

    key = (np.arange(S // TK)[:, None, None] * TK + np.arange(TK)[None, :, None]) // SEL_BLOCK
    blk = np.arange(LANES)[None, None, :]
    expand = jnp.asarray((blk == key), dtype=BF16)

    o_nsa = _nsa(q, kcmp, vcmp, ks, vs, kw, vw, gates.reshape(B, S, G * LANES), expand)
    o_ret = _retention(rq.reshape(B, S, -1), rk.reshape(B, S, -1), rv.reshape(B, S, -1),
                       rg.reshape(B, S, -1), ret_norm_w.reshape(1, RET_V_WIDTH))

    out = _out_ffn(x2, o_nsa.reshape(N, -1), o_ret.reshape(N, -1), w_out.astype(BF16),
                   ln2_w[None, :], w_up.astype(BF16), w_down.astype(BF16))
    return out.reshape(B, S, D)


def kernel(x, ln1_w, w_in, q_norm_w, k_norm_w, cmp_pe_k, cmp_pe_v, cmp_wk1, cmp_wk2, cmp_wv1, cmp_wv2,
           ret_norm_w, w_out, ln2_w, w_up, w_down):
    h = x
    for l in range(ln1_w.shape[0]):
        h = _layer(h, ln1_w[l], w_in[l], q_norm_w[l], k_norm_w[l], cmp_pe_k[l], cmp_pe_v[l],
                   cmp_wk1[l], cmp_wk2[l], cmp_wv1[l], cmp_wv2[l], ret_norm_w[l], w_out[l],
                   ln2_w[l], w_up[l], w_down[l])
    return h
```

```python
import functools
import math

import jax
import jax.numpy as jnp
import numpy as np
from jax import lax
from jax.experimental import pallas as pl
from jax.experimental.pallas import tpu as pltpu

F32 = jnp.float32
BF16 = jnp.bfloat16

NSA_HEADS = 8
NSA_KV_HEADS = 2
NSA_GROUP = NSA_HEADS // NSA_KV_HEADS
HEAD_DIM = 64
CMP_BLOCK = 32
CMP_STRIDE = 16
CMP_HIDDEN = 256
SEL_BLOCK = 64
SEL_TOPK = 8
WINDOW = 256
FORCE_SCORE = 1.0e4
RET_HEADS = 4
RET_QK_DIM = 64
RET_V_DIM = 128
RET_CHUNK = 128
ROPE_THETA = 10000.0
EPS = 1e-6
NEG_INF = -1.0e30
SEL_BIAS = 2.0 ** 100

NSA_Q_DIM = NSA_HEADS * HEAD_DIM
NSA_KV_DIM = NSA_KV_HEADS * HEAD_DIM
NSA_GATE_DIM = 3 * NSA_HEADS
RET_QK_WIDTH = RET_HEADS * RET_QK_DIM
RET_V_WIDTH = RET_HEADS * RET_V_DIM

LANES = 128
VMEM_LIMIT_BYTES = 56 * 1024 * 1024

TM_PROJ = 512
TQ = 256
TK = 256
KV_CHUNK = 128
TM_FFN = 512
FF_CHUNK = 512
CMP_ROWS = 512


def _dot(a, b):
    return jnp.dot(a, b, preferred_element_type=F32)


def _dot_nt(a, b):
    return lax.dot_general(a, b, (((1,), (1,)), ((), ())), preferred_element_type=F32)


def _sigmoid(x):
    return 1.0 / (1.0 + jnp.exp(-x))


_Q0 = 0
_K0 = _Q0 + NSA_Q_DIM
_V0 = _K0 + 3 * NSA_KV_DIM
_G0 = _V0 + 3 * NSA_KV_DIM
_RQ0 = _G0 + NSA_KV_HEADS * LANES
_RK0 = _RQ0 + RET_QK_WIDTH
_RV0 = _RK0 + RET_QK_WIDTH
_RG0 = _RV0 + RET_V_WIDTH
_W_IN_COLS = _RG0 + RET_V_WIDTH


def _w_in_column_order():
    sizes = [NSA_Q_DIM] + [NSA_KV_DIM] * 6 + [NSA_GATE_DIM, RET_QK_WIDTH, RET_QK_WIDTH, RET_V_WIDTH, RET_V_WIDTH]
    off = np.concatenate([[0], np.cumsum(sizes)])
    (q, kc, vc, ks, vs, kw, vw, gate, rq, rk, rv, rg) = [np.arange(off[i], off[i + 1]) for i in range(12)]
    gates = -np.ones((NSA_KV_HEADS, LANES), np.int64)
    for g in range(NSA_KV_HEADS):
        for br in range(3):
            for r in range(NSA_GROUP):
                gates[g, br * NSA_GROUP + r] = gate[br * NSA_HEADS + g * NSA_GROUP + r]
    order = np.concatenate([q, kc, ks, kw, vc, vs, vw, gates.reshape(-1), rq, rk, rv, rg])
    assert order.shape[0] == _W_IN_COLS
    return order


def _in_proj_kernel(x_ref, ln_ref, w_ref, cos_ref, sin_ref, qn_ref, kn_ref,
                    q_ref, kc_ref, vc_ref, ks_ref, vs_ref, kw_ref, vw_ref, gate_ref,
                    rq_ref, rk_ref, rv_ref, rg_ref, *, tiles_per_seq):
    x = x_ref[...]
    ms = jnp.mean(x * x, axis=-1, keepdims=True)
    xn = (x * lax.rsqrt(ms + EPS) * ln_ref[...]).astype(BF16)
    cos = cos_ref[...]
    sin = sin_ref[...]
    lane = lax.broadcasted_iota(jnp.int32, (1, LANES), 1)
    low_half = (lane & (HEAD_DIM // 2)) == 0
    first_head = lane < HEAD_DIM

    def proj(a, b):
        return _dot(xn, w_ref[:, a:b])

    def rope(t):
        swapped = jnp.where(low_half, pltpu.roll(t, LANES - HEAD_DIM // 2, 1), pltpu.roll(t, HEAD_DIM // 2, 1))
        return t * cos + swapped * sin

    def head_norm(t, w):
        t2 = t * t
        s0 = jnp.sum(jnp.where(first_head, t2, 0.0), axis=-1, keepdims=True)
        s1 = jnp.sum(jnp.where(first_head, 0.0, t2), axis=-1, keepdims=True)
        msq = jnp.where(first_head, s0, s1) * (1.0 / HEAD_DIM)
        return t * lax.rsqrt(msq + EPS) * w

    def tiles(sec):
        return [sec[:, c * LANES:(c + 1) * LANES] for c in range(sec.shape[1] // LANES)]

    qn = qn_ref[...]
    def split_heads(t, fill):
        return [jnp.where(first_head, t, fill), jnp.where(first_head, pltpu.roll(t, HEAD_DIM, 1), fill)]

    for c, t in enumerate(tiles(proj(_Q0, _K0))):
        t = rope(head_norm(t, qn)) * (HEAD_DIM ** -0.5 * math.log2(math.e))
        for j, tj in enumerate(split_heads(t, 0.0)):
            h = 2 * c + j
            q_ref[0, h // NSA_GROUP, h % NSA_GROUP] = tj.astype(BF16)

    ksec = tiles(proj(_K0, _V0))
    kc = rope(head_norm(ksec[0], kn_ref[0:1, :]))
    ks = rope(head_norm(ksec[1], kn_ref[1:2, :]))
    kw = rope(head_norm(ksec[2], kn_ref[2:3, :])).astype(BF16)
    vsec = tiles(proj(_V0, _G0))
    tm = x.shape[0]
    s0 = (pl.program_id(0) % tiles_per_seq) * tm
    blk = jnp.right_shift(s0 + lax.broadcasted_iota(jnp.int32, (tm, 1), 0), int(math.log2(SEL_BLOCK)))
    onehot = (lane - HEAD_DIM == blk).astype(F32)
    ks_aug = split_heads(ks, onehot)
    for g in range(NSA_KV_HEADS):
        sl = slice(g * HEAD_DIM, (g + 1) * HEAD_DIM)
        kc_ref[0, g] = kc[:, sl]
        vc_ref[0, g] = vsec[0][:, sl]
        ks_ref[0, g] = ks_aug[g].astype(BF16)
        kw_ref[0, g] = kw[:, sl]

    def store_transposed(v, out_ref):
        chunk = out_ref.shape[-1]
        for c in range(v.shape[0] // chunk):
            vt = v[c * chunk:(c + 1) * chunk, :].T.astype(BF16)
            for g in range(NSA_KV_HEADS):
                out_ref[0, g, c] = vt[g * HEAD_DIM:(g + 1) * HEAD_DIM, :]

    store_transposed(vsec[1], vs_ref)
    store_transposed(vsec[2], vw_ref)

    gate_ref[...] = proj(_G0, _RQ0)

    for c, t in enumerate(tiles(proj(_RQ0, _RK0))):
        rq_ref[:, c * LANES:(c + 1) * LANES] = rope(t).astype(BF16)
    for c, t in enumerate(tiles(proj(_RK0, _RV0))):
        rk_ref[:, c * LANES:(c + 1) * LANES] = (rope(t) * (RET_QK_DIM ** -0.5)).astype(BF16)
    rv_ref[...] = proj(_RV0, _RG0).astype(BF16)
    rg_ref[...] = proj(_RG0, _W_IN_COLS).astype(BF16)


def _in_proj(x2, ln1, w_perm, cos, sin, qn, kn, B, S):
    N, D = x2.shape
    tm = TM_PROJ
    spt = S // tm
    G, R, dh = NSA_KV_HEADS, NSA_GROUP, HEAD_DIM

    def row(i):
        return (i, 0)

    def hm(i):
        return (i // spt, 0, i % spt, 0)

    hm_spec = pl.BlockSpec((1, G, tm, dh), hm)
    out_shape = [
        jax.ShapeDtypeStruct((B, G, R, S, LANES), BF16),
        jax.ShapeDtypeStruct((B, G, S, dh), F32),
        jax.ShapeDtypeStruct((B, G, S, dh), F32),
        jax.ShapeDtypeStruct((B, G, S, LANES), BF16),
        jax.ShapeDtypeStruct((B, G, S // TK, dh, TK), BF16),
        jax.ShapeDtypeStruct((B, G, S, dh), BF16),
        jax.ShapeDtypeStruct((B, G, S // KV_CHUNK, dh, KV_CHUNK), BF16),
        jax.ShapeDtypeStruct((N, G * LANES), F32),
        jax.ShapeDtypeStruct((N, RET_QK_WIDTH), BF16),
        jax.ShapeDtypeStruct((N, RET_QK_WIDTH), BF16),
        jax.ShapeDtypeStruct((N, RET_V_WIDTH), BF16),
        jax.ShapeDtypeStruct((N, RET_V_WIDTH), BF16),
    ]
    out_specs = [
        pl.BlockSpec((1, G, R, tm, LANES), lambda i: (i // spt, 0, 0, i % spt, 0)),
        hm_spec, hm_spec, pl.BlockSpec((1, G, tm, LANES), hm),
        pl.BlockSpec((1, G, tm // TK, dh, TK), lambda i: (i // spt, 0, i % spt, 0, 0)),
        hm_spec,
        pl.BlockSpec((1, G, tm // KV_CHUNK, dh, KV_CHUNK), lambda i: (i // spt, 0, i % spt, 0, 0)),
        pl.BlockSpec((tm, G * LANES), row),
        pl.BlockSpec((tm, RET_QK_WIDTH), row),
        pl.BlockSpec((tm, RET_QK_WIDTH), row),
        pl.BlockSpec((tm, RET_V_WIDTH), row),
        pl.BlockSpec((tm, RET_V_WIDTH), row),
    ]
    const = lambda i: (0, 0)
    in_specs = [
        pl.BlockSpec((tm, D), row),
        pl.BlockSpec((1, D), const),
        pl.BlockSpec((D, _W_IN_COLS), const, pipeline_mode=pl.Buffered(1)),
        pl.BlockSpec((tm, LANES), lambda i: (i % spt, 0)),
        pl.BlockSpec((tm, LANES), lambda i: (i % spt, 0)),
        pl.BlockSpec((1, LANES), const),
        pl.BlockSpec((3, LANES), const),
    ]
    return pl.pallas_call(
        functools.partial(_in_proj_kernel, tiles_per_seq=spt),
        grid=(N // tm,),
        in_specs=in_specs,
        out_specs=out_specs,
        out_shape=out_shape,
        compiler_params=pltpu.CompilerParams(
            dimension_semantics=("parallel",), vmem_limit_bytes=VMEM_LIMIT_BYTES),
        name="in_proj",
    )(x2, ln1, w_perm, cos, sin, qn, kn)


def _compress_kernel(k_ref, v_ref, pek_ref, pev_ref, wk1_ref, wk2_ref, wv1_ref, wv2_ref, ko_ref, vo_ref):
    half = CMP_STRIDE * HEAD_DIM

    def run(seg_ref, pe_ref, w1_ref, w2_ref):
        seg = seg_ref[...]
        rows = seg.shape[0]
        top = (seg + pe_ref[0:1, :]).astype(BF16)
        bot = (seg + pe_ref[1:2, :]).astype(BF16)
        a = _dot(top, w1_ref[0:half, :])
        b = _dot(bot, w1_ref[half:2 * half, :])
        h = a + pltpu.roll(b, rows - 1, 0)
        h = (h * _sigmoid(h)).astype(BF16)
        return _dot(h, w2_ref[...])

    ko_ref[...] = run(k_ref, pek_ref, wk1_ref, wk2_ref).astype(BF16)
    vo_ref[...] = run(v_ref, pev_ref, wv1_ref, wv2_ref).T[0:HEAD_DIM, :].astype(BF16)


def _compress(kc_seg, vc_seg, pek, pev, wk1, wk2, wv1, wv2):
    rows, width = kc_seg.shape
    tr = min(CMP_ROWS, rows)
    const = lambda i: (0, 0)
    seg_spec = pl.BlockSpec((tr, width), lambda i: (i, 0))
    out_spec = pl.BlockSpec((tr, HEAD_DIM), lambda i: (i, 0))
    return pl.pallas_call(
        _compress_kernel,
        grid=(rows // tr,),
        in_specs=[seg_spec, seg_spec,
                  pl.BlockSpec((2, width), const), pl.BlockSpec((2, width), const),
                  pl.BlockSpec((2 * width, CMP_HIDDEN), const), pl.BlockSpec((CMP_HIDDEN, HEAD_DIM), const),
                  pl.BlockSpec((2 * width, CMP_HIDDEN), const), pl.BlockSpec((CMP_HIDDEN, LANES), const)],
        out_specs=[out_spec, pl.BlockSpec((HEAD_DIM, tr), lambda i: (0, i))],
        out_shape=[jax.ShapeDtypeStruct((rows, HEAD_DIM), BF16), jax.ShapeDtypeStruct((HEAD_DIM, rows), BF16)],
        compiler_params=pltpu.CompilerParams(
            dimension_semantics=("parallel",), vmem_limit_bytes=VMEM_LIMIT_BYTES),
        name="compress",
    )(kc_seg, vc_seg, pek, pev, wk1, wk2, wv1, wv2)


def _nsa_kernel(q_ref, kcmp_ref, vcmp_ref, ks_ref, vs_ref, kw_ref, vw_ref, gate_ref,
                o_ref, m_ref, l_ref, acc_ref):
    G, R = NSA_KV_HEADS, NSA_GROUP
    n_cp = kcmp_ref.shape[2]
    S = ks_ref.shape[2]
    n_cmp = (S - CMP_BLOCK) // CMP_STRIDE + 1
    n_blk = S // SEL_BLOCK
    qi = pl.program_id(1)
    q0 = qi * TQ
    t_q = q0 + lax.broadcasted_iota(jnp.int32, (1, TQ), 1)
    q_pad = [q_ref[0, g].reshape(R * TQ, LANES) for g in range(G)]
    qs = [q[:, 0:HEAD_DIM] for q in q_pad]

    def heads(x):
        return [x[:, r * TQ:(r + 1) * TQ] for r in range(R)]

    def masked_exp(s_heads, mask):
        es, invs = [], []
        for s in s_heads:
            s = jnp.where(mask, s, NEG_INF)
            e = jnp.exp2(s - jnp.max(s, axis=0, keepdims=True))
            es.append(e)
            invs.append(1.0 / jnp.sum(e, axis=0, keepdims=True))
        return es, invs

    c_idx = lax.broadcasted_iota(jnp.int32, (n_cp, 1), 0)
    cmask = ((c_idx * CMP_STRIDE + (CMP_BLOCK - 1)) <= t_q) & (c_idx < n_cmp)
    jj = lax.broadcasted_iota(jnp.int32, (n_blk, n_cp), 0)
    cc = lax.broadcasted_iota(jnp.int32, (n_blk, n_cp), 1)
    overlap = ((cc * CMP_STRIDE < (jj + 1) * SEL_BLOCK) & (cc * CMP_STRIDE + CMP_BLOCK > jj * SEL_BLOCK)
               & (cc < n_cmp)).astype(BF16)
    jb = lax.broadcasted_iota(jnp.int32, (n_blk, 1), 0)
    cur = jnp.right_shift(t_q, int(math.log2(SEL_BLOCK)))
    forced = (jb == 0) | (jb == cur) | (jb == cur - 1)
    valid = jb <= cur

    n_wc = (TQ + WINDOW) // KV_CHUNK
    c0 = jnp.maximum(q0 // KV_CHUNK - WINDOW // KV_CHUNK, 0)
    w0 = pl.multiple_of(c0 * KV_CHUNK, KV_CHUNK)
    diff = t_q - (w0 + lax.broadcasted_iota(jnp.int32, (n_wc * KV_CHUNK, 1), 0))
    wmask = (diff >= 0) & (diff < WINDOW)

    s_cmp = [_dot_nt(kcmp_ref[0, g], qs[g]) for g in range(G)]
    s_win = [_dot_nt(kw_ref[0, g, pl.ds(w0, n_wc * KV_CHUNK), :], qs[g]) for g in range(G)]
    any_cmp = t_q >= CMP_BLOCK - 1
    p_cmp, o_cmp, o_win = [], [], []
    for g in range(G):
        es, invs = masked_exp(heads(s_cmp[g]), cmask)
        p_cmp.append([e * jnp.where(any_cmp, inv, 0.0) for e, inv in zip(es, invs)])
    for g in range(G):
        o_cmp.append(_dot(vcmp_ref[:, g * n_cp:(g + 1) * n_cp],
                          jnp.concatenate(p_cmp[g], axis=1).astype(BF16)))
    for g in range(G):
        es, invs = masked_exp(heads(s_win[g]), wmask)
        e_w = jnp.concatenate(es, axis=1).astype(BF16)
        acc = _dot(vw_ref[0, g, c0], e_w[0:KV_CHUNK])
        for j in range(1, n_wc):
            acc = acc + _dot(vw_ref[0, g, c0 + j], e_w[j * KV_CHUNK:(j + 1) * KV_CHUNK])
        o_win.append(acc * jnp.concatenate(invs, axis=1))

    sels = []
    for g in range(G):
        p_c = p_cmp[g]
        p_sum = p_c[0]
        for r in range(1, R):
            p_sum = p_sum + p_c[r]
        p_hi = p_sum.astype(BF16)
        p_lo = (p_sum - p_hi.astype(F32)).astype(BF16)
        p_slc = _dot(overlap, p_hi) + _dot(overlap, p_lo)
        score = jnp.where(forced, FORCE_SCORE, jnp.where(valid, p_slc, -1.0))
        rank = jnp.zeros((n_blk, TQ), jnp.int32)
        for i in range(n_blk):
            row = score[i:i + 1, :]
            before = (row > score) | ((row == score) & (jb > i))
            rank = rank + before.astype(jnp.int32)
        bias = jnp.where(rank < min(SEL_TOPK, n_blk), 0.0, -SEL_BIAS)
        bias = jnp.concatenate([jnp.zeros((HEAD_DIM, TQ), F32), bias,
                                jnp.zeros((LANES - HEAD_DIM - n_blk, TQ), F32)], axis=0).T
        qh = q_pad[g].astype(F32)
        sels.append(jnp.concatenate([qh[r * TQ:(r + 1) * TQ] + bias for r in range(R)], axis=0).astype(BF16))

    m_ref[...] = jnp.full(m_ref.shape, NEG_INF, F32)
    l_ref[...] = jnp.zeros(l_ref.shape, F32)
    acc_ref[...] = jnp.zeros(acc_ref.shape, F32)

    def sel_tile(kt, causal):
        k0 = pl.multiple_of(kt * TK, TK)
        scores = [_dot_nt(ks_ref[0, g, pl.ds(k0, TK), :], sels[g]) for g in range(G)]
        for g in range(G):
            s = scores[g]
            if causal is not None:
                s = jnp.concatenate([jnp.where(causal, sh, NEG_INF) for sh in heads(s)], axis=1)
            m_old = m_ref[g]
            m_new = jnp.maximum(m_old, jnp.max(s, axis=0, keepdims=True))
            alpha = jnp.exp2(m_old - m_new)
            p = jnp.exp2(s - m_new)
            l_ref[g] = alpha * l_ref[g] + jnp.sum(p, axis=0, keepdims=True)
            acc_ref[g] = alpha * acc_ref[g] + _dot(vs_ref[0, g, kt], p.astype(BF16))
            m_ref[g] = m_new

    def interior_tile(kt, carry):
        sel_tile(kt, None)
        return carry

    lax.fori_loop(0, qi, interior_tile, 0)
    sel_tile(qi, (q0 + lax.broadcasted_iota(jnp.int32, (TK, 1), 0)) <= t_q)

    gates = _sigmoid(gate_ref[0].T)
    outs = []
    for g in range(G):
        o_sel = acc_ref[g] * (1.0 / l_ref[g])
        for r, (oc, os_, ow) in enumerate(zip(heads(o_cmp[g]), heads(o_sel), heads(o_win[g]))):
            g0 = g * LANES + r
            outs.append(gates[g0:g0 + 1] * oc + gates[g0 + R:g0 + R + 1] * os_
                        + gates[g0 + 2 * R:g0 + 2 * R + 1] * ow)
    o_ref[0] = jnp.concatenate(outs, axis=0).T.astype(BF16)


def _nsa(q, kcmp, vcmp, ks, vs, kw, vw, gates):
    B, G, R, S, _ = q.shape
    dh = HEAD_DIM
    n_cp = kcmp.shape[2]
    assert TQ == TK, "the key sweep treats exactly one tile per query tile as the diagonal"
    k_spec = lambda a: pl.BlockSpec((1,) + a.shape[1:], lambda b, i: (b, 0, 0, 0))
    vt_spec = lambda a: pl.BlockSpec((1,) + a.shape[1:], lambda b, i: (b, 0, 0, 0, 0))
    return pl.pallas_call(
        _nsa_kernel,
        grid=(B, S // TQ),
        in_specs=[
            pl.BlockSpec((1, G, R, TQ, LANES), lambda b, i: (b, 0, 0, i, 0)),
            pl.BlockSpec((1, G, n_cp, dh), lambda b, i: (b, 0, 0, 0)),
            pl.BlockSpec((dh, G * n_cp), lambda b, i: (0, b)),
            k_spec(ks), vt_spec(vs), k_spec(kw), vt_spec(vw),
            pl.BlockSpec((1, TQ, G * LANES), lambda b, i: (b, i, 0)),
        ],
        out_specs=pl.BlockSpec((1, TQ, G * R * dh), lambda b, i: (b, i, 0)),
        out_shape=jax.ShapeDtypeStruct((B, S, G * R * dh), BF16),
        scratch_shapes=[pltpu.VMEM((G, 1, R * TQ), F32), pltpu.VMEM((G, 1, R * TQ), F32),
                        pltpu.VMEM((G, dh, R * TQ), F32)],
        compiler_params=pltpu.CompilerParams(
            dimension_semantics=("parallel", "arbitrary"), vmem_limit_bytes=VMEM_LIMIT_BYTES),
        name="nsa_attention",
    )(q, kcmp, vcmp, ks, vs, kw, vw, gates)


def _retention_kernel(q_ref, k_ref, v_ref, g_ref, w_ref, o_ref, state_ref):
    C = RET_CHUNK

    @pl.when(pl.program_id(1) == 0)
    def _():
        state_ref[...] = jnp.zeros(state_ref.shape, F32)

    i_col = lax.broadcasted_iota(jnp.int32, (C, 1), 0)
    d_int = lax.broadcasted_iota(jnp.int32, (C, C), 0) - lax.broadcasted_iota(jnp.int32, (C, C), 1)
    for h in range(RET_HEADS):
        log_gamma = math.log(1.0 - 2.0 ** (-5.0 - h))
        dmat = jnp.where(d_int >= 0, jnp.exp(log_gamma * jnp.maximum(d_int, 0).astype(F32)), 0.0)
        xi = jnp.exp(log_gamma * (i_col + 1).astype(F32))
        zeta = jnp.exp(log_gamma * (C - 1 - i_col).astype(F32))
        gamma_c = math.exp(log_gamma * C)

        qs = slice(h * RET_QK_DIM, (h + 1) * RET_QK_DIM)
        vsl = slice(h * RET_V_DIM, (h + 1) * RET_V_DIM)
        q = q_ref[0, :, qs]
        k = k_ref[0, :, qs]
        v = v_ref[0, :, vsl]
        state = state_ref[h]
        inner = _dot_nt(q, k) * dmat
        y = _dot(inner.astype(BF16), v) + _dot(q, state.astype(BF16)) * xi
        kz = (k.astype(F32) * zeta).T.astype(BF16)
        state_ref[h] = gamma_c * state + _dot(kz, v)

        mu = jnp.mean(y, axis=-1, keepdims=True)
        yc = y - mu
        var = jnp.mean(yc * yc, axis=-1, keepdims=True)
        yn = yc * lax.rsqrt(var + EPS) * w_ref[:, vsl]
        gate = g_ref[0, :, vsl].astype(F32)
        o_ref[0, :, vsl] = (gate * _sigmoid(gate) * yn).astype(BF16)


def _retention(rq, rk, rv, rg, w):
    B, S, _ = rq.shape
    C = RET_CHUNK
    qk_spec = pl.BlockSpec((1, C, RET_QK_WIDTH), lambda b, c: (b, c, 0))
    v_spec = pl.BlockSpec((1, C, RET_V_WIDTH), lambda b, c: (b, c, 0))
    return pl.pallas_call(
        _retention_kernel,
        grid=(B, S // C),
        in_specs=[qk_spec, qk_spec, v_spec, v_spec, pl.BlockSpec((1, RET_V_WIDTH), lambda b, c: (0, 0))],
        out_specs=v_spec,
        out_shape=jax.ShapeDtypeStruct((B, S, RET_V_WIDTH), BF16),
        scratch_shapes=[pltpu.VMEM((RET_HEADS, RET_QK_DIM, RET_V_DIM), F32)],
        compiler_params=pltpu.CompilerParams(
            dimension_semantics=("parallel", "arbitrary"), vmem_limit_bytes=VMEM_LIMIT_BYTES),
        name="retention",
    )(rq, rk, rv, rg, w)


def _out_ffn_kernel(x_ref, a_ref, r_ref, wo_ref, ln_ref, wu_ref, wd_ref, o_ref):
    na = a_ref.shape[1]
    mix = _dot(a_ref[...], wo_ref[0:na, :]) + _dot(r_ref[...], wo_ref[na:, :])
    h = x_ref[...] + mix
    ms = jnp.mean(h * h, axis=-1, keepdims=True)
    hn = (h * lax.rsqrt(ms + EPS) * ln_ref[...]).astype(BF16)
    d_ff = wu_ref.shape[1]
    acc = None
    for f in range(d_ff // FF_CHUNK):
        cols = slice(f * FF_CHUNK, (f + 1) * FF_CHUNK)
        u = jnp.maximum(_dot(hn, wu_ref[:, cols]), 0.0)
        d = _dot((u * u).astype(BF16), wd_ref[cols, :])
        acc = d if acc is None else acc + d
    o_ref[...] = h + acc


def _out_ffn(x2, o_nsa, o_ret, w_out, ln2, w_up, w_down):
    N, D = x2.shape
    tm = TM_FFN
    row = lambda i: (i, 0)
    const = lambda i: (0, 0)
    resident = functools.partial(pl.BlockSpec, index_map=const, pipeline_mode=pl.Buffered(1))
    return pl.pallas_call(
        _out_ffn_kernel,
        grid=(N // tm,),
        in_specs=[
            pl.BlockSpec((tm, D), row),
            pl.BlockSpec((tm, o_nsa.shape[1]), row),
            pl.BlockSpec((tm, o_ret.shape[1]), row),
            resident(w_out.shape),
            pl.BlockSpec((1, D), const),
            resident(w_up.shape),
            resident(w_down.shape),
        ],
        out_specs=pl.BlockSpec((tm, D), row),
        out_shape=jax.ShapeDtypeStruct((N, D), F32),
        compiler_params=pltpu.CompilerParams(
            dimension_semantics=("parallel",), vmem_limit_bytes=VMEM_LIMIT_BYTES),
        name="out_ffn",
    )(x2, o_nsa, o_ret, w_out, ln2, w_up, w_down)


def _rope_tables(S):
    half = HEAD_DIM // 2
    inv = ROPE_THETA ** (-jnp.arange(half, dtype=F32) / half)
    ang = jnp.arange(S).astype(F32)[:, None] * inv[None, :]
    cos, sin = jnp.cos(ang), jnp.sin(ang)
    reps = LANES // HEAD_DIM
    cos_t = jnp.tile(jnp.concatenate([cos, cos], axis=-1), (1, reps))
    sin_t = jnp.tile(jnp.concatenate([-sin, sin], axis=-1), (1, reps))
    return cos_t, sin_t


def _layer(h, ln1_w, w_in, q_norm_w, k_norm_w, cmp_pe_k, cmp_pe_v, cmp_wk1, cmp_wk2,
           cmp_wv1, cmp_wv2, ret_norm_w, w_out, ln2_w, w_up, w_down):
    B, S, D = h.shape
    N = B * S
    G = NSA_KV_HEADS
    x2 = h.reshape(N, D)

    order = _w_in_column_order()
    w_perm = jnp.where(order[None, :] >= 0, w_in[:, np.maximum(order, 0)], 0.0).astype(BF16)
    cos_t, sin_t = _rope_tables(S)
    reps = LANES // HEAD_DIM
    qn = jnp.tile(q_norm_w, reps)[None, :]
    kn = jnp.tile(k_norm_w, (1, reps))

    (q, kc, vc, ks, vs, kw, vw, gates, rq, rk, rv, rg) = _in_proj(
        x2, ln1_w[None, :], w_perm, cos_t, sin_t, qn, kn, B, S)

    seg_w = CMP_STRIDE * HEAD_DIM
    n_seg = S // CMP_STRIDE
    kcmp, vcmp = _compress(
        kc.reshape(B * G * n_seg, seg_w), vc.reshape(B * G * n_seg, seg_w),
        cmp_pe_k.reshape(2, seg_w), cmp_pe_v.reshape(2, seg_w),
        cmp_wk1.astype(BF16), cmp_wk2.astype(BF16), cmp_wv1.astype(BF16),
        jnp.pad(cmp_wv2, ((0, 0), (0, LANES - HEAD_DIM))).astype(BF16))
    kcmp = kcmp.reshape(B, G, n_seg, HEAD_DIM)

    o_nsa = _nsa(q, kcmp, vcmp, ks, vs, kw, vw, gates.reshape(B, S, G * LANES))
    o_ret = _retention(rq.reshape(B, S, -1), rk.reshape(B, S, -1), rv.reshape(B, S, -1),
                       rg.reshape(B, S, -1), ret_norm_w.reshape(1, RET_V_WIDTH))

    out = _out_ffn(x2, o_nsa.reshape(N, -1), o_ret.reshape(N, -1), w_out.astype(BF16),
                   ln2_w[None, :], w_up.astype(BF16), w_down.astype(BF16))
    return out.reshape(B, S, D)


def kernel(x, ln1_w, w_in, q_norm_w, k_norm_w, cmp_pe_k, cmp_pe_v, cmp_wk1, cmp_wk2, cmp_wv1, cmp_wv2,
           ret_norm_w, w_out, ln2_w, w_up, w_down):
    h = x
    for l in range(ln1_w.shape[0]):
        h = _layer(h, ln1_w[l], w_in[l], q_norm_w[l], k_norm_w[l], cmp_pe_k[l], cmp_pe_v[l],
                   cmp_wk1[l], cmp_wk2[l], cmp_wv1[l], cmp_wv2[l], ret_norm_w[l], w_out[l],
                   ln2_w[l], w_up[l], w_down[l])
    return h
```

```python
import functools
import math

import jax
import jax.numpy as jnp
import numpy as np
from jax import lax
from jax.experimental import pallas as pl
from jax.experimental.pallas import tpu as pltpu

F32 = jnp.float32
BF16 = jnp.bfloat16

NSA_HEADS = 8
NSA_KV_HEADS = 2
NSA_GROUP = NSA_HEADS // NSA_KV_HEADS
HEAD_DIM = 64
CMP_BLOCK = 32
CMP_STRIDE = 16
CMP_HIDDEN = 256
SEL_BLOCK = 64
SEL_TOPK = 8
WINDOW = 256
FORCE_SCORE = 1.0e4
RET_HEADS = 4
RET_QK_DIM = 64
RET_V_DIM = 128
RET_CHUNK = 128
ROPE_THETA = 10000.0
EPS = 1e-6
NEG_INF = -1.0e30
SEL_BIAS = 2.0 ** 100

NSA_Q_DIM = NSA_HEADS * HEAD_DIM
NSA_KV_DIM = NSA_KV_HEADS * HEAD_DIM
NSA_GATE_DIM = 3 * NSA_HEADS
RET_QK_WIDTH = RET_HEADS * RET_QK_DIM
RET_V_WIDTH = RET_HEADS * RET_V_DIM

LANES = 128
VMEM_LIMIT_BYTES = 56 * 1024 * 1024

TM_PROJ = 512
TQ = 256
TK = 256
KV_CHUNK = 128
TM_FFN = 512
FF_CHUNK = 512
CMP_ROWS = 512
RET_BATCH = 4


def _dot(a, b):
    return jnp.dot(a, b, preferred_element_type=F32)


def _dot_nt(a, b):
    return lax.dot_general(a, b, (((1,), (1,)), ((), ())), preferred_element_type=F32)


def _sigmoid(x):
    return 1.0 / (1.0 + jnp.exp(-x))


_Q0 = 0
_K0 = _Q0 + NSA_Q_DIM
_V0 = _K0 + 3 * NSA_KV_DIM
_G0 = _V0 + 3 * NSA_KV_DIM
_RQ0 = _G0 + LANES
_RK0 = _RQ0 + RET_QK_WIDTH
_RV0 = _RK0 + RET_QK_WIDTH
_RG0 = _RV0 + RET_V_WIDTH
_W_IN_COLS = _RG0 + RET_V_WIDTH


def _permute_w_in(w_in):
    sizes = [NSA_Q_DIM] + [NSA_KV_DIM] * 6 + [NSA_GATE_DIM, RET_QK_WIDTH, RET_QK_WIDTH, RET_V_WIDTH, RET_V_WIDTH]
    off = [int(o) for o in np.concatenate([[0], np.cumsum(sizes)])]
    (q, kc, vc, ks, vs, kw, vw, gate, rq, rk, rv, rg) = [w_in[:, off[i]:off[i + 1]] for i in range(12)]
    d = w_in.shape[0]
    gate = gate.reshape(d, 3, NSA_KV_HEADS, NSA_GROUP).transpose(0, 2, 1, 3).reshape(d, NSA_KV_HEADS, 3 * NSA_GROUP)
    gate = jnp.pad(gate, ((0, 0), (0, 0), (0, HEAD_DIM - 3 * NSA_GROUP))).reshape(d, LANES)
    w = jnp.concatenate([q, kc, ks, kw, vc, vs, vw, gate, rq, rk, rv, rg], axis=1)
    assert w.shape[1] == _W_IN_COLS
    return w


def _in_proj_kernel(x_ref, ln_ref, w_ref, cos_ref, sin_ref, qn_ref, kn_ref,
                    q_ref, kc_ref, vc_ref, ks_ref, vs_ref, kw_ref, vw_ref, gate_ref,
                    rq_ref, rk_ref, rv_ref, rg_ref, *, tiles_per_seq):
    x = x_ref[...]
    ms = jnp.mean(x * x, axis=-1, keepdims=True)
    xn = (x * lax.rsqrt(ms + EPS) * ln_ref[...]).astype(BF16)
    cos = cos_ref[...]
    sin = sin_ref[...]
    lane = lax.broadcasted_iota(jnp.int32, (1, LANES), 1)
    low_half = (lane & (HEAD_DIM // 2)) == 0
    first_head = lane < HEAD_DIM

    def proj(a, b):
        return _dot(xn, w_ref[:, a:b])

    def rope(t):
        swapped = jnp.where(low_half, pltpu.roll(t, LANES - HEAD_DIM // 2, 1), pltpu.roll(t, HEAD_DIM // 2, 1))
        return t * cos + swapped * sin

    def head_norm(t, w):
        t2 = t * t
        s0 = jnp.sum(jnp.where(first_head, t2, 0.0), axis=-1, keepdims=True)
        s1 = jnp.sum(jnp.where(first_head, 0.0, t2), axis=-1, keepdims=True)
        msq = jnp.where(first_head, s0, s1) * (1.0 / HEAD_DIM)
        return t * lax.rsqrt(msq + EPS) * w

    def tiles(sec):
        return [sec[:, c * LANES:(c + 1) * LANES] for c in range(sec.shape[1] // LANES)]

    def split_heads(t, fill):
        return [jnp.where(first_head, t, fill), jnp.where(first_head, pltpu.roll(t, HEAD_DIM, 1), fill)]

    tm = x.shape[0]

    qn = qn_ref[...]
    for c, t in enumerate(tiles(proj(_Q0, _K0))):
        t = rope(head_norm(t, qn)) * (HEAD_DIM ** -0.5 * math.log2(math.e))
        for j, tj in enumerate(split_heads(t, 0.0)):
            h = 2 * c + j
            q_ref[0, h // NSA_GROUP, h % NSA_GROUP] = tj.astype(BF16)

    ksec = tiles(proj(_K0, _V0))
    kc = rope(head_norm(ksec[0], kn_ref[0:1, :]))
    ks = rope(head_norm(ksec[1], kn_ref[1:2, :]))
    kw = rope(head_norm(ksec[2], kn_ref[2:3, :])).astype(BF16)
    s0 = (pl.program_id(0) % tiles_per_seq) * tm
    blk = jnp.right_shift(s0 + lax.broadcasted_iota(jnp.int32, (tm, 1), 0), int(math.log2(SEL_BLOCK)))
    onehot = (lane - HEAD_DIM == blk).astype(F32)
    ks_aug = split_heads(ks, onehot)
    for g in range(NSA_KV_HEADS):
        sl = slice(g * HEAD_DIM, (g + 1) * HEAD_DIM)
        kc_ref[0, g] = kc[:, sl]
        ks_ref[0, g] = ks_aug[g].astype(BF16)
        kw_ref[0, g] = kw[:, sl]

    def store_transposed(v, out_ref):
        chunk = out_ref.shape[-1]
        for c in range(v.shape[0] // chunk):
            vt = v[c * chunk:(c + 1) * chunk, :].T.astype(BF16)
            for g in range(NSA_KV_HEADS):
                out_ref[0, g, c] = vt[g * HEAD_DIM:(g + 1) * HEAD_DIM, :]

    vsec = tiles(proj(_V0, _RQ0))
    for g in range(NSA_KV_HEADS):
        vc_ref[0, g] = vsec[0][:, g * HEAD_DIM:(g + 1) * HEAD_DIM]
    store_transposed(vsec[1], vs_ref)
    store_transposed(vsec[2], vw_ref)
    gate_ref[...] = vsec[3]

    for c, t in enumerate(tiles(proj(_RQ0, _RK0))):
        rq_ref[:, c * LANES:(c + 1) * LANES] = rope(t).astype(BF16)
    for c, t in enumerate(tiles(proj(_RK0, _RV0))):
        t = rope(t) * (RET_QK_DIM ** -0.5)
        for j in range(tm // RET_CHUNK):
            rk_ref[0, j, c * LANES:(c + 1) * LANES, :] = t[j * RET_CHUNK:(j + 1) * RET_CHUNK, :].T.astype(BF16)
    rv_ref[...] = proj(_RV0, _RG0).astype(BF16)
    rg_ref[...] = proj(_RG0, _W_IN_COLS).astype(BF16)


def _in_proj(x2, ln1, w_perm, cos, sin, qn, kn, B, S):
    N, D = x2.shape
    tm = TM_PROJ
    spt = S // tm
    G, R, dh = NSA_KV_HEADS, NSA_GROUP, HEAD_DIM

    def row(i):
        return (i, 0)

    def hm(i):
        return (i // spt, 0, i % spt, 0)

    hm_spec = pl.BlockSpec((1, G, tm, dh), hm)
    out_shape = [
        jax.ShapeDtypeStruct((B, G, R, S, LANES), BF16),
        jax.ShapeDtypeStruct((B, G, S, dh), F32),
        jax.ShapeDtypeStruct((B, G, S, dh), F32),
        jax.ShapeDtypeStruct((B, G, S, LANES), BF16),
        jax.ShapeDtypeStruct((B, G, S // TK, dh, TK), BF16),
        jax.ShapeDtypeStruct((B, G, S, dh), BF16),
        jax.ShapeDtypeStruct((B, G, S // KV_CHUNK, dh, KV_CHUNK), BF16),
        jax.ShapeDtypeStruct((N, LANES), F32),
        jax.ShapeDtypeStruct((N, RET_QK_WIDTH), BF16),
        jax.ShapeDtypeStruct((B, S // RET_CHUNK, RET_QK_WIDTH, RET_CHUNK), BF16),
        jax.ShapeDtypeStruct((N, RET_V_WIDTH), BF16),
        jax.ShapeDtypeStruct((N, RET_V_WIDTH), BF16),
    ]
    out_specs = [
        pl.BlockSpec((1, G, R, tm, LANES), lambda i: (i // spt, 0, 0, i % spt, 0)),
        hm_spec, hm_spec, pl.BlockSpec((1, G, tm, LANES), hm),
        pl.BlockSpec((1, G, tm // TK, dh, TK), lambda i: (i // spt, 0, i % spt, 0, 0)),
        hm_spec,
        pl.BlockSpec((1, G, tm // KV_CHUNK, dh, KV_CHUNK), lambda i: (i // spt, 0, i % spt, 0, 0)),
        pl.BlockSpec((tm, LANES), row),
        pl.BlockSpec((tm, RET_QK_WIDTH), row),
        pl.BlockSpec((1, tm // RET_CHUNK, RET_QK_WIDTH, RET_CHUNK), lambda i: (i // spt, i % spt, 0, 0)),
        pl.BlockSpec((tm, RET_V_WIDTH), row),
        pl.BlockSpec((tm, RET_V_WIDTH), row),
    ]
    const = lambda i: (0, 0)
    in_specs = [
        pl.BlockSpec((tm, D), row),
        pl.BlockSpec((1, D), const),
        pl.BlockSpec((D, _W_IN_COLS), const, pipeline_mode=pl.Buffered(1)),
        pl.BlockSpec((tm, LANES), lambda i: (i % spt, 0)),
        pl.BlockSpec((tm, LANES), lambda i: (i % spt, 0)),
        pl.BlockSpec((1, LANES), const),
        pl.BlockSpec((3, LANES), const),
    ]
    return pl.pallas_call(
        functools.partial(_in_proj_kernel, tiles_per_seq=spt),
        grid=(N // tm,),
        in_specs=in_specs,
        out_specs=out_specs,
        out_shape=out_shape,
        compiler_params=pltpu.CompilerParams(
            dimension_semantics=("parallel",), vmem_limit_bytes=VMEM_LIMIT_BYTES),
        name="in_proj",
    )(x2, ln1, w_perm, cos, sin, qn, kn)


def _compress_kernel(k_ref, v_ref, pek_ref, pev_ref, wk1_ref, wk2_ref, wv1_ref, wv2_ref, ko_ref, vo_ref):
    half = CMP_STRIDE * HEAD_DIM

    def run(seg_ref, pe_ref, w1_ref, w2_ref):
        seg = seg_ref[...]
        rows = seg.shape[0]
        top = (seg + pe_ref[0:1, :]).astype(BF16)
        bot = (seg + pe_ref[1:2, :]).astype(BF16)
        a = _dot(top, w1_ref[0:half, :])
        b = _dot(bot, w1_ref[half:2 * half, :])
        h = a + pltpu.roll(b, rows - 1, 0)
        h = (h * _sigmoid(h)).astype(BF16)
        return _dot(h, w2_ref[...])

    ko_ref[...] = run(k_ref, pek_ref, wk1_ref, wk2_ref).astype(BF16)
    vo_ref[...] = run(v_ref, pev_ref, wv1_ref, wv2_ref).T[0:HEAD_DIM, :].astype(BF16)


def _compress(kc_seg, vc_seg, pek, pev, wk1, wk2, wv1, wv2):
    rows, width = kc_seg.shape
    tr = min(CMP_ROWS, rows)
    const = lambda i: (0, 0)
    seg_spec = pl.BlockSpec((tr, width), lambda i: (i, 0))
    out_spec = pl.BlockSpec((tr, HEAD_DIM), lambda i: (i, 0))
    return pl.pallas_call(
        _compress_kernel,
        grid=(rows // tr,),
        in_specs=[seg_spec, seg_spec,
                  pl.BlockSpec((2, width), const), pl.BlockSpec((2, width), const),
                  pl.BlockSpec((2 * width, CMP_HIDDEN), const), pl.BlockSpec((CMP_HIDDEN, HEAD_DIM), const),
                  pl.BlockSpec((2 * width, CMP_HIDDEN), const), pl.BlockSpec((CMP_HIDDEN, LANES), const)],
        out_specs=[out_spec, pl.BlockSpec((HEAD_DIM, tr), lambda i: (0, i))],
        out_shape=[jax.ShapeDtypeStruct((rows, HEAD_DIM), BF16), jax.ShapeDtypeStruct((HEAD_DIM, rows), BF16)],
        compiler_params=pltpu.CompilerParams(
            dimension_semantics=("parallel",), vmem_limit_bytes=VMEM_LIMIT_BYTES),
        name="compress",
    )(kc_seg, vc_seg, pek, pev, wk1, wk2, wv1, wv2)


def _nsa_kernel(q_ref, kcmp_ref, vcmp_ref, ks_ref, vs_ref, kw_ref, vw_ref, gate_ref,
                o_ref, m_ref, l_ref, acc_ref):
    G, R = NSA_KV_HEADS, NSA_GROUP
    n_cp = kcmp_ref.shape[2]
    S = ks_ref.shape[2]
    n_cmp = (S - CMP_BLOCK) // CMP_STRIDE + 1
    n_blk = S // SEL_BLOCK
    qi = pl.program_id(1)
    q0 = qi * TQ
    t_q = q0 + lax.broadcasted_iota(jnp.int32, (1, TQ), 1)
    q_pad = [q_ref[0, g].reshape(R * TQ, LANES) for g in range(G)]
    qs = [q[:, 0:HEAD_DIM] for q in q_pad]

    def heads(x):
        return [x[:, r * TQ:(r + 1) * TQ] for r in range(R)]

    def masked_exp(s_heads, mask):
        es, invs = [], []
        for s in s_heads:
            s = jnp.where(mask, s, NEG_INF)
            e = jnp.exp2(s - jnp.max(s, axis=0, keepdims=True))
            es.append(e)
            invs.append(1.0 / jnp.sum(e, axis=0, keepdims=True))
        return es, invs

    c_idx = lax.broadcasted_iota(jnp.int32, (n_cp, 1), 0)
    cmask = ((c_idx * CMP_STRIDE + (CMP_BLOCK - 1)) <= t_q) & (c_idx < n_cmp)
    jj = lax.broadcasted_iota(jnp.int32, (n_blk, n_cp), 0)
    cc = lax.broadcasted_iota(jnp.int32, (n_blk, n_cp), 1)
    overlap = ((cc * CMP_STRIDE < (jj + 1) * SEL_BLOCK) & (cc * CMP_STRIDE + CMP_BLOCK > jj * SEL_BLOCK)
               & (cc < n_cmp)).astype(BF16)
    jb = lax.broadcasted_iota(jnp.int32, (n_blk, 1), 0)
    cur = jnp.right_shift(t_q, int(math.log2(SEL_BLOCK)))
    forced = (jb == 0) | (jb == cur) | (jb == cur - 1)
    valid = jb <= cur

    n_wc = (TQ + WINDOW) // KV_CHUNK
    c0 = jnp.maximum(q0 // KV_CHUNK - WINDOW // KV_CHUNK, 0)
    w0 = pl.multiple_of(c0 * KV_CHUNK, KV_CHUNK)
    diff = t_q - (w0 + lax.broadcasted_iota(jnp.int32, (n_wc * KV_CHUNK, 1), 0))
    wmask = (diff >= 0) & (diff < WINDOW)

    s_cmp = [_dot_nt(kcmp_ref[0, g], qs[g]) for g in range(G)]
    s_win = [_dot_nt(kw_ref[0, g, pl.ds(w0, n_wc * KV_CHUNK), :], qs[g]) for g in range(G)]
    any_cmp = t_q >= CMP_BLOCK - 1
    p_cmp, o_cmp, o_win = [], [], []
    for g in range(G):
        es, invs = masked_exp(heads(s_cmp[g]), cmask)
        p_cmp.append([e * jnp.where(any_cmp, inv, 0.0) for e, inv in zip(es, invs)])
    for g in range(G):
        o_cmp.append(_dot(vcmp_ref[:, g * n_cp:(g + 1) * n_cp],
                          jnp.concatenate(p_cmp[g], axis=1).astype(BF16)))
    for g in range(G):
        es, invs = masked_exp(heads(s_win[g]), wmask)
        e_w = jnp.concatenate(es, axis=1).astype(BF16)
        acc = _dot(vw_ref[0, g, c0], e_w[0:KV_CHUNK])
        for j in range(1, n_wc):
            acc = acc + _dot(vw_ref[0, g, c0 + j], e_w[j * KV_CHUNK:(j + 1) * KV_CHUNK])
        o_win.append(acc * jnp.concatenate(invs, axis=1))

    sels = []
    for g in range(G):
        p_c = p_cmp[g]
        p_sum = p_c[0]
        for r in range(1, R):
            p_sum = p_sum + p_c[r]
        p_hi = p_sum.astype(BF16)
        p_lo = (p_sum - p_hi.astype(F32)).astype(BF16)
        p_slc = _dot(overlap, p_hi) + _dot(overlap, p_lo)
        score = jnp.where(forced, FORCE_SCORE, jnp.where(valid, p_slc, -1.0))
        rank = jnp.zeros((n_blk, TQ), jnp.int32)
        for i in range(n_blk):
            row = score[i:i + 1, :]
            before = (row > score) | ((row == score) & (jb > i))
            rank = rank + before.astype(jnp.int32)
        bias = jnp.where(rank < min(SEL_TOPK, n_blk), 0.0, -SEL_BIAS)
        bias = jnp.concatenate([jnp.zeros((HEAD_DIM, TQ), F32), bias,
                                jnp.zeros((LANES - HEAD_DIM - n_blk, TQ), F32)], axis=0).T
        qh = q_pad[g].astype(F32)
        sels.append(jnp.concatenate([qh[r * TQ:(r + 1) * TQ] + bias for r in range(R)], axis=0).astype(BF16))

    m_ref[...] = jnp.full(m_ref.shape, NEG_INF, F32)
    l_ref[...] = jnp.zeros(l_ref.shape, F32)
    acc_ref[...] = jnp.zeros(acc_ref.shape, F32)

    def sel_tile(kt, causal):
        k0 = pl.multiple_of(kt * TK, TK)
        scores = [_dot_nt(ks_ref[0, g, pl.ds(k0, TK), :], sels[g]) for g in range(G)]
        for g in range(G):
            s = scores[g]
            if causal is not None:
                s = jnp.concatenate([jnp.where(causal, sh, NEG_INF) for sh in heads(s)], axis=1)
            m_old = m_ref[g]
            m_new = jnp.maximum(m_old, jnp.max(s, axis=0, keepdims=True))
            alpha = jnp.exp2(m_old - m_new)
            p = jnp.exp2(s - m_new)
            l_ref[g] = alpha * l_ref[g] + jnp.sum(p, axis=0, keepdims=True)
            acc_ref[g] = alpha * acc_ref[g] + _dot(vs_ref[0, g, kt], p.astype(BF16))
            m_ref[g] = m_new

    def interior_tile(kt, carry):
        sel_tile(kt, None)
        return carry

    lax.fori_loop(0, qi, interior_tile, 0)
    sel_tile(qi, (q0 + lax.broadcasted_iota(jnp.int32, (TK, 1), 0)) <= t_q)

    gates = _sigmoid(gate_ref[0].T)
    outs = []
    for g in range(G):
        o_sel = acc_ref[g] * (1.0 / l_ref[g])
        for r, (oc, os_, ow) in enumerate(zip(heads(o_cmp[g]), heads(o_sel), heads(o_win[g]))):
            g0 = g * HEAD_DIM + r
            outs.append(gates[g0:g0 + 1] * oc + gates[g0 + R:g0 + R + 1] * os_
                        + gates[g0 + 2 * R:g0 + 2 * R + 1] * ow)
    o_ref[0] = jnp.concatenate(outs, axis=0).T.astype(BF16)


def _nsa(q, kcmp, vcmp, ks, vs, kw, vw, gates):
    B, G, R, S, _ = q.shape
    dh = HEAD_DIM
    n_cp = kcmp.shape[2]
    assert TQ == TK, "the key sweep treats exactly one tile per query tile as the diagonal"
    k_spec = lambda a: pl.BlockSpec((1,) + a.shape[1:], lambda b, i: (b, 0, 0, 0))
    vt_spec = lambda a: pl.BlockSpec((1,) + a.shape[1:], lambda b, i: (b, 0, 0, 0, 0))
    return pl.pallas_call(
        _nsa_kernel,
        grid=(B, S // TQ),
        in_specs=[
            pl.BlockSpec((1, G, R, TQ, LANES), lambda b, i: (b, 0, 0, i, 0)),
            pl.BlockSpec((1, G, n_cp, dh), lambda b, i: (b, 0, 0, 0)),
            pl.BlockSpec((dh, G * n_cp), lambda b, i: (0, b)),
            k_spec(ks), vt_spec(vs), k_spec(kw), vt_spec(vw),
            pl.BlockSpec((1, TQ, LANES), lambda b, i: (b, i, 0)),
        ],
        out_specs=pl.BlockSpec((1, TQ, G * R * dh), lambda b, i: (b, i, 0)),
        out_shape=jax.ShapeDtypeStruct((B, S, G * R * dh), BF16),
        scratch_shapes=[pltpu.VMEM((G, 1, R * TQ), F32), pltpu.VMEM((G, 1, R * TQ), F32),
                        pltpu.VMEM((G, dh, R * TQ), F32)],
        compiler_params=pltpu.CompilerParams(
            dimension_semantics=("parallel", "arbitrary"), vmem_limit_bytes=VMEM_LIMIT_BYTES),
        name="nsa_attention",
    )(q, kcmp, vcmp, ks, vs, kw, vw, gates)


def _retention_kernel(q_ref, kt_ref, v_ref, g_ref, w_ref, o_ref, state_ref):
    C = RET_CHUNK
    NB = q_ref.shape[0]

    @pl.when(pl.program_id(1) == 0)
    def _():
        state_ref[...] = jnp.zeros(state_ref.shape, F32)

    i_col = lax.broadcasted_iota(jnp.int32, (C, 1), 0)
    i_row = lax.broadcasted_iota(jnp.int32, (1, C), 1)
    d_int = i_col - i_row
    log_gamma = [math.log(1.0 - 2.0 ** (-5.0 - h)) for h in range(RET_HEADS)]
    dmat = [jnp.where(d_int >= 0, jnp.exp(lg * jnp.maximum(d_int, 0).astype(F32)), 0.0) for lg in log_gamma]
    xi = [jnp.exp(lg * (i_col + 1).astype(F32)) for lg in log_gamma]
    zeta = [jnp.exp(lg * (C - 1 - i_row).astype(F32)) for lg in log_gamma]
    gamma_c = [math.exp(lg * C) for lg in log_gamma]

    units = [(n, h) for n in range(NB) for h in range(RET_HEADS)]
    q, kt, v, st = {}, {}, {}, {}
    for n, h in units:
        q[n, h] = q_ref[n, :, h * RET_QK_DIM:(h + 1) * RET_QK_DIM]
        kt[n, h] = kt_ref[n, 0, h * RET_QK_DIM:(h + 1) * RET_QK_DIM, :]
        v[n, h] = v_ref[n, :, h * RET_V_DIM:(h + 1) * RET_V_DIM]
        st[n, h] = state_ref[n, h]
    inner = {u: _dot(q[u], kt[u]) for u in units}
    cross = {u: _dot(q[u], st[u].astype(BF16)) for u in units}
    kv = {u: _dot((kt[u].astype(F32) * zeta[u[1]]).astype(BF16), v[u]) for u in units}
    for n, h in units:
        u = (n, h)
        vsl = slice(h * RET_V_DIM, (h + 1) * RET_V_DIM)
        y = _dot((inner[u] * dmat[h]).astype(BF16), v[u]) + cross[u] * xi[h]
        state_ref[n, h] = gamma_c[h] * st[u] + kv[u]

        mu = jnp.mean(y, axis=-1, keepdims=True)
        yc = y - mu
        var = jnp.mean(yc * yc, axis=-1, keepdims=True)
        yn = yc * lax.rsqrt(var + EPS) * w_ref[:, vsl]
        gate = g_ref[n, :, vsl].astype(F32)
        o_ref[n, :, vsl] = (gate * _sigmoid(gate) * yn).astype(BF16)


def _retention(rq, rkt, rv, rg, w):
    B, S, _ = rq.shape
    C = RET_CHUNK
    nb = math.gcd(RET_BATCH, B)
    q_spec = pl.BlockSpec((nb, C, RET_QK_WIDTH), lambda b, c: (b, c, 0))
    kt_spec = pl.BlockSpec((nb, 1, RET_QK_WIDTH, C), lambda b, c: (b, c, 0, 0))
    v_spec = pl.BlockSpec((nb, C, RET_V_WIDTH), lambda b, c: (b, c, 0))
    return pl.pallas_call(
        _retention_kernel,
        grid=(B // nb, S // C),
        in_specs=[q_spec, kt_spec, v_spec, v_spec, pl.BlockSpec((1, RET_V_WIDTH), lambda b, c: (0, 0))],
        out_specs=v_spec,
        out_shape=jax.ShapeDtypeStruct((B, S, RET_V_WIDTH), BF16),
        scratch_shapes=[pltpu.VMEM((nb, RET_HEADS, RET_QK_DIM, RET_V_DIM), F32)],
        compiler_params=pltpu.CompilerParams(
            dimension_semantics=("parallel", "arbitrary"), vmem_limit_bytes=VMEM_LIMIT_BYTES),
        name="retention",
    )(rq, rkt, rv, rg, w)


def _out_ffn_kernel(x_ref, a_ref, r_ref, wo_ref, ln_ref, wu_ref, wd_ref, o_ref):
    na = a_ref.shape[1]
    mix = _dot(a_ref[...], wo_ref[0:na, :]) + _dot(r_ref[...], wo_ref[na:, :])
    h = x_ref[...] + mix
    ms = jnp.mean(h * h, axis=-1, keepdims=True)
    hn = (h * lax.rsqrt(ms + EPS) * ln_ref[...]).astype(BF16)
    d_ff = wu_ref.shape[1]
    acc = None
    for f in range(d_ff // FF_CHUNK):
        cols = slice(f * FF_CHUNK, (f + 1) * FF_CHUNK)
        u = jnp.maximum(_dot(hn, wu_ref[:, cols]), 0.0)
        d = _dot((u * u).astype(BF16), wd_ref[cols, :])
        acc = d if acc is None else acc + d
    o_ref[...] = h + acc


def _out_ffn(x2, o_nsa, o_ret, w_out, ln2, w_up, w_down):
    N, D = x2.shape
    tm = TM_FFN
    row = lambda i: (i, 0)
    const = lambda i: (0, 0)
    resident = functools.partial(pl.BlockSpec, index_map=const, pipeline_mode=pl.Buffered(1))
    return pl.pallas_call(
        _out_ffn_kernel,
        grid=(N // tm,),
        in_specs=[
            pl.BlockSpec((tm, D), row),
            pl.BlockSpec((tm, o_nsa.shape[1]), row),
            pl.BlockSpec((tm, o_ret.shape[1]), row),
            resident(w_out.shape),
            pl.BlockSpec((1, D), const),
            resident(w_up.shape),
            resident(w_down.shape),
        ],
        out_specs=pl.BlockSpec((tm, D), row),
        out_shape=jax.ShapeDtypeStruct((N, D), F32),
        compiler_params=pltpu.CompilerParams(
            dimension_semantics=("parallel",), vmem_limit_bytes=VMEM_LIMIT_BYTES),
        name="out_ffn",
    )(x2, o_nsa, o_ret, w_out, ln2, w_up, w_down)


def _rope_tables(S):
    half = HEAD_DIM // 2
    inv = ROPE_THETA ** (-jnp.arange(half, dtype=F32) / half)
    ang = jnp.arange(S).astype(F32)[:, None] * inv[None, :]
    cos, sin = jnp.cos(ang), jnp.sin(ang)
    reps = LANES // HEAD_DIM
    cos_t = jnp.tile(jnp.concatenate([cos, cos], axis=-1), (1, reps))
    sin_t = jnp.tile(jnp.concatenate([-sin, sin], axis=-1), (1, reps))
    return cos_t, sin_t


def _layer(h, ln1_w, w_in, q_norm_w, k_norm_w, cmp_pe_k, cmp_pe_v, cmp_wk1, cmp_wk2,
           cmp_wv1, cmp_wv2, ret_norm_w, w_out, ln2_w, w_up, w_down):
    B, S, D = h.shape
    N = B * S
    G = NSA_KV_HEADS
    x2 = h.reshape(N, D)

    w_perm = _permute_w_in(w_in).astype(BF16)
    cos_t, sin_t = _rope_tables(S)
    reps = LANES // HEAD_DIM
    qn = jnp.tile(q_norm_w, reps)[None, :]
    kn = jnp.tile(k_norm_w, (1, reps))

    (q, kc, vc, ks, vs, kw, vw, gates, rq, rk, rv, rg) = _in_proj(
        x2, ln1_w[None, :], w_perm, cos_t, sin_t, qn, kn, B, S)

    seg_w = CMP_STRIDE * HEAD_DIM
    n_seg = S // CMP_STRIDE
    kcmp, vcmp = _compress(
        kc.reshape(B * G * n_seg, seg_w), vc.reshape(B * G * n_seg, seg_w),
        cmp_pe_k.reshape(2, seg_w), cmp_pe_v.reshape(2, seg_w),
        cmp_wk1.astype(BF16), cmp_wk2.astype(BF16), cmp_wv1.astype(BF16),
        jnp.pad(cmp_wv2, ((0, 0), (0, LANES - HEAD_DIM))).astype(BF16))
    kcmp = kcmp.reshape(B, G, n_seg, HEAD_DIM)

    o_nsa = _nsa(q, kcmp, vcmp, ks, vs, kw, vw, gates.reshape(B, S, LANES))
    o_ret = _retention(rq.reshape(B, S, -1), rk, rv.reshape(B, S, -1),
                       rg.reshape(B, S, -1), ret_norm_w.reshape(1, RET_V_WIDTH))

    out = _out_ffn(x2, o_nsa.reshape(N, -1), o_ret.reshape(N, -1), w_out.astype(BF16),
                   ln2_w[None, :], w_up.astype(BF16), w_down.astype(BF16))
    return out.reshape(B, S, D)


def kernel(x, ln1_w, w_in, q_norm_w, k_norm_w, cmp_pe_k, cmp_pe_v, cmp_wk1, cmp_wk2, cmp_wv1, cmp_wv2,
           ret_norm_w, w_out, ln2_w, w_up, w_down):
    h = x
    for l in range(ln1_w.shape[0]):
        h = _layer(h, ln1_w[l], w_in[l], q_norm_w[l], k_norm_w[l], cmp_pe_k[l], cmp_pe_v[l],
                   cmp_wk1[l], cmp_wk2[l], cmp_wv1[l], cmp_wv2[l], ret_norm_w[l], w_out[l],
                   ln2_w[l], w_up[l], w_down[l])
    return h
```

```python
import functools
import math

import jax
import jax.numpy as jnp
import numpy as np
from jax import lax
from jax.experimental import pallas as pl
from jax.experimental.pallas import tpu as pltpu

F32 = jnp.float32
BF16 = jnp.bfloat16

NSA_HEADS = 8
NSA_KV_HEADS = 2
NSA_GROUP = NSA_HEADS // NSA_KV_HEADS
HEAD_DIM = 64
CMP_BLOCK = 32
CMP_STRIDE = 16
CMP_HIDDEN = 256
SEL_BLOCK = 64
SEL_TOPK = 8
WINDOW = 256
FORCE_SCORE = 1.0e4
RET_HEADS = 4
RET_QK_DIM = 64
RET_V_DIM = 128
RET_CHUNK = 128
ROPE_THETA = 10000.0
EPS = 1e-6
NEG_INF = -1.0e30
SEL_BIAS = 2.0 ** 100

NSA_Q_DIM = NSA_HEADS * HEAD_DIM
NSA_KV_DIM = NSA_KV_HEADS * HEAD_DIM
NSA_GATE_DIM = 3 * NSA_HEADS
RET_QK_WIDTH = RET_HEADS * RET_QK_DIM
RET_V_WIDTH = RET_HEADS * RET_V_DIM

LANES = 128
VMEM_LIMIT_BYTES = 56 * 1024 * 1024

TM_PROJ = 512
TQ = 256
TK = 256
KV_CHUNK = 128
TM_FFN = 512
FF_CHUNK = 512
RET_BATCH = 4


def _dot(a, b):
    return jnp.dot(a, b, preferred_element_type=F32)


def _dot_nt(a, b):
    return lax.dot_general(a, b, (((1,), (1,)), ((), ())), preferred_element_type=F32)


def _sigmoid(x):
    return 1.0 / (1.0 + jnp.exp(-x))


_Q0 = 0
_K0 = _Q0 + NSA_Q_DIM
_V0 = _K0 + 3 * NSA_KV_DIM
_G0 = _V0 + 3 * NSA_KV_DIM
_RQ0 = _G0 + LANES
_RK0 = _RQ0 + RET_QK_WIDTH
_RV0 = _RK0 + RET_QK_WIDTH
_RG0 = _RV0 + RET_V_WIDTH
_W_IN_COLS = _RG0 + RET_V_WIDTH


def _permute_w_in(w_in):
    sizes = [NSA_Q_DIM] + [NSA_KV_DIM] * 6 + [NSA_GATE_DIM, RET_QK_WIDTH, RET_QK_WIDTH, RET_V_WIDTH, RET_V_WIDTH]
    off = [int(o) for o in np.concatenate([[0], np.cumsum(sizes)])]
    (q, kc, vc, ks, vs, kw, vw, gate, rq, rk, rv, rg) = [w_in[:, off[i]:off[i + 1]] for i in range(12)]
    d = w_in.shape[0]
    gate = gate.reshape(d, 3, NSA_KV_HEADS, NSA_GROUP).transpose(0, 2, 1, 3).reshape(d, NSA_KV_HEADS, 3 * NSA_GROUP)
    gate = jnp.pad(gate, ((0, 0), (0, 0), (0, HEAD_DIM - 3 * NSA_GROUP))).reshape(d, LANES)
    w = jnp.concatenate([q, kc, ks, kw, vc, vs, vw, gate, rq, rk, rv, rg], axis=1)
    assert w.shape[1] == _W_IN_COLS
    return w


def _in_proj_kernel(x_ref, ln_ref, w_ref, cos_ref, sin_ref, qn_ref, kn_ref,
                    q_ref, kc_ref, vc_ref, ks_ref, vs_ref, kw_ref, vw_ref, gate_ref,
                    rq_ref, rk_ref, rv_ref, rg_ref, *, tiles_per_seq):
    x = x_ref[...]
    ms = jnp.mean(x * x, axis=-1, keepdims=True)
    xn = (x * lax.rsqrt(ms + EPS) * ln_ref[...]).astype(BF16)
    cos = cos_ref[...]
    sin = sin_ref[...]
    lane = lax.broadcasted_iota(jnp.int32, (1, LANES), 1)
    low_half = (lane & (HEAD_DIM // 2)) == 0
    first_head = lane < HEAD_DIM

    def proj(a, b):
        return _dot(xn, w_ref[:, a:b])

    def rope(t):
        swapped = jnp.where(low_half, pltpu.roll(t, LANES - HEAD_DIM // 2, 1), pltpu.roll(t, HEAD_DIM // 2, 1))
        return t * cos + swapped * sin

    def head_norm(t, w):
        t2 = t * t
        s0 = jnp.sum(jnp.where(first_head, t2, 0.0), axis=-1, keepdims=True)
        s1 = jnp.sum(jnp.where(first_head, 0.0, t2), axis=-1, keepdims=True)
        msq = jnp.where(first_head, s0, s1) * (1.0 / HEAD_DIM)
        return t * lax.rsqrt(msq + EPS) * w

    def tiles(sec):
        return [sec[:, c * LANES:(c + 1) * LANES] for c in range(sec.shape[1] // LANES)]

    def split_heads(t, fill):
        return [jnp.where(first_head, t, fill), jnp.where(first_head, pltpu.roll(t, HEAD_DIM, 1), fill)]

    tm = x.shape[0]

    qn = qn_ref[...]
    for c, t in enumerate(tiles(proj(_Q0, _K0))):
        t = rope(head_norm(t, qn)) * (HEAD_DIM ** -0.5 * math.log2(math.e))
        for j, tj in enumerate(split_heads(t, 0.0)):
            h = 2 * c + j
            q_ref[0, h // NSA_GROUP, h % NSA_GROUP] = tj.astype(BF16)

    ksec = tiles(proj(_K0, _V0))
    kc = rope(head_norm(ksec[0], kn_ref[0:1, :]))
    ks = rope(head_norm(ksec[1], kn_ref[1:2, :]))
    kw = rope(head_norm(ksec[2], kn_ref[2:3, :])).astype(BF16)
    s0 = (pl.program_id(0) % tiles_per_seq) * tm
    blk = jnp.right_shift(s0 + lax.broadcasted_iota(jnp.int32, (tm, 1), 0), int(math.log2(SEL_BLOCK)))
    onehot = (lane - HEAD_DIM == blk).astype(F32)
    ks_aug = split_heads(ks, onehot)
    kc_ref[...] = kc
    for g in range(NSA_KV_HEADS):
        ks_ref[0, g] = ks_aug[g].astype(BF16)
        kw_ref[0, g] = kw[:, g * HEAD_DIM:(g + 1) * HEAD_DIM]

    def store_transposed(v, out_ref):
        chunk = out_ref.shape[-1]
        for c in range(v.shape[0] // chunk):
            vt = v[c * chunk:(c + 1) * chunk, :].T.astype(BF16)
            for g in range(NSA_KV_HEADS):
                out_ref[0, g, c] = vt[g * HEAD_DIM:(g + 1) * HEAD_DIM, :]

    vsec = tiles(proj(_V0, _RQ0))
    vc_ref[...] = vsec[0]
    store_transposed(vsec[1], vs_ref)
    store_transposed(vsec[2], vw_ref)
    gate_ref[...] = vsec[3]

    for c, t in enumerate(tiles(proj(_RQ0, _RK0))):
        rq_ref[:, c * LANES:(c + 1) * LANES] = rope(t).astype(BF16)
    for c, t in enumerate(tiles(proj(_RK0, _RV0))):
        t = rope(t) * (RET_QK_DIM ** -0.5)
        for j in range(tm // RET_CHUNK):
            rk_ref[0, j, c * LANES:(c + 1) * LANES, :] = t[j * RET_CHUNK:(j + 1) * RET_CHUNK, :].T.astype(BF16)
    rv_ref[...] = proj(_RV0, _RG0).astype(BF16)
    rg_ref[...] = proj(_RG0, _W_IN_COLS).astype(BF16)


def _in_proj(x2, ln1, w_perm, cos, sin, qn, kn, B, S):
    N, D = x2.shape
    tm = TM_PROJ
    spt = S // tm
    G, R, dh = NSA_KV_HEADS, NSA_GROUP, HEAD_DIM

    def row(i):
        return (i, 0)

    def hm(i):
        return (i // spt, 0, i % spt, 0)

    hm_spec = pl.BlockSpec((1, G, tm, dh), hm)
    out_shape = [
        jax.ShapeDtypeStruct((B, G, R, S, LANES), BF16),
        jax.ShapeDtypeStruct((N, LANES), F32),
        jax.ShapeDtypeStruct((N, LANES), F32),
        jax.ShapeDtypeStruct((B, G, S, LANES), BF16),
        jax.ShapeDtypeStruct((B, G, S // TK, dh, TK), BF16),
        jax.ShapeDtypeStruct((B, G, S, dh), BF16),
        jax.ShapeDtypeStruct((B, G, S // KV_CHUNK, dh, KV_CHUNK), BF16),
        jax.ShapeDtypeStruct((N, LANES), F32),
        jax.ShapeDtypeStruct((N, RET_QK_WIDTH), BF16),
        jax.ShapeDtypeStruct((B, S // RET_CHUNK, RET_QK_WIDTH, RET_CHUNK), BF16),
        jax.ShapeDtypeStruct((N, RET_V_WIDTH), BF16),
        jax.ShapeDtypeStruct((N, RET_V_WIDTH), BF16),
    ]
    out_specs = [
        pl.BlockSpec((1, G, R, tm, LANES), lambda i: (i // spt, 0, 0, i % spt, 0)),
        pl.BlockSpec((tm, LANES), row), pl.BlockSpec((tm, LANES), row), pl.BlockSpec((1, G, tm, LANES), hm),
        pl.BlockSpec((1, G, tm // TK, dh, TK), lambda i: (i // spt, 0, i % spt, 0, 0)),
        hm_spec,
        pl.BlockSpec((1, G, tm // KV_CHUNK, dh, KV_CHUNK), lambda i: (i // spt, 0, i % spt, 0, 0)),
        pl.BlockSpec((tm, LANES), row),
        pl.BlockSpec((tm, RET_QK_WIDTH), row),
        pl.BlockSpec((1, tm // RET_CHUNK, RET_QK_WIDTH, RET_CHUNK), lambda i: (i // spt, i % spt, 0, 0)),
        pl.BlockSpec((tm, RET_V_WIDTH), row),
        pl.BlockSpec((tm, RET_V_WIDTH), row),
    ]
    const = lambda i: (0, 0)
    in_specs = [
        pl.BlockSpec((tm, D), row),
        pl.BlockSpec((1, D), const),
        pl.BlockSpec((D, _W_IN_COLS), const, pipeline_mode=pl.Buffered(1)),
        pl.BlockSpec((tm, LANES), lambda i: (i % spt, 0)),
        pl.BlockSpec((tm, LANES), lambda i: (i % spt, 0)),
        pl.BlockSpec((1, LANES), const),
        pl.BlockSpec((3, LANES), const),
    ]
    return pl.pallas_call(
        functools.partial(_in_proj_kernel, tiles_per_seq=spt),
        grid=(N // tm,),
        in_specs=in_specs,
        out_specs=out_specs,
        out_shape=out_shape,
        compiler_params=pltpu.CompilerParams(
            dimension_semantics=("parallel",), vmem_limit_bytes=VMEM_LIMIT_BYTES),
        name="in_proj",
    )(x2, ln1, w_perm, cos, sin, qn, kn)


def _compress_kernel(k_ref, v_ref, pek_ref, pev_ref, wk1_ref, wk2_ref, wv1_ref, wv2_ref, ko_ref, vo_ref):
    n_seg = k_ref.shape[0] // CMP_STRIDE
    G = NSA_KV_HEADS

    def run(x_ref, pe_ref, w1_ref):
        first = second = None
        for l in range(CMP_STRIDE):
            x = x_ref[pl.ds(l, n_seg, stride=CMP_STRIDE), :]
            a = _dot((x + pe_ref[l:l + 1, :]).astype(BF16), w1_ref[l])
            b = _dot((x + pe_ref[CMP_STRIDE + l:CMP_STRIDE + l + 1, :]).astype(BF16), w1_ref[CMP_STRIDE + l])
            first = a if first is None else first + a
            second = b if second is None else second + b
        h = first + pltpu.roll(second, n_seg - 1, 0)
        return (h * _sigmoid(h)).astype(BF16)

    hk = run(k_ref, pek_ref, wk1_ref)
    hv = run(v_ref, pev_ref, wv1_ref)
    for g in range(G):
        cols = slice(g * CMP_HIDDEN, (g + 1) * CMP_HIDDEN)
        ko_ref[0, g] = _dot(hk[:, cols], wk2_ref[...]).astype(BF16)
        vo_ref[:, g * n_seg:(g + 1) * n_seg] = _dot(hv[:, cols], wv2_ref[...]).T[0:HEAD_DIM, :].astype(BF16)


def _compress(kc, vc, pek, pev, wk1, wk2, wv1, wv2, B, S):
    G = NSA_KV_HEADS
    n_seg = S // CMP_STRIDE
    width = kc.shape[1]
    const2 = lambda b: (0, 0)
    const3 = lambda b: (0, 0, 0)
    tok_spec = pl.BlockSpec((S, width), lambda b: (b, 0))
    return pl.pallas_call(
        _compress_kernel,
        grid=(B,),
        in_specs=[tok_spec, tok_spec,
                  pl.BlockSpec(pek.shape, const2), pl.BlockSpec(pev.shape, const2),
                  pl.BlockSpec(wk1.shape, const3), pl.BlockSpec(wk2.shape, const2),
                  pl.BlockSpec(wv1.shape, const3), pl.BlockSpec(wv2.shape, const2)],
        out_specs=[pl.BlockSpec((1, G, n_seg, HEAD_DIM), lambda b: (b, 0, 0, 0)),
                   pl.BlockSpec((HEAD_DIM, G * n_seg), lambda b: (0, b))],
        out_shape=[jax.ShapeDtypeStruct((B, G, n_seg, HEAD_DIM), BF16),
                   jax.ShapeDtypeStruct((HEAD_DIM, B * G * n_seg), BF16)],
        compiler_params=pltpu.CompilerParams(
            dimension_semantics=("parallel",), vmem_limit_bytes=VMEM_LIMIT_BYTES),
        name="compress",
    )(kc, vc, pek, pev, wk1, wk2, wv1, wv2)


def _block_diag_w1(w1):
    G = NSA_KV_HEADS
    w = w1.reshape(CMP_BLOCK, HEAD_DIM, CMP_HIDDEN)
    z = jnp.zeros_like(w)
    rows = [jnp.concatenate([w if j == g else z for j in range(G)], axis=2) for g in range(G)]
    return jnp.concatenate(rows, axis=1)


def _nsa_kernel(q_ref, kcmp_ref, vcmp_ref, ks_ref, vs_ref, kw_ref, vw_ref, gate_ref,
                o_ref, m_ref, l_ref, acc_ref, s0_ref):
    G, R = NSA_KV_HEADS, NSA_GROUP
    n_cp = kcmp_ref.shape[2]
    S = ks_ref.shape[2]
    n_cmp = (S - CMP_BLOCK) // CMP_STRIDE + 1
    n_blk = S // SEL_BLOCK
    qi = pl.program_id(1)
    q0 = qi * TQ
    t_q = q0 + lax.broadcasted_iota(jnp.int32, (1, TQ), 1)
    q_pad = [q_ref[0, g].reshape(R * TQ, LANES) for g in range(G)]
    qs = [q[:, 0:HEAD_DIM] for q in q_pad]

    def heads(x):
        return [x[:, r * TQ:(r + 1) * TQ] for r in range(R)]

    def masked_exp(s_heads, mask):
        es, invs = [], []
        for s in s_heads:
            s = jnp.where(mask, s, NEG_INF)
            e = jnp.exp2(s - jnp.max(s, axis=0, keepdims=True))
            es.append(e)
            invs.append(1.0 / jnp.sum(e, axis=0, keepdims=True))
        return es, invs

    c_idx = lax.broadcasted_iota(jnp.int32, (n_cp, 1), 0)
    cmask = ((c_idx * CMP_STRIDE + (CMP_BLOCK - 1)) <= t_q) & (c_idx < n_cmp)
    jj = lax.broadcasted_iota(jnp.int32, (n_blk, n_cp), 0)
    cc = lax.broadcasted_iota(jnp.int32, (n_blk, n_cp), 1)
    overlap = ((cc * CMP_STRIDE < (jj + 1) * SEL_BLOCK) & (cc * CMP_STRIDE + CMP_BLOCK > jj * SEL_BLOCK)
               & (cc < n_cmp)).astype(BF16)
    jb = lax.broadcasted_iota(jnp.int32, (n_blk, 1), 0)
    cur = jnp.right_shift(t_q, int(math.log2(SEL_BLOCK)))
    forced = (jb == 0) | (jb == cur) | (jb == cur - 1)
    valid = jb <= cur

    n_wc = (TQ + WINDOW) // KV_CHUNK
    c0 = jnp.maximum(q0 // KV_CHUNK - WINDOW // KV_CHUNK, 0)
    w0 = pl.multiple_of(c0 * KV_CHUNK, KV_CHUNK)
    diff = t_q - (w0 + lax.broadcasted_iota(jnp.int32, (n_wc * KV_CHUNK, 1), 0))
    wmask = (diff >= 0) & (diff < WINDOW)

    s_cmp = [_dot_nt(kcmp_ref[0, g], qs[g]) for g in range(G)]
    s_win = [_dot_nt(kw_ref[0, g, pl.ds(w0, n_wc * KV_CHUNK), :], qs[g]) for g in range(G)]
    any_cmp = t_q >= CMP_BLOCK - 1
    p_cmp, o_cmp, o_win = [], [], []
    for g in range(G):
        es, invs = masked_exp(heads(s_cmp[g]), cmask)
        p_cmp.append([e * jnp.where(any_cmp, inv, 0.0) for e, inv in zip(es, invs)])
    for g in range(G):
        o_cmp.append(_dot(vcmp_ref[:, g * n_cp:(g + 1) * n_cp],
                          jnp.concatenate(p_cmp[g], axis=1).astype(BF16)))
    for g in range(G):
        es, invs = masked_exp(heads(s_win[g]), wmask)
        e_w = jnp.concatenate(es, axis=1).astype(BF16)
        acc = _dot(vw_ref[0, g, c0], e_w[0:KV_CHUNK])
        for j in range(1, n_wc):
            acc = acc + _dot(vw_ref[0, g, c0 + j], e_w[j * KV_CHUNK:(j + 1) * KV_CHUNK])
        o_win.append(acc * jnp.concatenate(invs, axis=1))

    sels = []
    for g in range(G):
        p_c = p_cmp[g]
        p_sum = p_c[0]
        for r in range(1, R):
            p_sum = p_sum + p_c[r]
        p_hi = p_sum.astype(BF16)
        p_lo = (p_sum - p_hi.astype(F32)).astype(BF16)
        p_slc = _dot(overlap, p_hi) + _dot(overlap, p_lo)
        score = jnp.where(forced, FORCE_SCORE, jnp.where(valid, p_slc, -1.0))
        rank = jnp.zeros((n_blk, TQ), jnp.int32)
        for i in range(n_blk):
            row = score[i:i + 1, :]
            before = (row > score) | ((row == score) & (jb > i))
            rank = rank + before.astype(jnp.int32)
        bias = jnp.where(rank < min(SEL_TOPK, n_blk), 0.0, -SEL_BIAS)
        bias = jnp.concatenate([jnp.zeros((HEAD_DIM, TQ), F32), bias,
                                jnp.zeros((LANES - HEAD_DIM - n_blk, TQ), F32)], axis=0).T.astype(BF16)
        sels.append(jnp.concatenate([q_pad[g][r * TQ:(r + 1) * TQ] + bias for r in range(R)], axis=0))

    m_ref[...] = jnp.full(m_ref.shape, NEG_INF, F32)
    l_ref[...] = jnp.zeros(l_ref.shape, F32)
    acc_ref[...] = jnp.zeros(acc_ref.shape, F32)

    def scores(g, kt):
        return _dot_nt(ks_ref[0, g, pl.ds(pl.multiple_of(kt * TK, TK), TK), :], sels[g])

    def accumulate(g, kt, s, causal):
        if causal is not None:
            s = jnp.concatenate([jnp.where(causal, sh, NEG_INF) for sh in heads(s)], axis=1)
        m_old = m_ref[g]
        m_new = jnp.maximum(m_old, jnp.max(s, axis=0, keepdims=True))
        alpha = jnp.exp2(m_old - m_new)
        p = jnp.exp2(s - m_new)
        l_ref[g] = alpha * l_ref[g] + jnp.sum(p, axis=0, keepdims=True)
        acc_ref[g] = alpha * acc_ref[g] + _dot(vs_ref[0, g, kt], p.astype(BF16))
        m_ref[g] = m_new

    assert G == 2
    s0_ref[...] = scores(0, 0)

    def interior_tile(kt, carry):
        s1 = scores(1, kt)
        accumulate(0, kt, s0_ref[...], None)
        s0_ref[...] = scores(0, kt + 1)
        accumulate(1, kt, s1, None)
        return carry

    lax.fori_loop(0, qi, interior_tile, 0)
    causal = (q0 + lax.broadcasted_iota(jnp.int32, (TK, 1), 0)) <= t_q
    s1 = scores(1, qi)
    accumulate(0, qi, s0_ref[...], causal)
    accumulate(1, qi, s1, causal)

    gates = _sigmoid(gate_ref[0].T)
    outs = []
    for g in range(G):
        o_sel = acc_ref[g] * (1.0 / l_ref[g])
        for r, (oc, os_, ow) in enumerate(zip(heads(o_cmp[g]), heads(o_sel), heads(o_win[g]))):
            g0 = g * HEAD_DIM + r
            outs.append(gates[g0:g0 + 1] * oc + gates[g0 + R:g0 + R + 1] * os_
                        + gates[g0 + 2 * R:g0 + 2 * R + 1] * ow)
    o_ref[0] = jnp.concatenate(outs, axis=0).T.astype(BF16)


def _nsa(q, kcmp, vcmp, ks, vs, kw, vw, gates):
    B, G, R, S, _ = q.shape
    dh = HEAD_DIM
    n_cp = kcmp.shape[2]
    assert TQ == TK, "the key sweep treats exactly one tile per query tile as the diagonal"
    k_spec = lambda a: pl.BlockSpec((1,) + a.shape[1:], lambda b, i: (b, 0, 0, 0))
    vt_spec = lambda a: pl.BlockSpec((1,) + a.shape[1:], lambda b, i: (b, 0, 0, 0, 0))
    return pl.pallas_call(
        _nsa_kernel,
        grid=(B, S // TQ),
        in_specs=[
            pl.BlockSpec((1, G, R, TQ, LANES), lambda b, i: (b, 0, 0, i, 0)),
            pl.BlockSpec((1, G, n_cp, dh), lambda b, i: (b, 0, 0, 0)),
            pl.BlockSpec((dh, G * n_cp), lambda b, i: (0, b)),
            k_spec(ks), vt_spec(vs), k_spec(kw), vt_spec(vw),
            pl.BlockSpec((1, TQ, LANES), lambda b, i: (b, i, 0)),
        ],
        out_specs=pl.BlockSpec((1, TQ, G * R * dh), lambda b, i: (b, i, 0)),
        out_shape=jax.ShapeDtypeStruct((B, S, G * R * dh), BF16),
        scratch_shapes=[pltpu.VMEM((G, 1, R * TQ), F32), pltpu.VMEM((G, 1, R * TQ), F32),
                        pltpu.VMEM((G, dh, R * TQ), F32), pltpu.VMEM((TK, R * TQ), F32)],
        compiler_params=pltpu.CompilerParams(
            dimension_semantics=("parallel", "arbitrary"), vmem_limit_bytes=VMEM_LIMIT_BYTES),
        name="nsa_attention",
    )(q, kcmp, vcmp, ks, vs, kw, vw, gates)


def _retention_kernel(q_ref, kt_ref, v_ref, g_ref, w_ref, o_ref, state_ref):
    C = RET_CHUNK
    NB = q_ref.shape[0]

    @pl.when(pl.program_id(1) == 0)
    def _():
        state_ref[...] = jnp.zeros(state_ref.shape, F32)

    i_col = lax.broadcasted_iota(jnp.int32, (C, 1), 0)
    i_row = lax.broadcasted_iota(jnp.int32, (1, C), 1)
    d_int = i_col - i_row
    log_gamma = [math.log(1.0 - 2.0 ** (-5.0 - h)) for h in range(RET_HEADS)]
    dmat = [jnp.where(d_int >= 0, jnp.exp(lg * jnp.maximum(d_int, 0).astype(F32)), 0.0) for lg in log_gamma]
    xi = [jnp.exp(lg * (i_col + 1).astype(F32)) for lg in log_gamma]
    zeta = [jnp.exp(lg * (C - 1 - i_row).astype(F32)) for lg in log_gamma]
    gamma_c = [math.exp(lg * C) for lg in log_gamma]

    units = [(n, h) for n in range(NB) for h in range(RET_HEADS)]
    q, kt, v, st = {}, {}, {}, {}
    for n, h in units:
        q[n, h] = q_ref[n, :, h * RET_QK_DIM:(h + 1) * RET_QK_DIM]
        kt[n, h] = kt_ref[n, 0, h * RET_QK_DIM:(h + 1) * RET_QK_DIM, :]
        v[n, h] = v_ref[n, :, h * RET_V_DIM:(h + 1) * RET_V_DIM]
        st[n, h] = state_ref[n, h]
    inner = {u: _dot(q[u], kt[u]) for u in units}
    cross = {u: _dot(q[u], st[u].astype(BF16)) for u in units}
    kv = {u: _dot((kt[u].astype(F32) * zeta[u[1]]).astype(BF16), v[u]) for u in units}
    for n, h in units:
        u = (n, h)
        vsl = slice(h * RET_V_DIM, (h + 1) * RET_V_DIM)
        y = _dot((inner[u] * dmat[h]).astype(BF16), v[u]) + cross[u] * xi[h]
        state_ref[n, h] = gamma_c[h] * st[u] + kv[u]

        mu = jnp.mean(y, axis=-1, keepdims=True)
        yc = y - mu
        var = jnp.mean(yc * yc, axis=-1, keepdims=True)
        yn = yc * lax.rsqrt(var + EPS) * w_ref[:, vsl]
        gate = g_ref[n, :, vsl].astype(F32)
        o_ref[n, :, vsl] = (gate * _sigmoid(gate) * yn).astype(BF16)


def _retention(rq, rkt, rv, rg, w):
    B, S, _ = rq.shape
    C = RET_CHUNK
    nb = math.gcd(RET_BATCH, B)
    q_spec = pl.BlockSpec((nb, C, RET_QK_WIDTH), lambda b, c: (b, c, 0))
    kt_spec = pl.BlockSpec((nb, 1, RET_QK_WIDTH, C), lambda b, c: (b, c, 0, 0))
    v_spec = pl.BlockSpec((nb, C, RET_V_WIDTH), lambda b, c: (b, c, 0))
    return pl.pallas_call(
        _retention_kernel,
        grid=(B // nb, S // C),
        in_specs=[q_spec, kt_spec, v_spec, v_spec, pl.BlockSpec((1, RET_V_WIDTH), lambda b, c: (0, 0))],
        out_specs=v_spec,
        out_shape=jax.ShapeDtypeStruct((B, S, RET_V_WIDTH), BF16),
        scratch_shapes=[pltpu.VMEM((nb, RET_HEADS, RET_QK_DIM, RET_V_DIM), F32)],
        compiler_params=pltpu.CompilerParams(
            dimension_semantics=("parallel", "arbitrary"), vmem_limit_bytes=VMEM_LIMIT_BYTES),
        name="retention",
    )(rq, rkt, rv, rg, w)


def _out_ffn_kernel(x_ref, a_ref, r_ref, wo_ref, ln_ref, wu_ref, wd_ref, o_ref):
    na = a_ref.shape[1]
    mix = _dot(a_ref[...], wo_ref[0:na, :]) + _dot(r_ref[...], wo_ref[na:, :])
    h = x_ref[...] + mix
    ms = jnp.mean(h * h, axis=-1, keepdims=True)
    hn = (h * lax.rsqrt(ms + EPS) * ln_ref[...]).astype(BF16)
    d_ff = wu_ref.shape[1]
    acc = None
    for f in range(d_ff // FF_CHUNK):
        cols = slice(f * FF_CHUNK, (f + 1) * FF_CHUNK)
        u = jnp.maximum(_dot(hn, wu_ref[:, cols]), 0.0)
        d = _dot((u * u).astype(BF16), wd_ref[cols, :])
        acc = d if acc is None else acc + d
    o_ref[...] = h + acc


def _out_ffn(x2, o_nsa, o_ret, w_out, ln2, w_up, w_down):
    N, D = x2.shape
    tm = TM_FFN
    row = lambda i: (i, 0)
    const = lambda i: (0, 0)
    resident = functools.partial(pl.BlockSpec, index_map=const, pipeline_mode=pl.Buffered(1))
    return pl.pallas_call(
        _out_ffn_kernel,
        grid=(N // tm,),
        in_specs=[
            pl.BlockSpec((tm, D), row),
            pl.BlockSpec((tm, o_nsa.shape[1]), row),
            pl.BlockSpec((tm, o_ret.shape[1]), row),
            resident(w_out.shape),
            pl.BlockSpec((1, D), const),
            resident(w_up.shape),
            resident(w_down.shape),
        ],
        out_specs=pl.BlockSpec((tm, D), row),
        out_shape=jax.ShapeDtypeStruct((N, D), F32),
        compiler_params=pltpu.CompilerParams(
            dimension_semantics=("parallel",), vmem_limit_bytes=VMEM_LIMIT_BYTES),
        name="out_ffn",
    )(x2, o_nsa, o_ret, w_out, ln2, w_up, w_down)


def _rope_tables(S):
    half = HEAD_DIM // 2
    inv = ROPE_THETA ** (-jnp.arange(half, dtype=F32) / half)
    ang = jnp.arange(S).astype(F32)[:, None] * inv[None, :]
    cos, sin = jnp.cos(ang), jnp.sin(ang)
    reps = LANES // HEAD_DIM
    cos_t = jnp.tile(jnp.concatenate([cos, cos], axis=-1), (1, reps))
    sin_t = jnp.tile(jnp.concatenate([-sin, sin], axis=-1), (1, reps))
    return cos_t, sin_t


def _layer(h, ln1_w, w_in, q_norm_w, k_norm_w, cmp_pe_k, cmp_pe_v, cmp_wk1, cmp_wk2,
           cmp_wv1, cmp_wv2, ret_norm_w, w_out, ln2_w, w_up, w_down):
    B, S, D = h.shape
    N = B * S
    G = NSA_KV_HEADS
    x2 = h.reshape(N, D)

    w_perm = _permute_w_in(w_in).astype(BF16)
    cos_t, sin_t = _rope_tables(S)
    reps = LANES // HEAD_DIM
    qn = jnp.tile(q_norm_w, reps)[None, :]
    kn = jnp.tile(k_norm_w, (1, reps))

    (q, kc, vc, ks, vs, kw, vw, gates, rq, rk, rv, rg) = _in_proj(
        x2, ln1_w[None, :], w_perm, cos_t, sin_t, qn, kn, B, S)

    kcmp, vcmp = _compress(
        kc, vc, jnp.tile(cmp_pe_k, (1, G)), jnp.tile(cmp_pe_v, (1, G)),
        _block_diag_w1(cmp_wk1).astype(BF16), cmp_wk2.astype(BF16), _block_diag_w1(cmp_wv1).astype(BF16),
        jnp.pad(cmp_wv2, ((0, 0), (0, LANES - HEAD_DIM))).astype(BF16), B, S)

    o_nsa = _nsa(q, kcmp, vcmp, ks, vs, kw, vw, gates.reshape(B, S, LANES))
    o_ret = _retention(rq.reshape(B, S, -1), rk, rv.reshape(B, S, -1),
                       rg.reshape(B, S, -1), ret_norm_w.reshape(1, RET_V_WIDTH))

    out = _out_ffn(x2, o_nsa.reshape(N, -1), o_ret.reshape(N, -1), w_out.astype(BF16),
                   ln2_w[None, :], w_up.astype(BF16), w_down.astype(BF16))
    return out.reshape(B, S, D)


def kernel(x, ln1_w, w_in, q_norm_w, k_norm_w, cmp_pe_k, cmp_pe_v, cmp_wk1, cmp_wk2, cmp_wv1, cmp_wv2,
           ret_norm_w, w_out, ln2_w, w_up, w_down):
    h = x
    for l in range(ln1_w.shape[0]):
        h = _layer(h, ln1_w[l], w_in[l], q_norm_w[l], k_norm_w[l], cmp_pe_k[l], cmp_pe_v[l],
                   cmp_wk1[l], cmp_wk2[l], cmp_wv1[l], cmp_wv2[l], ret_norm_w[l], w_out[l],
                   ln2_w[l], w_up[l], w_down[l])
    return h
```

```python
import functools
import math

import jax
import jax.numpy as jnp
import numpy as np
from jax import lax
from jax.experimental import pallas as pl
from jax.experimental.pallas import tpu as pltpu

F32 = jnp.float32
BF16 = jnp.bfloat16

NSA_HEADS = 8
NSA_KV_HEADS = 2
NSA_GROUP = NSA_HEADS // NSA_KV_HEADS
HEAD_DIM = 64
CMP_BLOCK = 32
CMP_STRIDE = 16
CMP_HIDDEN = 256
SEL_BLOCK = 64
SEL_TOPK = 8
WINDOW = 256
FORCE_SCORE = 1.0e4
RET_HEADS = 4
RET_QK_DIM = 64
RET_V_DIM = 128
RET_CHUNK = 128
ROPE_THETA = 10000.0
EPS = 1e-6
NEG_INF = -1.0e30
SEL_BIAS = 2.0 ** 100

NSA_Q_DIM = NSA_HEADS * HEAD_DIM
NSA_KV_DIM = NSA_KV_HEADS * HEAD_DIM
NSA_GATE_DIM = 3 * NSA_HEADS
RET_QK_WIDTH = RET_HEADS * RET_QK_DIM
RET_V_WIDTH = RET_HEADS * RET_V_DIM

LANES = 128
VMEM_LIMIT_BYTES = 56 * 1024 * 1024

TM_PROJ = 512
TQ = 256
TK = 256
KV_CHUNK = 128
ONES_ROWS = 16
TM_FFN = 512
FF_CHUNK = 512
RET_BATCH = 8


def _dot(a, b):
    return jnp.dot(a, b, preferred_element_type=F32)


def _dot_nt(a, b):
    return lax.dot_general(a, b, (((1,), (1,)), ((), ())), preferred_element_type=F32)


def _sigmoid(x):
    return 1.0 / (1.0 + jnp.exp(-x))


_Q0 = 0
_K0 = _Q0 + NSA_Q_DIM
_V0 = _K0 + 3 * NSA_KV_DIM
_G0 = _V0 + 3 * NSA_KV_DIM
_RQ0 = _G0 + LANES
_RK0 = _RQ0 + RET_QK_WIDTH
_RV0 = _RK0 + RET_QK_WIDTH
_RG0 = _RV0 + RET_V_WIDTH
_W_IN_COLS = _RG0 + RET_V_WIDTH


def _permute_w_in(w_in):
    sizes = [NSA_Q_DIM] + [NSA_KV_DIM] * 6 + [NSA_GATE_DIM, RET_QK_WIDTH, RET_QK_WIDTH, RET_V_WIDTH, RET_V_WIDTH]
    off = [int(o) for o in np.concatenate([[0], np.cumsum(sizes)])]
    (q, kc, vc, ks, vs, kw, vw, gate, rq, rk, rv, rg) = [w_in[:, off[i]:off[i + 1]] for i in range(12)]
    d = w_in.shape[0]
    gate = gate.reshape(d, 3, NSA_KV_HEADS, NSA_GROUP).transpose(0, 2, 1, 3).reshape(d, NSA_KV_HEADS, 3 * NSA_GROUP)
    gate = jnp.pad(gate, ((0, 0), (0, 0), (0, HEAD_DIM - 3 * NSA_GROUP))).reshape(d, LANES)
    w = jnp.concatenate([q, kc, ks, kw, vc, vs, vw, gate, rq, rk, rv, rg], axis=1)
    assert w.shape[1] == _W_IN_COLS
    return w


def _in_proj_kernel(x_ref, ln_ref, w_ref, cos_ref, sin_ref, qn_ref, kn_ref,
                    q_ref, kc_ref, vc_ref, ks_ref, vs_ref, kw_ref, vw_ref, gate_ref,
                    rq_ref, rk_ref, rv_ref, rg_ref, *, tiles_per_seq):
    x = x_ref[...]
    ms = jnp.mean(x * x, axis=-1, keepdims=True)
    xn = (x * lax.rsqrt(ms + EPS) * ln_ref[...]).astype(BF16)
    cos = cos_ref[...]
    sin = sin_ref[...]
    lane = lax.broadcasted_iota(jnp.int32, (1, LANES), 1)
    low_half = (lane & (HEAD_DIM // 2)) == 0
    first_head = lane < HEAD_DIM

    def proj(a, b):
        return _dot(xn, w_ref[:, a:b])

    def rope(t):
        swapped = jnp.where(low_half, pltpu.roll(t, LANES - HEAD_DIM // 2, 1), pltpu.roll(t, HEAD_DIM // 2, 1))
        return t * cos + swapped * sin

    def head_norm(t, w):
        t2 = t * t
        s0 = jnp.sum(jnp.where(first_head, t2, 0.0), axis=-1, keepdims=True)
        s1 = jnp.sum(jnp.where(first_head, 0.0, t2), axis=-1, keepdims=True)
        msq = jnp.where(first_head, s0, s1) * (1.0 / HEAD_DIM)
        return t * lax.rsqrt(msq + EPS) * w

    def tiles(sec):
        return [sec[:, c * LANES:(c + 1) * LANES] for c in range(sec.shape[1] // LANES)]

    def split_heads(t, fill):
        return [jnp.where(first_head, t, fill), jnp.where(first_head, pltpu.roll(t, HEAD_DIM, 1), fill)]

    tm = x.shape[0]

    qn = qn_ref[...]
    for c, t in enumerate(tiles(proj(_Q0, _K0))):
        t = rope(head_norm(t, qn)) * (HEAD_DIM ** -0.5 * math.log2(math.e))
        for j, tj in enumerate(split_heads(t, 0.0)):
            h = 2 * c + j
            q_ref[0, h // NSA_GROUP, h % NSA_GROUP] = tj.astype(BF16)

    ksec = tiles(proj(_K0, _V0))
    kc = rope(head_norm(ksec[0], kn_ref[0:1, :]))
    ks = rope(head_norm(ksec[1], kn_ref[1:2, :]))
    kw = rope(head_norm(ksec[2], kn_ref[2:3, :])).astype(BF16)
    s0 = (pl.program_id(0) % tiles_per_seq) * tm
    blk = jnp.right_shift(s0 + lax.broadcasted_iota(jnp.int32, (tm, 1), 0), int(math.log2(SEL_BLOCK)))
    onehot = (lane - HEAD_DIM == blk).astype(F32)
    ks_aug = split_heads(ks, onehot)
    kc_ref[...] = kc
    for g in range(NSA_KV_HEADS):
        ks_ref[0, g] = ks_aug[g].astype(BF16)
        kw_ref[0, g] = kw[:, g * HEAD_DIM:(g + 1) * HEAD_DIM]

    def store_transposed(v, out_ref):
        chunk = out_ref.shape[-1]
        ones = jnp.ones((ONES_ROWS, chunk), BF16)
        for c in range(v.shape[0] // chunk):
            vt = v[c * chunk:(c + 1) * chunk, :].T.astype(BF16)
            for g in range(NSA_KV_HEADS):
                out_ref[0, g, c, 0:HEAD_DIM, :] = vt[g * HEAD_DIM:(g + 1) * HEAD_DIM, :]
                out_ref[0, g, c, HEAD_DIM:HEAD_DIM + ONES_ROWS, :] = ones

    vsec = tiles(proj(_V0, _RQ0))
    vc_ref[...] = vsec[0]
    store_transposed(vsec[1], vs_ref)
    store_transposed(vsec[2], vw_ref)
    gate_ref[...] = vsec[3]

    for c, t in enumerate(tiles(proj(_RQ0, _RK0))):
        rq_ref[:, c * LANES:(c + 1) * LANES] = rope(t).astype(BF16)
    for c, t in enumerate(tiles(proj(_RK0, _RV0))):
        t = rope(t) * (RET_QK_DIM ** -0.5)
        for j in range(tm // RET_CHUNK):
            rk_ref[0, j, c * LANES:(c + 1) * LANES, :] = t[j * RET_CHUNK:(j + 1) * RET_CHUNK, :].T.astype(BF16)
    rv_ref[...] = proj(_RV0, _RG0).astype(BF16)
    rg_ref[...] = proj(_RG0, _W_IN_COLS).astype(BF16)


def _in_proj(x2, ln1, w_perm, cos, sin, qn, kn, B, S):
    N, D = x2.shape
    tm = TM_PROJ
    spt = S // tm
    G, R, dh = NSA_KV_HEADS, NSA_GROUP, HEAD_DIM

    def row(i):
        return (i, 0)

    def hm(i):
        return (i // spt, 0, i % spt, 0)

    hm_spec = pl.BlockSpec((1, G, tm, dh), hm)
    out_shape = [
        jax.ShapeDtypeStruct((B, G, R, S, LANES), BF16),
        jax.ShapeDtypeStruct((N, LANES), F32),
        jax.ShapeDtypeStruct((N, LANES), F32),
        jax.ShapeDtypeStruct((B, G, S, LANES), BF16),
        jax.ShapeDtypeStruct((B, G, S // TK, dh + ONES_ROWS, TK), BF16),
        jax.ShapeDtypeStruct((B, G, S, dh), BF16),
        jax.ShapeDtypeStruct((B, G, S // KV_CHUNK, dh + ONES_ROWS, KV_CHUNK), BF16),
        jax.ShapeDtypeStruct((N, LANES), F32),
        jax.ShapeDtypeStruct((N, RET_QK_WIDTH), BF16),
        jax.ShapeDtypeStruct((B, S // RET_CHUNK, RET_QK_WIDTH, RET_CHUNK), BF16),
        jax.ShapeDtypeStruct((N, RET_V_WIDTH), BF16),
        jax.ShapeDtypeStruct((N, RET_V_WIDTH), BF16),
    ]
    out_specs = [
        pl.BlockSpec((1, G, R, tm, LANES), lambda i: (i // spt, 0, 0, i % spt, 0)),
        pl.BlockSpec((tm, LANES), row), pl.BlockSpec((tm, LANES), row), pl.BlockSpec((1, G, tm, LANES), hm),
        pl.BlockSpec((1, G, tm // TK, dh + ONES_ROWS, TK), lambda i: (i // spt, 0, i % spt, 0, 0)),
        hm_spec,
        pl.BlockSpec((1, G, tm // KV_CHUNK, dh + ONES_ROWS, KV_CHUNK), lambda i: (i // spt, 0, i % spt, 0, 0)),
        pl.BlockSpec((tm, LANES), row),
        pl.BlockSpec((tm, RET_QK_WIDTH), row),
        pl.BlockSpec((1, tm // RET_CHUNK, RET_QK_WIDTH, RET_CHUNK), lambda i: (i // spt, i % spt, 0, 0)),
        pl.BlockSpec((tm, RET_V_WIDTH), row),
        pl.BlockSpec((tm, RET_V_WIDTH), row),
    ]
    const = lambda i: (0, 0)
    in_specs = [
        pl.BlockSpec((tm, D), row),
        pl.BlockSpec((1, D), const),
        pl.BlockSpec((D, _W_IN_COLS), const, pipeline_mode=pl.Buffered(1)),
        pl.BlockSpec((tm, LANES), lambda i: (i % spt, 0)),
        pl.BlockSpec((tm, LANES), lambda i: (i % spt, 0)),
        pl.BlockSpec((1, LANES), const),
        pl.BlockSpec((3, LANES), const),
    ]
    return pl.pallas_call(
        functools.partial(_in_proj_kernel, tiles_per_seq=spt),
        grid=(N // tm,),
        in_specs=in_specs,
        out_specs=out_specs,
        out_shape=out_shape,
        compiler_params=pltpu.CompilerParams(
            dimension_semantics=("parallel",), vmem_limit_bytes=VMEM_LIMIT_BYTES),
        name="in_proj",
    )(x2, ln1, w_perm, cos, sin, qn, kn)


def _compress_kernel(k_ref, v_ref, pek_ref, pev_ref, wk1_ref, wk2_ref, wv1_ref, wv2_ref, ko_ref, vo_ref):
    n_seg = k_ref.shape[0] // CMP_STRIDE
    G = NSA_KV_HEADS

    def run(x_ref, pe_ref, w1_ref):
        first = second = None
        for l in range(CMP_STRIDE):
            x = x_ref[pl.ds(l, n_seg, stride=CMP_STRIDE), :]
            a = _dot((x + pe_ref[l:l + 1, :]).astype(BF16), w1_ref[l])
            b = _dot((x + pe_ref[CMP_STRIDE + l:CMP_STRIDE + l + 1, :]).astype(BF16), w1_ref[CMP_STRIDE + l])
            first = a if first is None else first + a
            second = b if second is None else second + b
        h = first + pltpu.roll(second, n_seg - 1, 0)
        return (h * _sigmoid(h)).astype(BF16)

    hk = run(k_ref, pek_ref, wk1_ref)
    hv = run(v_ref, pev_ref, wv1_ref)
    for g in range(G):
        cols = slice(g * CMP_HIDDEN, (g + 1) * CMP_HIDDEN)
        ko_ref[0, g] = _dot(hk[:, cols], wk2_ref[...]).astype(BF16)
        vo_ref[:, g * n_seg:(g + 1) * n_seg] = _dot(hv[:, cols], wv2_ref[...]).T[0:HEAD_DIM, :].astype(BF16)


def _compress(kc, vc, pek, pev, wk1, wk2, wv1, wv2, B, S):
    G = NSA_KV_HEADS
    n_seg = S // CMP_STRIDE
    width = kc.shape[1]
    const2 = lambda b: (0, 0)
    const3 = lambda b: (0, 0, 0)
    tok_spec = pl.BlockSpec((S, width), lambda b: (b, 0))
    return pl.pallas_call(
        _compress_kernel,
        grid=(B,),
        in_specs=[tok_spec, tok_spec,
                  pl.BlockSpec(pek.shape, const2), pl.BlockSpec(pev.shape, const2),
                  pl.BlockSpec(wk1.shape, const3), pl.BlockSpec(wk2.shape, const2),
                  pl.BlockSpec(wv1.shape, const3), pl.BlockSpec(wv2.shape, const2)],
        out_specs=[pl.BlockSpec((1, G, n_seg, HEAD_DIM), lambda b: (b, 0, 0, 0)),
                   pl.BlockSpec((HEAD_DIM, G * n_seg), lambda b: (0, b))],
        out_shape=[jax.ShapeDtypeStruct((B, G, n_seg, HEAD_DIM), BF16),
                   jax.ShapeDtypeStruct((HEAD_DIM, B * G * n_seg), BF16)],
        compiler_params=pltpu.CompilerParams(
            dimension_semantics=("parallel",), vmem_limit_bytes=VMEM_LIMIT_BYTES),
        name="compress",
    )(kc, vc, pek, pev, wk1, wk2, wv1, wv2)


def _block_diag_w1(w1):
    G = NSA_KV_HEADS
    w = w1.reshape(CMP_BLOCK, HEAD_DIM, CMP_HIDDEN)
    z = jnp.zeros_like(w)
    rows = [jnp.concatenate([w if j == g else z for j in range(G)], axis=2) for g in range(G)]
    return jnp.concatenate(rows, axis=1)


def _nsa_kernel(q_ref, kcmp_ref, vcmp_ref, ks_ref, vs_ref, kw_ref, vw_ref, gate_ref,
                o_ref, m_ref, acc_ref, s0_ref):
    G, R = NSA_KV_HEADS, NSA_GROUP
    n_cp = kcmp_ref.shape[2]
    S = ks_ref.shape[2]
    n_cmp = (S - CMP_BLOCK) // CMP_STRIDE + 1
    n_blk = S // SEL_BLOCK
    qi = pl.program_id(1)
    q0 = qi * TQ
    t_q = q0 + lax.broadcasted_iota(jnp.int32, (1, TQ), 1)
    q_pad = [q_ref[0, g].reshape(R * TQ, LANES) for g in range(G)]
    qs = [q[:, 0:HEAD_DIM] for q in q_pad]

    def heads(x):
        return [x[:, r * TQ:(r + 1) * TQ] for r in range(R)]

    def masked_exp(s_heads, mask):
        es = []
        for s in s_heads:
            s = jnp.where(mask, s, NEG_INF)
            es.append(jnp.exp2(s - jnp.max(s, axis=0, keepdims=True)))
        return es

    c_idx = lax.broadcasted_iota(jnp.int32, (n_cp, 1), 0)
    cmask = ((c_idx * CMP_STRIDE + (CMP_BLOCK - 1)) <= t_q) & (c_idx < n_cmp)
    jj = lax.broadcasted_iota(jnp.int32, (n_blk, n_cp), 0)
    cc = lax.broadcasted_iota(jnp.int32, (n_blk, n_cp), 1)
    overlap = ((cc * CMP_STRIDE < (jj + 1) * SEL_BLOCK) & (cc * CMP_STRIDE + CMP_BLOCK > jj * SEL_BLOCK)
               & (cc < n_cmp)).astype(BF16)
    jb = lax.broadcasted_iota(jnp.int32, (n_blk, 1), 0)
    cur = jnp.right_shift(t_q, int(math.log2(SEL_BLOCK)))
    forced = (jb == 0) | (jb == cur) | (jb == cur - 1)
    valid = jb <= cur

    n_wc = (TQ + WINDOW) // KV_CHUNK
    c0 = jnp.maximum(q0 // KV_CHUNK - WINDOW // KV_CHUNK, 0)
    w0 = pl.multiple_of(c0 * KV_CHUNK, KV_CHUNK)
    diff = t_q - (w0 + lax.broadcasted_iota(jnp.int32, (n_wc * KV_CHUNK, 1), 0))
    wmask = (diff >= 0) & (diff < WINDOW)

    s_cmp = [_dot_nt(kcmp_ref[0, g], qs[g]) for g in range(G)]
    s_win = [_dot_nt(kw_ref[0, g, pl.ds(w0, n_wc * KV_CHUNK), :], qs[g]) for g in range(G)]
    any_cmp = t_q >= CMP_BLOCK - 1
    p_cmp, o_cmp, o_win = [], [], []
    for g in range(G):
        es = masked_exp(heads(s_cmp[g]), cmask)
        p_cmp.append([e * jnp.where(any_cmp, 1.0 / jnp.sum(e, axis=0, keepdims=True), 0.0) for e in es])
    for g in range(G):
        o_cmp.append(_dot(vcmp_ref[:, g * n_cp:(g + 1) * n_cp],
                          jnp.concatenate(p_cmp[g], axis=1).astype(BF16)))
    for g in range(G):
        e_w = jnp.concatenate(masked_exp(heads(s_win[g]), wmask), axis=1).astype(BF16)
        acc = _dot(vw_ref[0, g, c0], e_w[0:KV_CHUNK])
        for j in range(1, n_wc):
            acc = acc + _dot(vw_ref[0, g, c0 + j], e_w[j * KV_CHUNK:(j + 1) * KV_CHUNK])
        o_win.append(acc)

    sels = []
    for g in range(G):
        p_c = p_cmp[g]
        p_sum = p_c[0]
        for r in range(1, R):
            p_sum = p_sum + p_c[r]
        p_hi = p_sum.astype(BF16)
        p_lo = (p_sum - p_hi.astype(F32)).astype(BF16)
        p_slc = _dot(overlap, p_hi) + _dot(overlap, p_lo)
        score = jnp.where(forced, FORCE_SCORE, jnp.where(valid, p_slc, -1.0))
        rank = jnp.zeros((n_blk, TQ), jnp.int32)
        for i in range(n_blk):
            row = score[i:i + 1, :]
            before = (row > score) | ((row == score) & (jb > i))
            rank = rank + before.astype(jnp.int32)
        bias = jnp.where(rank < min(SEL_TOPK, n_blk), 0.0, -SEL_BIAS)
        bias = jnp.concatenate([jnp.zeros((HEAD_DIM, TQ), F32), bias,
                                jnp.zeros((LANES - HEAD_DIM - n_blk, TQ), F32)], axis=0).T.astype(BF16)
        sels.append(jnp.concatenate([q_pad[g][r * TQ:(r + 1) * TQ] + bias for r in range(R)], axis=0))

    m_ref[...] = jnp.full(m_ref.shape, NEG_INF, F32)
    acc_ref[...] = jnp.zeros(acc_ref.shape, F32)

    def scores(g, kt):
        return _dot_nt(ks_ref[0, g, pl.ds(pl.multiple_of(kt * TK, TK), TK), :], sels[g])

    def accumulate(g, kt, s, causal):
        if causal is not None:
            s = jnp.concatenate([jnp.where(causal, sh, NEG_INF) for sh in heads(s)], axis=1)
        m_old = m_ref[g]
        m_new = jnp.maximum(m_old, jnp.max(s, axis=0, keepdims=True))
        alpha = jnp.exp2(m_old - m_new)
        p = jnp.exp2(s - m_new)
        acc_ref[g] = alpha * acc_ref[g] + _dot(vs_ref[0, g, kt], p.astype(BF16))
        m_ref[g] = m_new

    assert G == 2
    s0_ref[...] = scores(0, 0)

    def interior_tile(kt, carry):
        s1 = scores(1, kt)
        accumulate(0, kt, s0_ref[...], None)
        s0_ref[...] = scores(0, kt + 1)
        accumulate(1, kt, s1, None)
        return carry

    lax.fori_loop(0, qi, interior_tile, 0)
    causal = (q0 + lax.broadcasted_iota(jnp.int32, (TK, 1), 0)) <= t_q
    s1 = scores(1, qi)
    accumulate(0, qi, s0_ref[...], causal)
    accumulate(1, qi, s1, causal)

    gates = _sigmoid(gate_ref[0].T)
    outs = []
    for g in range(G):
        o_sel = acc_ref[g]
        for r, (oc, os_, ow) in enumerate(zip(heads(o_cmp[g]), heads(o_sel), heads(o_win[g]))):
            g0 = g * HEAD_DIM + r
            g_sel = gates[g0 + R:g0 + R + 1] * (1.0 / os_[HEAD_DIM:HEAD_DIM + 1])
            g_win = gates[g0 + 2 * R:g0 + 2 * R + 1] * (1.0 / ow[HEAD_DIM:HEAD_DIM + 1])
            outs.append(gates[g0:g0 + 1] * oc + g_sel * os_[0:HEAD_DIM] + g_win * ow[0:HEAD_DIM])
    o_ref[0] = jnp.concatenate(outs, axis=0).T.astype(BF16)


def _nsa(q, kcmp, vcmp, ks, vs, kw, vw, gates):
    B, G, R, S, _ = q.shape
    dh = HEAD_DIM
    n_cp = kcmp.shape[2]
    assert TQ == TK, "the key sweep treats exactly one tile per query tile as the diagonal"
    k_spec = lambda a: pl.BlockSpec((1,) + a.shape[1:], lambda b, i: (b, 0, 0, 0))
    vt_spec = lambda a: pl.BlockSpec((1,) + a.shape[1:], lambda b, i: (b, 0, 0, 0, 0))
    return pl.pallas_call(
        _nsa_kernel,
        grid=(B, S // TQ),
        in_specs=[
            pl.BlockSpec((1, G, R, TQ, LANES), lambda b, i: (b, 0, 0, i, 0)),
            pl.BlockSpec((1, G, n_cp, dh), lambda b, i: (b, 0, 0, 0)),
            pl.BlockSpec((dh, G * n_cp), lambda b, i: (0, b)),
            k_spec(ks), vt_spec(vs), k_spec(kw), vt_spec(vw),
            pl.BlockSpec((1, TQ, LANES), lambda b, i: (b, i, 0)),
        ],
        out_specs=pl.BlockSpec((1, TQ, G * R * dh), lambda b, i: (b, i, 0)),
        out_shape=jax.ShapeDtypeStruct((B, S, G * R * dh), BF16),
        scratch_shapes=[pltpu.VMEM((G, 1, R * TQ), F32), pltpu.VMEM((G, dh + ONES_ROWS, R * TQ), F32),
                        pltpu.VMEM((TK, R * TQ), F32)],
        compiler_params=pltpu.CompilerParams(
            dimension_semantics=("parallel", "arbitrary"), vmem_limit_bytes=VMEM_LIMIT_BYTES),
        name="nsa_attention",
    )(q, kcmp, vcmp, ks, vs, kw, vw, gates)


def _retention_kernel(q_ref, kt_ref, v_ref, g_ref, w_ref, o_ref, state_ref):
    C = RET_CHUNK
    NB = q_ref.shape[0]

    @pl.when(pl.program_id(1) == 0)
    def _():
        state_ref[...] = jnp.zeros(state_ref.shape, F32)

    i_col = lax.broadcasted_iota(jnp.int32, (C, 1), 0)
    i_row = lax.broadcasted_iota(jnp.int32, (1, C), 1)
    d_int = i_col - i_row
    log_gamma = [math.log(1.0 - 2.0 ** (-5.0 - h)) for h in range(RET_HEADS)]
    dmat = [jnp.where(d_int >= 0, jnp.exp(lg * jnp.maximum(d_int, 0).astype(F32)), 0.0) for lg in log_gamma]
    xi = [jnp.exp(lg * (i_col + 1).astype(F32)) for lg in log_gamma]
    zeta = [jnp.exp(lg * (C - 1 - i_row).astype(F32)) for lg in log_gamma]
    gamma_c = [math.exp(lg * C) for lg in log_gamma]

    units = [(n, h) for n in range(NB) for h in range(RET_HEADS)]
    q, kt, v, st = {}, {}, {}, {}
    for n, h in units:
        q[n, h] = q_ref[n, :, h * RET_QK_DIM:(h + 1) * RET_QK_DIM]
        kt[n, h] = kt_ref[n, 0, h * RET_QK_DIM:(h + 1) * RET_QK_DIM, :]
        v[n, h] = v_ref[n, :, h * RET_V_DIM:(h + 1) * RET_V_DIM]
        st[n, h] = state_ref[n, h]
    inner = {u: _dot(q[u], kt[u]) for u in units}
    cross = {u: _dot(q[u], st[u].astype(BF16)) for u in units}
    kv = {u: _dot((kt[u].astype(F32) * zeta[u[1]]).astype(BF16), v[u]) for u in units}
    for n, h in units:
        u = (n, h)
        vsl = slice(h * RET_V_DIM, (h + 1) * RET_V_DIM)
        y = _dot((inner[u] * dmat[h]).astype(BF16), v[u]) + cross[u] * xi[h]
        state_ref[n, h] = gamma_c[h] * st[u] + kv[u]

        mu = jnp.mean(y, axis=-1, keepdims=True)
        yc = y - mu
        var = jnp.mean(yc * yc, axis=-1, keepdims=True)
        yn = yc * lax.rsqrt(var + EPS) * w_ref[:, vsl]
        gate = g_ref[n, :, vsl].astype(F32)
        o_ref[n, :, vsl] = (gate * _sigmoid(gate) * yn).astype(BF16)


def _retention(rq, rkt, rv, rg, w):
    B, S, _ = rq.shape
    C = RET_CHUNK
    nb = math.gcd(RET_BATCH, B)
    q_spec = pl.BlockSpec((nb, C, RET_QK_WIDTH), lambda b, c: (b, c, 0))
    kt_spec = pl.BlockSpec((nb, 1, RET_QK_WIDTH, C), lambda b, c: (b, c, 0, 0))
    v_spec = pl.BlockSpec((nb, C, RET_V_WIDTH), lambda b, c: (b, c, 0))
    return pl.pallas_call(
        _retention_kernel,
        grid=(B // nb, S // C),
        in_specs=[q_spec, kt_spec, v_spec, v_spec, pl.BlockSpec((1, RET_V_WIDTH), lambda b, c: (0, 0))],
        out_specs=v_spec,
        out_shape=jax.ShapeDtypeStruct((B, S, RET_V_WIDTH), BF16),
        scratch_shapes=[pltpu.VMEM((nb, RET_HEADS, RET_QK_DIM, RET_V_DIM), F32)],
        compiler_params=pltpu.CompilerParams(
            dimension_semantics=("parallel", "arbitrary"), vmem_limit_bytes=VMEM_LIMIT_BYTES),
        name="retention",
    )(rq, rkt, rv, rg, w)


def _out_ffn_kernel(x_ref, a_ref, r_ref, wo_ref, ln_ref, wu_ref, wd_ref, o_ref):
    na = a_ref.shape[1]
    mix = _dot(a_ref[...], wo_ref[0:na, :]) + _dot(r_ref[...], wo_ref[na:, :])
    h = x_ref[...] + mix
    ms = jnp.mean(h * h, axis=-1, keepdims=True)
    hn = (h * lax.rsqrt(ms + EPS) * ln_ref[...]).astype(BF16)
    d_ff = wu_ref.shape[1]
    acc = None
    for f in range(d_ff // FF_CHUNK):
        cols = slice(f * FF_CHUNK, (f + 1) * FF_CHUNK)
        u = jnp.maximum(_dot(hn, wu_ref[:, cols]), 0.0)
        d = _dot((u * u).astype(BF16), wd_ref[cols, :])
        acc = d if acc is None else acc + d
    o_ref[...] = h + acc


def _out_ffn(x2, o_nsa, o_ret, w_out, ln2, w_up, w_down):
    N, D = x2.shape
    tm = TM_FFN
    row = lambda i: (i, 0)
    const = lambda i: (0, 0)
    resident = functools.partial(pl.BlockSpec, index_map=const, pipeline_mode=pl.Buffered(1))
    return pl.pallas_call(
        _out_ffn_kernel,
        grid=(N // tm,),
        in_specs=[
            pl.BlockSpec((tm, D), row),
            pl.BlockSpec((tm, o_nsa.shape[1]), row),
            pl.BlockSpec((tm, o_ret.shape[1]), row),
            resident(w_out.shape),
            pl.BlockSpec((1, D), const),
            resident(w_up.shape),
            resident(w_down.shape),
        ],
        out_specs=pl.BlockSpec((tm, D), row),
        out_shape=jax.ShapeDtypeStruct((N, D), F32),
        compiler_params=pltpu.CompilerParams(
            dimension_semantics=("parallel",), vmem_limit_bytes=VMEM_LIMIT_BYTES),
        name="out_ffn",
    )(x2, o_nsa, o_ret, w_out, ln2, w_up, w_down)


def _rope_tables(S):
    half = HEAD_DIM // 2
    inv = ROPE_THETA ** (-jnp.arange(half, dtype=F32) / half)
    ang = jnp.arange(S).astype(F32)[:, None] * inv[None, :]
    cos, sin = jnp.cos(ang), jnp.sin(ang)
    reps = LANES // HEAD_DIM
    cos_t = jnp.tile(jnp.concatenate([cos, cos], axis=-1), (1, reps))
    sin_t = jnp.tile(jnp.concatenate([-sin, sin], axis=-1), (1, reps))
    return cos_t, sin_t


def _layer(h, ln1_w, w_in, q_norm_w, k_norm_w, cmp_pe_k, cmp_pe_v, cmp_wk1, cmp_wk2,
           cmp_wv1, cmp_wv2, ret_norm_w, w_out, ln2_w, w_up, w_down):
    B, S, D = h.shape
    N = B * S
    G = NSA_KV_HEADS
    x2 = h.reshape(N, D)

    w_perm = _permute_w_in(w_in).astype(BF16)
    cos_t, sin_t = _rope_tables(S)
    reps = LANES // HEAD_DIM
    qn = jnp.tile(q_norm_w, reps)[None, :]
    kn = jnp.tile(k_norm_w, (1, reps))

    (q, kc, vc, ks, vs, kw, vw, gates, rq, rk, rv, rg) = _in_proj(
        x2, ln1_w[None, :], w_perm, cos_t, sin_t, qn, kn, B, S)

    kcmp, vcmp = _compress(
        kc, vc, jnp.tile(cmp_pe_k, (1, G)), jnp.tile(cmp_pe_v, (1, G)),
        _block_diag_w1(cmp_wk1).astype(BF16), cmp_wk2.astype(BF16), _block_diag_w1(cmp_wv1).astype(BF16),
        jnp.pad(cmp_wv2, ((0, 0), (0, LANES - HEAD_DIM))).astype(BF16), B, S)

    o_nsa = _nsa(q, kcmp, vcmp, ks, vs, kw, vw, gates.reshape(B, S, LANES))
    o_ret = _retention(rq.reshape(B, S, -1), rk, rv.reshape(B, S, -1),
                       rg.reshape(B, S, -1), ret_norm_w.reshape(1, RET_V_WIDTH))

    out = _out_ffn(x2, o_nsa.reshape(N, -1), o_ret.reshape(N, -1), w_out.astype(BF16),
                   ln2_w[None, :], w_up.astype(BF16), w_down.astype(BF16))
    return out.reshape(B, S, D)


def kernel(x, ln1_w, w_in, q_norm_w, k_norm_w, cmp_pe_k, cmp_pe_v, cmp_wk1, cmp_wk2, cmp_wv1, cmp_wv2,
           ret_norm_w, w_out, ln2_w, w_up, w_down):
    h = x
    for l in range(ln1_w.shape[0]):
        h = _layer(h, ln1_w[l], w_in[l], q_norm_w[l], k_norm_w[l], cmp_pe_k[l], cmp_pe_v[l],
                   cmp_wk1[l], cmp_wk2[l], cmp_wv1[l], cmp_wv2[l], ret_norm_w[l], w_out[l],
                   ln2_w[l], w_up[l], w_down[l])
    return h
```

```python
import functools
import math

import jax
import jax.numpy as jnp
import numpy as np
from jax import lax
from jax.experimental import pallas as pl
from jax.experimental.pallas import tpu as pltpu

F32 = jnp.float32
BF16 = jnp.bfloat16

NSA_HEADS = 8
NSA_KV_HEADS = 2
NSA_GROUP = NSA_HEADS // NSA_KV_HEADS
HEAD_DIM = 64
CMP_BLOCK = 32
CMP_STRIDE = 16
CMP_HIDDEN = 256
SEL_BLOCK = 64
SEL_TOPK = 8
WINDOW = 256
RET_HEADS = 4
RET_QK_DIM = 64
RET_V_DIM = 128
RET_CHUNK = 128
ROPE_THETA = 10000.0
EPS = 1e-6
NEG_INF = -1.0e30
SEL_BIAS = 2.0 ** 100

NSA_Q_DIM = NSA_HEADS * HEAD_DIM
NSA_KV_DIM = NSA_KV_HEADS * HEAD_DIM
NSA_GATE_DIM = 3 * NSA_HEADS
RET_QK_WIDTH = RET_HEADS * RET_QK_DIM
RET_V_WIDTH = RET_HEADS * RET_V_DIM

LANES = 128
VMEM_LIMIT_BYTES = 56 * 1024 * 1024

TM_PROJ = 512
TQ = 256
TK = 256
KV_CHUNK = 128
ONES_ROWS = 16
TM_FFN = 512
FF_CHUNK = 512
RET_BATCH = 8


def _dot(a, b):
    return jnp.dot(a, b, preferred_element_type=F32)


def _dot_nt(a, b):
    return lax.dot_general(a, b, (((1,), (1,)), ((), ())), preferred_element_type=F32)


def _sigmoid(x):
    return 1.0 / (1.0 + jnp.exp(-x))


_Q0 = 0
_K0 = _Q0 + NSA_Q_DIM
_V0 = _K0 + 3 * NSA_KV_DIM
_G0 = _V0 + 3 * NSA_KV_DIM
_RQ0 = _G0 + LANES
_RK0 = _RQ0 + RET_QK_WIDTH
_RV0 = _RK0 + RET_QK_WIDTH
_RG0 = _RV0 + RET_V_WIDTH
_W_IN_COLS = _RG0 + RET_V_WIDTH


def _permute_w_in(w_in):
    sizes = [NSA_Q_DIM] + [NSA_KV_DIM] * 6 + [NSA_GATE_DIM, RET_QK_WIDTH, RET_QK_WIDTH, RET_V_WIDTH, RET_V_WIDTH]
    off = [int(o) for o in np.concatenate([[0], np.cumsum(sizes)])]
    (q, kc, vc, ks, vs, kw, vw, gate, rq, rk, rv, rg) = [w_in[:, off[i]:off[i + 1]] for i in range(12)]
    d = w_in.shape[0]
    gate = gate.reshape(d, 3, NSA_KV_HEADS, NSA_GROUP).transpose(0, 2, 1, 3).reshape(d, NSA_KV_HEADS, 3 * NSA_GROUP)
    gate = jnp.pad(gate, ((0, 0), (0, 0), (0, HEAD_DIM - 3 * NSA_GROUP))).reshape(d, LANES)
    w = jnp.concatenate([q, kc, ks, kw, vc, vs, vw, gate, rq, rk, rv, rg], axis=1)
    assert w.shape[1] == _W_IN_COLS
    return w


def _in_proj_kernel(x_ref, ln_ref, w_ref, cos_ref, sin_ref, qn_ref, kn_ref,
                    q_ref, kc_ref, vc_ref, ks_ref, vs_ref, kw_ref, vw_ref, gate_ref,
                    rq_ref, rk_ref, rv_ref, rg_ref, *, tiles_per_seq):
    x = x_ref[...]
    ms = jnp.mean(x * x, axis=-1, keepdims=True)
    xn = (x * lax.rsqrt(ms + EPS) * ln_ref[...]).astype(BF16)
    cos = cos_ref[...]
    sin = sin_ref[...]
    lane = lax.broadcasted_iota(jnp.int32, (1, LANES), 1)
    low_half = (lane & (HEAD_DIM // 2)) == 0
    first_head = lane < HEAD_DIM

    def proj(a, b):
        return _dot(xn, w_ref[:, a:b])

    def rope(t):
        swapped = jnp.where(low_half, pltpu.roll(t, LANES - HEAD_DIM // 2, 1), pltpu.roll(t, HEAD_DIM // 2, 1))
        return t * cos + swapped * sin

    def head_norm(t, w):
        t2 = t * t
        s0 = jnp.sum(jnp.where(first_head, t2, 0.0), axis=-1, keepdims=True)
        s1 = jnp.sum(jnp.where(first_head, 0.0, t2), axis=-1, keepdims=True)
        msq = jnp.where(first_head, s0, s1) * (1.0 / HEAD_DIM)
        return t * lax.rsqrt(msq + EPS) * w

    def tiles(sec):
        return [sec[:, c * LANES:(c + 1) * LANES] for c in range(sec.shape[1] // LANES)]

    def split_heads(t, fill):
        return [jnp.where(first_head, t, fill), jnp.where(first_head, pltpu.roll(t, HEAD_DIM, 1), fill)]

    tm = x.shape[0]

    qn = qn_ref[...]
    for c, t in enumerate(tiles(proj(_Q0, _K0))):
        t = rope(head_norm(t, qn)) * (HEAD_DIM ** -0.5 * math.log2(math.e))
        for j, tj in enumerate(split_heads(t, 0.0)):
            h = 2 * c + j
            q_ref[0, h // NSA_GROUP, h % NSA_GROUP] = tj.astype(BF16)

    ksec = tiles(proj(_K0, _V0))
    kc = rope(head_norm(ksec[0], kn_ref[0:1, :]))
    ks = rope(head_norm(ksec[1], kn_ref[1:2, :]))
    kw = rope(head_norm(ksec[2], kn_ref[2:3, :])).astype(BF16)
    s0 = (pl.program_id(0) % tiles_per_seq) * tm
    blk = jnp.right_shift(s0 + lax.broadcasted_iota(jnp.int32, (tm, 1), 0), int(math.log2(SEL_BLOCK)))
    onehot = (lane - HEAD_DIM == blk).astype(F32)
    ks_aug = split_heads(ks, onehot)
    kc_ref[...] = kc
    for g in range(NSA_KV_HEADS):
        ks_ref[0, g] = ks_aug[g].astype(BF16)
        kw_ref[0, g] = kw[:, g * HEAD_DIM:(g + 1) * HEAD_DIM]

    def store_transposed(v, out_ref):
        chunk = out_ref.shape[-1]
        ones = jnp.ones((ONES_ROWS, chunk), BF16)
        for c in range(v.shape[0] // chunk):
            vt = v[c * chunk:(c + 1) * chunk, :].T.astype(BF16)
            for g in range(NSA_KV_HEADS):
                out_ref[0, g, c, 0:HEAD_DIM, :] = vt[g * HEAD_DIM:(g + 1) * HEAD_DIM, :]
                out_ref[0, g, c, HEAD_DIM:HEAD_DIM + ONES_ROWS, :] = ones

    vsec = tiles(proj(_V0, _RQ0))
    vc_ref[...] = vsec[0]
    store_transposed(vsec[1], vs_ref)
    store_transposed(vsec[2], vw_ref)
    gate_ref[...] = vsec[3]

    for c, t in enumerate(tiles(proj(_RQ0, _RK0))):
        rq_ref[:, c * LANES:(c + 1) * LANES] = rope(t).astype(BF16)
    for c, t in enumerate(tiles(proj(_RK0, _RV0))):
        t = rope(t) * (RET_QK_DIM ** -0.5)
        for j in range(tm // RET_CHUNK):
            rk_ref[0, j, c * LANES:(c + 1) * LANES, :] = t[j * RET_CHUNK:(j + 1) * RET_CHUNK, :].T.astype(BF16)
    rv_ref[...] = proj(_RV0, _RG0).astype(BF16)
    rg_ref[...] = proj(_RG0, _W_IN_COLS).astype(BF16)


def _in_proj(x2, ln1, w_perm, cos, sin, qn, kn, B, S):
    N, D = x2.shape
    tm = TM_PROJ
    spt = S // tm
    G, R, dh = NSA_KV_HEADS, NSA_GROUP, HEAD_DIM

    def row(i):
        return (i, 0)

    def hm(i):
        return (i // spt, 0, i % spt, 0)

    hm_spec = pl.BlockSpec((1, G, tm, dh), hm)
    out_shape = [
        jax.ShapeDtypeStruct((B, G, R, S, LANES), BF16),
        jax.ShapeDtypeStruct((N, LANES), F32),
        jax.ShapeDtypeStruct((N, LANES), F32),
        jax.ShapeDtypeStruct((B, G, S, LANES), BF16),
        jax.ShapeDtypeStruct((B, G, S // TK, dh + ONES_ROWS, TK), BF16),
        jax.ShapeDtypeStruct((B, G, S, dh), BF16),
        jax.ShapeDtypeStruct((B, G, S // KV_CHUNK, dh + ONES_ROWS, KV_CHUNK), BF16),
        jax.ShapeDtypeStruct((N, LANES), F32),
        jax.ShapeDtypeStruct((N, RET_QK_WIDTH), BF16),
        jax.ShapeDtypeStruct((B, S // RET_CHUNK, RET_QK_WIDTH, RET_CHUNK), BF16),
        jax.ShapeDtypeStruct((N, RET_V_WIDTH), BF16),
        jax.ShapeDtypeStruct((N, RET_V_WIDTH), BF16),
    ]
    out_specs = [
        pl.BlockSpec((1, G, R, tm, LANES), lambda i: (i // spt, 0, 0, i % spt, 0)),
        pl.BlockSpec((tm, LANES), row), pl.BlockSpec((tm, LANES), row), pl.BlockSpec((1, G, tm, LANES), hm),
        pl.BlockSpec((1, G, tm // TK, dh + ONES_ROWS, TK), lambda i: (i // spt, 0, i % spt, 0, 0)),
        hm_spec,
        pl.BlockSpec((1, G, tm // KV_CHUNK, dh + ONES_ROWS, KV_CHUNK), lambda i: (i // spt, 0, i % spt, 0, 0)),
        pl.BlockSpec((tm, LANES), row),
        pl.BlockSpec((tm, RET_QK_WIDTH), row),
        pl.BlockSpec((1, tm // RET_CHUNK, RET_QK_WIDTH, RET_CHUNK), lambda i: (i // spt, i % spt, 0, 0)),
        pl.BlockSpec((tm, RET_V_WIDTH), row),
        pl.BlockSpec((tm, RET_V_WIDTH), row),
    ]
    const = lambda i: (0, 0)
    in_specs = [
        pl.BlockSpec((tm, D), row),
        pl.BlockSpec((1, D), const),
        pl.BlockSpec((D, _W_IN_COLS), const, pipeline_mode=pl.Buffered(1)),
        pl.BlockSpec((tm, LANES), lambda i: (i % spt, 0)),
        pl.BlockSpec((tm, LANES), lambda i: (i % spt, 0)),
        pl.BlockSpec((1, LANES), const),
        pl.BlockSpec((3, LANES), const),
    ]
    return pl.pallas_call(
        functools.partial(_in_proj_kernel, tiles_per_seq=spt),
        grid=(N // tm,),
        in_specs=in_specs,
        out_specs=out_specs,
        out_shape=out_shape,
        compiler_params=pltpu.CompilerParams(
            dimension_semantics=("parallel",), vmem_limit_bytes=VMEM_LIMIT_BYTES),
        name="in_proj",
    )(x2, ln1, w_perm, cos, sin, qn, kn)


def _compress_kernel(k_ref, v_ref, pek_ref, pev_ref, wk1_ref, wk2_ref, wv1_ref, wv2_ref, ko_ref, vo_ref):
    n_seg = k_ref.shape[0] // CMP_STRIDE
    G = NSA_KV_HEADS

    def run(x_ref, pe_ref, w1_ref):
        first = second = None
        for l in range(CMP_STRIDE):
            x = x_ref[pl.ds(l, n_seg, stride=CMP_STRIDE), :]
            a = _dot((x + pe_ref[l:l + 1, :]).astype(BF16), w1_ref[l])
            b = _dot((x + pe_ref[CMP_STRIDE + l:CMP_STRIDE + l + 1, :]).astype(BF16), w1_ref[CMP_STRIDE + l])
            first = a if first is None else first + a
            second = b if second is None else second + b
        h = first + pltpu.roll(second, n_seg - 1, 0)
        return (h * _sigmoid(h)).astype(BF16)

    hk = run(k_ref, pek_ref, wk1_ref)
    hv = run(v_ref, pev_ref, wv1_ref)
    for g in range(G):
        cols = slice(g * CMP_HIDDEN, (g + 1) * CMP_HIDDEN)
        ko_ref[0, g] = _dot(hk[:, cols], wk2_ref[...]).astype(BF16)
        vo_ref[:, g * n_seg:(g + 1) * n_seg] = _dot(hv[:, cols], wv2_ref[...]).T[0:HEAD_DIM, :].astype(BF16)


def _compress(kc, vc, pek, pev, wk1, wk2, wv1, wv2, B, S):
    G = NSA_KV_HEADS
    n_seg = S // CMP_STRIDE
    width = kc.shape[1]
    const2 = lambda b: (0, 0)
    const3 = lambda b: (0, 0, 0)
    tok_spec = pl.BlockSpec((S, width), lambda b: (b, 0))
    return pl.pallas_call(
        _compress_kernel,
        grid=(B,),
        in_specs=[tok_spec, tok_spec,
                  pl.BlockSpec(pek.shape, const2), pl.BlockSpec(pev.shape, const2),
                  pl.BlockSpec(wk1.shape, const3), pl.BlockSpec(wk2.shape, const2),
                  pl.BlockSpec(wv1.shape, const3), pl.BlockSpec(wv2.shape, const2)],
        out_specs=[pl.BlockSpec((1, G, n_seg, HEAD_DIM), lambda b: (b, 0, 0, 0)),
                   pl.BlockSpec((HEAD_DIM, G * n_seg), lambda b: (0, b))],
        out_shape=[jax.ShapeDtypeStruct((B, G, n_seg, HEAD_DIM), BF16),
                   jax.ShapeDtypeStruct((HEAD_DIM, B * G * n_seg), BF16)],
        compiler_params=pltpu.CompilerParams(
            dimension_semantics=("parallel",), vmem_limit_bytes=VMEM_LIMIT_BYTES),
        name="compress",
    )(kc, vc, pek, pev, wk1, wk2, wv1, wv2)


def _block_diag_w1(w1):
    G = NSA_KV_HEADS
    w = w1.reshape(CMP_BLOCK, HEAD_DIM, CMP_HIDDEN)
    z = jnp.zeros_like(w)
    rows = [jnp.concatenate([w if j == g else z for j in range(G)], axis=2) for g in range(G)]
    return jnp.concatenate(rows, axis=1)


def _nsa_kernel(q_ref, kcmp_ref, vcmp_ref, ks_ref, vs_ref, kw_ref, vw_ref, gate_ref,
                o_ref, m_ref, acc_ref, s0_ref):
    G, R = NSA_KV_HEADS, NSA_GROUP
    n_cp = kcmp_ref.shape[2]
    S = ks_ref.shape[2]
    n_cmp = (S - CMP_BLOCK) // CMP_STRIDE + 1
    n_blk = S // SEL_BLOCK
    qi = pl.program_id(1)
    q0 = qi * TQ
    t_q = q0 + lax.broadcasted_iota(jnp.int32, (1, TQ), 1)
    q_pad = [q_ref[0, g].reshape(R * TQ, LANES) for g in range(G)]
    qs = [q[:, 0:HEAD_DIM] for q in q_pad]

    def heads(x):
        return [x[:, r * TQ:(r + 1) * TQ] for r in range(R)]

    def masked_exp(s_heads, mask):
        es = []
        for s in s_heads:
            s = jnp.where(mask, s, NEG_INF)
            es.append(jnp.exp2(s - jnp.max(s, axis=0, keepdims=True)))
        return es

    c_idx = lax.broadcasted_iota(jnp.int32, (n_cp, TQ), 0)
    cmask = ((c_idx * CMP_STRIDE + (CMP_BLOCK - 1)) <= t_q) & (c_idx < n_cmp)
    jj = lax.broadcasted_iota(jnp.int32, (n_blk, n_cp), 0)
    cc = lax.broadcasted_iota(jnp.int32, (n_blk, n_cp), 1)
    overlap = ((cc * CMP_STRIDE < (jj + 1) * SEL_BLOCK) & (cc * CMP_STRIDE + CMP_BLOCK > jj * SEL_BLOCK)
               & (cc < n_cmp)).astype(BF16)
    jb = lax.broadcasted_iota(jnp.int32, (n_blk, TQ), 0)
    jb_f = jb.astype(F32)
    cur = jnp.right_shift(t_q, int(math.log2(SEL_BLOCK)))
    forced = (jb == 0) | (jb == cur) | (jb == cur - 1)
    valid = jb <= cur

    n_sub = TQ // KV_CHUNK
    n_wc = (KV_CHUNK + WINDOW) // KV_CHUNK
    win_c0, win_mask, win_q = [], [], []
    for h in range(n_sub):
        c0 = jnp.maximum(q0 // KV_CHUNK + h - WINDOW // KV_CHUNK, 0)
        kpos = c0 * KV_CHUNK + lax.broadcasted_iota(jnp.int32, (n_wc * KV_CHUNK, KV_CHUNK), 0)
        diff = t_q[:, h * KV_CHUNK:(h + 1) * KV_CHUNK] - kpos
        win_c0.append(c0)
        win_mask.append((diff >= 0) & (diff < WINDOW))
        win_q.append([jnp.concatenate([qs[g][r * TQ + h * KV_CHUNK:r * TQ + (h + 1) * KV_CHUNK]
                                       for r in range(R)], axis=0) for g in range(G)])

    def window_keys(g, h):
        return kw_ref[0, g, pl.ds(pl.multiple_of(win_c0[h] * KV_CHUNK, KV_CHUNK), n_wc * KV_CHUNK), :]

    s_cmp = [_dot_nt(kcmp_ref[0, g], qs[g]) for g in range(G)]
    s_win = [[_dot_nt(window_keys(g, h), win_q[h][g]) for h in range(n_sub)] for g in range(G)]
    any_cmp = t_q >= CMP_BLOCK - 1
    p_cmp, o_cmp, o_win = [], [], []
    for g in range(G):
        es = masked_exp(heads(s_cmp[g]), cmask)
        p_cmp.append([e * jnp.where(any_cmp, 1.0 / jnp.sum(e, axis=0, keepdims=True), 0.0) for e in es])
    for g in range(G):
        o_cmp.append(_dot(vcmp_ref[:, g * n_cp:(g + 1) * n_cp],
                          jnp.concatenate(p_cmp[g], axis=1).astype(BF16)))
    for g in range(G):
        sub = []
        for h in range(n_sub):
            s_heads = [s_win[g][h][:, r * KV_CHUNK:(r + 1) * KV_CHUNK] for r in range(R)]
            e_w = jnp.concatenate(masked_exp(s_heads, win_mask[h]), axis=1).astype(BF16)
            acc = _dot(vw_ref[0, g, win_c0[h]], e_w[0:KV_CHUNK])
            for j in range(1, n_wc):
                acc = acc + _dot(vw_ref[0, g, win_c0[h] + j], e_w[j * KV_CHUNK:(j + 1) * KV_CHUNK])
            sub.append(acc)
        o_win.append(jnp.concatenate([sub[h][:, r * KV_CHUNK:(r + 1) * KV_CHUNK]
                                      for r in range(R) for h in range(n_sub)], axis=1))

    sels = []
    for g in range(G):
        p_c = p_cmp[g]
        p_sum = p_c[0]
        for r in range(1, R):
            p_sum = p_sum + p_c[r]
        p_hi = p_sum.astype(BF16)
        p_lo = (p_sum - p_hi.astype(F32)).astype(BF16)
        p_slc = _dot(overlap, p_hi) + _dot(overlap, p_lo)
        work = jnp.where(valid & jnp.logical_not(forced), p_slc, -1.0)
        picked = forced
        for _ in range(min(SEL_TOPK, n_blk) - 3):
            best = jnp.max(work, axis=0, keepdims=True)
            first = jnp.min(jnp.where(work == best, jb_f, float(n_blk)), axis=0, keepdims=True)
            hit = jb_f == first
            picked = picked | hit
            work = jnp.where(hit, -2.0, work)
        bias = jnp.where(picked, 0.0, -SEL_BIAS)
        bias = jnp.concatenate([jnp.zeros((HEAD_DIM, TQ), F32), bias,
                                jnp.zeros((LANES - HEAD_DIM - n_blk, TQ), F32)], axis=0).T.astype(BF16)
        sels.append(jnp.concatenate([q_pad[g][r * TQ:(r + 1) * TQ] + bias for r in range(R)], axis=0))

    m_ref[...] = jnp.full(m_ref.shape, NEG_INF, F32)
    acc_ref[...] = jnp.zeros(acc_ref.shape, F32)

    def scores(g, kt):
        return _dot_nt(ks_ref[0, g, pl.ds(pl.multiple_of(kt * TK, TK), TK), :], sels[g])

    def accumulate(g, kt, s, causal):
        if causal is not None:
            s = jnp.concatenate([jnp.where(causal, sh, NEG_INF) for sh in heads(s)], axis=1)
        m_old = m_ref[g]
        m_new = jnp.maximum(m_old, jnp.max(s, axis=0, keepdims=True))
        alpha = jnp.exp2(m_old - m_new)
        p = jnp.exp2(s - m_new)
        acc_ref[g] = alpha * acc_ref[g] + _dot(vs_ref[0, g, kt], p.astype(BF16))
        m_ref[g] = m_new

    assert G == 2
    s0_ref[...] = scores(0, 0)

    def interior_tile(kt, carry):
        s1 = scores(1, kt)
        accumulate(0, kt, s0_ref[...], None)
        s0_ref[...] = scores(0, kt + 1)
        accumulate(1, kt, s1, None)
        return carry

    lax.fori_loop(0, qi, interior_tile, 0)
    causal = (q0 + lax.broadcasted_iota(jnp.int32, (TK, TQ), 0)) <= t_q
    s1 = scores(1, qi)
    accumulate(0, qi, s0_ref[...], causal)
    accumulate(1, qi, s1, causal)

    gates = _sigmoid(gate_ref[0].T)
    outs = []
    for g in range(G):
        o_sel = acc_ref[g]
        for r, (oc, os_, ow) in enumerate(zip(heads(o_cmp[g]), heads(o_sel), heads(o_win[g]))):
            g0 = g * HEAD_DIM + r
            g_sel = gates[g0 + R:g0 + R + 1] * (1.0 / os_[HEAD_DIM:HEAD_DIM + 1])
            g_win = gates[g0 + 2 * R:g0 + 2 * R + 1] * (1.0 / ow[HEAD_DIM:HEAD_DIM + 1])
            outs.append(gates[g0:g0 + 1] * oc + g_sel * os_[0:HEAD_DIM] + g_win * ow[0:HEAD_DIM])
    o_ref[0] = jnp.concatenate(outs, axis=0).T.astype(BF16)


def _nsa(q, kcmp, vcmp, ks, vs, kw, vw, gates):
    B, G, R, S, _ = q.shape
    dh = HEAD_DIM
    n_cp = kcmp.shape[2]
    assert TQ == TK, "the key sweep treats exactly one tile per query tile as the diagonal"
    k_spec = lambda a: pl.BlockSpec((1,) + a.shape[1:], lambda b, i: (b, 0, 0, 0))
    vt_spec = lambda a: pl.BlockSpec((1,) + a.shape[1:], lambda b, i: (b, 0, 0, 0, 0))
    return pl.pallas_call(
        _nsa_kernel,
        grid=(B, S // TQ),
        in_specs=[
            pl.BlockSpec((1, G, R, TQ, LANES), lambda b, i: (b, 0, 0, i, 0)),
            pl.BlockSpec((1, G, n_cp, dh), lambda b, i: (b, 0, 0, 0)),
            pl.BlockSpec((dh, G * n_cp), lambda b, i: (0, b)),
            k_spec(ks), vt_spec(vs), k_spec(kw), vt_spec(vw),
            pl.BlockSpec((1, TQ, LANES), lambda b, i: (b, i, 0)),
        ],
        out_specs=pl.BlockSpec((1, TQ, G * R * dh), lambda b, i: (b, i, 0)),
        out_shape=jax.ShapeDtypeStruct((B, S, G * R * dh), BF16),
        scratch_shapes=[pltpu.VMEM((G, 1, R * TQ), F32), pltpu.VMEM((G, dh + ONES_ROWS, R * TQ), F32),
                        pltpu.VMEM((TK, R * TQ), F32)],
        compiler_params=pltpu.CompilerParams(
            dimension_semantics=("parallel", "arbitrary"), vmem_limit_bytes=VMEM_LIMIT_BYTES),
        name="nsa_attention",
    )(q, kcmp, vcmp, ks, vs, kw, vw, gates)


def _retention_kernel(q_ref, kt_ref, v_ref, g_ref, w_ref, o_ref, state_ref):
    C = RET_CHUNK
    NB = q_ref.shape[0]

    @pl.when(pl.program_id(1) == 0)
    def _():
        state_ref[...] = jnp.zeros(state_ref.shape, F32)

    i_col = lax.broadcasted_iota(jnp.int32, (C, 1), 0)
    i_row = lax.broadcasted_iota(jnp.int32, (1, C), 1)
    d_int = i_col - i_row
    log_gamma = [math.log(1.0 - 2.0 ** (-5.0 - h)) for h in range(RET_HEADS)]
    dmat = [jnp.where(d_int >= 0, jnp.exp(lg * jnp.maximum(d_int, 0).astype(F32)), 0.0) for lg in log_gamma]
    xi = [jnp.exp(lg * (i_col + 1).astype(F32)) for lg in log_gamma]
    zeta = [jnp.exp(lg * (C - 1 - i_row).astype(F32)) for lg in log_gamma]
    gamma_c = [math.exp(lg * C) for lg in log_gamma]

    units = [(n, h) for n in range(NB) for h in range(RET_HEADS)]
    q, kt, v, st = {}, {}, {}, {}
    for n, h in units:
        q[n, h] = q_ref[n, :, h * RET_QK_DIM:(h + 1) * RET_QK_DIM]
        kt[n, h] = kt_ref[n, 0, h * RET_QK_DIM:(h + 1) * RET_QK_DIM, :]
        v[n, h] = v_ref[n, :, h * RET_V_DIM:(h + 1) * RET_V_DIM]
        st[n, h] = state_ref[n, h]
    inner = {u: _dot(q[u], kt[u]) for u in units}
    cross = {u: _dot(q[u], st[u].astype(BF16)) for u in units}
    kv = {u: _dot((kt[u].astype(F32) * zeta[u[1]]).astype(BF16), v[u]) for u in units}
    for n, h in units:
        u = (n, h)
        vsl = slice(h * RET_V_DIM, (h + 1) * RET_V_DIM)
        y = _dot((inner[u] * dmat[h]).astype(BF16), v[u]) + cross[u] * xi[h]
        state_ref[n, h] = gamma_c[h] * st[u] + kv[u]

        mu = jnp.mean(y, axis=-1, keepdims=True)
        yc = y - mu
        var = jnp.mean(yc * yc, axis=-1, keepdims=True)
        yn = yc * lax.rsqrt(var + EPS) * w_ref[:, vsl]
        gate = g_ref[n, :, vsl].astype(F32)
        o_ref[n, :, vsl] = (gate * _sigmoid(gate) * yn).astype(BF16)


def _retention(rq, rkt, rv, rg, w):
    B, S, _ = rq.shape
    C = RET_CHUNK
    nb = math.gcd(RET_BATCH, B)
    q_spec = pl.BlockSpec((nb, C, RET_QK_WIDTH), lambda b, c: (b, c, 0))
    kt_spec = pl.BlockSpec((nb, 1, RET_QK_WIDTH, C), lambda b, c: (b, c, 0, 0))
    v_spec = pl.BlockSpec((nb, C, RET_V_WIDTH), lambda b, c: (b, c, 0))
    return pl.pallas_call(
        _retention_kernel,
        grid=(B // nb, S // C),
        in_specs=[q_spec, kt_spec, v_spec, v_spec, pl.BlockSpec((1, RET_V_WIDTH), lambda b, c: (0, 0))],
        out_specs=v_spec,
        out_shape=jax.ShapeDtypeStruct((B, S, RET_V_WIDTH), BF16),
        scratch_shapes=[pltpu.VMEM((nb, RET_HEADS, RET_QK_DIM, RET_V_DIM), F32)],
        compiler_params=pltpu.CompilerParams(
            dimension_semantics=("parallel", "arbitrary"), vmem_limit_bytes=VMEM_LIMIT_BYTES),
        name="retention",
    )(rq, rkt, rv, rg, w)


def _out_ffn_kernel(x_ref, a_ref, r_ref, wo_ref, ln_ref, wu_ref, wd_ref, o_ref):
    na = a_ref.shape[1]
    mix = _dot(a_ref[...], wo_ref[0:na, :]) + _dot(r_ref[...], wo_ref[na:, :])
    h = x_ref[...] + mix
    ms = jnp.mean(h * h, axis=-1, keepdims=True)
    hn = (h * lax.rsqrt(ms + EPS) * ln_ref[...]).astype(BF16)
    d_ff = wu_ref.shape[1]
    acc = None
    for f in range(d_ff // FF_CHUNK):
        cols = slice(f * FF_CHUNK, (f + 1) * FF_CHUNK)
        u = jnp.maximum(_dot(hn, wu_ref[:, cols]), 0.0)
        d = _dot((u * u).astype(BF16), wd_ref[cols, :])
        acc = d if acc is None else acc + d
    o_ref[...] = h + acc


def _out_ffn(x2, o_nsa, o_ret, w_out, ln2, w_up, w_down):
    N, D = x2.shape
    tm = TM_FFN
    row = lambda i: (i, 0)
    const = lambda i: (0, 0)
    resident = functools.partial(pl.BlockSpec, index_map=const, pipeline_mode=pl.Buffered(1))
    return pl.pallas_call(
        _out_ffn_kernel,
        grid=(N // tm,),
        in_specs=[
            pl.BlockSpec((tm, D), row),
            pl.BlockSpec((tm, o_nsa.shape[1]), row),
            pl.BlockSpec((tm, o_ret.shape[1]), row),
            resident(w_out.shape),
            pl.BlockSpec((1, D), const),
            resident(w_up.shape),
            resident(w_down.shape),
        ],
        out_specs=pl.BlockSpec((tm, D), row),
        out_shape=jax.ShapeDtypeStruct((N, D), F32),
        compiler_params=pltpu.CompilerParams(
            dimension_semantics=("parallel",), vmem_limit_bytes=VMEM_LIMIT_BYTES),
        name="out_ffn",
    )(x2, o_nsa, o_ret, w_out, ln2, w_up, w_down)


def _rope_tables(S):
    half = HEAD_DIM // 2
    inv = ROPE_THETA ** (-jnp.arange(half, dtype=F32) / half)
    ang = jnp.arange(S).astype(F32)[:, None] * inv[None, :]
    cos, sin = jnp.cos(ang), jnp.sin(ang)
    reps = LANES // HEAD_DIM
    cos_t = jnp.tile(jnp.concatenate([cos, cos], axis=-1), (1, reps))
    sin_t = jnp.tile(jnp.concatenate([-sin, sin], axis=-1), (1, reps))
    return cos_t, sin_t


def _layer(h, ln1_w, w_in, q_norm_w, k_norm_w, cmp_pe_k, cmp_pe_v, cmp_wk1, cmp_wk2,
           cmp_wv1, cmp_wv2, ret_norm_w, w_out, ln2_w, w_up, w_down):
    B, S, D = h.shape
    N = B * S
    G = NSA_KV_HEADS
    x2 = h.reshape(N, D)

    w_perm = _permute_w_in(w_in).astype(BF16)
    cos_t, sin_t = _rope_tables(S)
    reps = LANES // HEAD_DIM
    qn = jnp.tile(q_norm_w, reps)[None, :]
    kn = jnp.tile(k_norm_w, (1, reps))

    (q, kc, vc, ks, vs, kw, vw, gates, rq, rk, rv, rg) = _in_proj(
        x2, ln1_w[None, :], w_perm, cos_t, sin_t, qn, kn, B, S)

    kcmp, vcmp = _compress(
        kc, vc, jnp.tile(cmp_pe_k, (1, G)), jnp.tile(cmp_pe_v, (1, G)),
        _block_diag_w1(cmp_wk1).astype(BF16), cmp_wk2.astype(BF16), _block_diag_w1(cmp_wv1).astype(BF16),
        jnp.pad(cmp_wv2, ((0, 0), (0, LANES - HEAD_DIM))).astype(BF16), B, S)

    o_nsa = _nsa(q, kcmp, vcmp, ks, vs, kw, vw, gates.reshape(B, S, LANES))
    o_ret = _retention(rq.reshape(B, S, -1), rk, rv.reshape(B, S, -1),
                       rg.reshape(B, S, -1), ret_norm_w.reshape(1, RET_V_WIDTH))

    out = _out_ffn(x2, o_nsa.reshape(N, -1), o_ret.reshape(N, -1), w_out.astype(BF16),
                   ln2_w[None, :], w_up.astype(BF16), w_down.astype(BF16))
    return out.reshape(B, S, D)


def kernel(x, ln1_w, w_in, q_norm_w, k_norm_w, cmp_pe_k, cmp_pe_v, cmp_wk1, cmp_wk2, cmp_wv1, cmp_wv2,
           ret_norm_w, w_out, ln2_w, w_up, w_down):
    h = x
    for l in range(ln1_w.shape[0]):
        h = _layer(h, ln1_w[l], w_in[l], q_norm_w[l], k_norm_w[l], cmp_pe_k[l], cmp_pe_v[l],
                   cmp_wk1[l], cmp_wk2[l], cmp_wv1[l], cmp_wv2[l], ret_norm_w[l], w_out[l],
                   ln2_w[l], w_up[l], w_down[l])
    return h
```

```python
import functools
import math

import jax
import jax.numpy as jnp
import numpy as np
from jax import lax
from jax.experimental import pallas as pl
from jax.experimental.pallas import tpu as pltpu

F32 = jnp.float32
BF16 = jnp.bfloat16

NSA_HEADS = 8
NSA_KV_HEADS = 2
NSA_GROUP = NSA_HEADS // NSA_KV_HEADS
HEAD_DIM = 64
CMP_BLOCK = 32
CMP_STRIDE = 16
CMP_HIDDEN = 256
SEL_BLOCK = 64
SEL_TOPK = 8
WINDOW = 256
RET_HEADS = 4
RET_QK_DIM = 64
RET_V_DIM = 128
RET_CHUNK = 128
ROPE_THETA = 10000.0
EPS = 1e-6
NEG_INF = -1.0e30
SEL_BIAS = 2.0 ** 100

NSA_Q_DIM = NSA_HEADS * HEAD_DIM
NSA_KV_DIM = NSA_KV_HEADS * HEAD_DIM
NSA_GATE_DIM = 3 * NSA_HEADS
RET_QK_WIDTH = RET_HEADS * RET_QK_DIM
RET_V_WIDTH = RET_HEADS * RET_V_DIM

LANES = 128
VMEM_LIMIT_BYTES = 56 * 1024 * 1024

TM_PROJ = 512
TQ = 256
TK = 256
KV_CHUNK = 128
ONES_ROWS = 16
TM_FFN = 512
FF_CHUNK = 512
RET_BATCH = 8


def _dot(a, b):
    return jnp.dot(a, b, preferred_element_type=F32)


def _dot_nt(a, b):
    return lax.dot_general(a, b, (((1,), (1,)), ((), ())), preferred_element_type=F32)


def _sigmoid(x):
    return 1.0 / (1.0 + jnp.exp(-x))


_K0 = 0
_V0 = _K0 + 3 * NSA_KV_DIM
_RQ0 = _V0 + NSA_KV_DIM + LANES
_RV0 = _RQ0 + RET_QK_WIDTH
_RG0 = _RV0 + RET_V_WIDTH
_W_IN_COLS = _RG0 + RET_V_WIDTH
_TQ0 = 0
_TV0 = _TQ0 + NSA_Q_DIM
_TRK0 = _TV0 + 2 * NSA_KV_DIM
_WT_ROWS = _TRK0 + RET_QK_WIDTH


def _permute_w_in(w_in):
    sizes = [NSA_Q_DIM] + [NSA_KV_DIM] * 6 + [NSA_GATE_DIM, RET_QK_WIDTH, RET_QK_WIDTH, RET_V_WIDTH, RET_V_WIDTH]
    off = [int(o) for o in np.concatenate([[0], np.cumsum(sizes)])]
    (q, kc, vc, ks, vs, kw, vw, gate, rq, rk, rv, rg) = [w_in[:, off[i]:off[i + 1]] for i in range(12)]
    d = w_in.shape[0]
    gate = gate.reshape(d, 3, NSA_KV_HEADS, NSA_GROUP).transpose(0, 2, 1, 3).reshape(d, NSA_KV_HEADS, 3 * NSA_GROUP)
    gate = jnp.pad(gate, ((0, 0), (0, 0), (0, HEAD_DIM - 3 * NSA_GROUP))).reshape(d, LANES)
    w = jnp.concatenate([kc, ks, kw, vc, gate, rq, rv, rg], axis=1)
    wt = jnp.concatenate([q, vs, vw, rk], axis=1).T
    assert w.shape[1] == _W_IN_COLS and wt.shape[0] == _WT_ROWS
    return w, wt


def _in_proj_kernel(x_ref, ln_ref, w_ref, wt_ref, cos_ref, sin_ref, cost_ref, sint_ref, qnt_ref, kn_ref,
                    q_ref, kc_ref, vc_ref, ks_ref, vs_ref, kw_ref, vw_ref, gate_ref,
                    rq_ref, rk_ref, rv_ref, rg_ref, *, tiles_per_seq):
    x = x_ref[...]
    ms = jnp.mean(x * x, axis=-1, keepdims=True)
    xn = (x * lax.rsqrt(ms + EPS) * ln_ref[...]).astype(BF16)
    cos = cos_ref[...]
    sin = sin_ref[...]
    lane = lax.broadcasted_iota(jnp.int32, (1, LANES), 1)
    low_half = (lane & (HEAD_DIM // 2)) == 0
    first_head = lane < HEAD_DIM

    def proj(a, b):
        return _dot(xn, w_ref[:, a:b])

    def rope(t):
        swapped = jnp.where(low_half, pltpu.roll(t, LANES - HEAD_DIM // 2, 1), pltpu.roll(t, HEAD_DIM // 2, 1))
        return t * cos + swapped * sin

    def head_norm(t, w):
        t2 = t * t
        s0 = jnp.sum(jnp.where(first_head, t2, 0.0), axis=-1, keepdims=True)
        s1 = jnp.sum(jnp.where(first_head, 0.0, t2), axis=-1, keepdims=True)
        msq = jnp.where(first_head, s0, s1) * (1.0 / HEAD_DIM)
        return t * lax.rsqrt(msq + EPS) * w

    def tiles(sec):
        return [sec[:, c * LANES:(c + 1) * LANES] for c in range(sec.shape[1] // LANES)]

    def split_heads(t, fill):
        return [jnp.where(first_head, t, fill), jnp.where(first_head, pltpu.roll(t, HEAD_DIM, 1), fill)]

    tm = x.shape[0]

    def proj_t(a, b):
        return _dot_nt(wt_ref[a:b, :], xn)

    cos_t = cost_ref[...]
    sin_t = sint_ref[...]

    def rope_t(t):
        half = HEAD_DIM // 2
        return t * cos_t + jnp.concatenate([t[half:], t[:half]], axis=0) * sin_t

    q_t = proj_t(_TQ0, _TV0)
    qn_t = qnt_ref[...]
    for h in range(NSA_HEADS):
        t = q_t[h * HEAD_DIM:(h + 1) * HEAD_DIM]
        t = t * lax.rsqrt(jnp.mean(t * t, axis=0, keepdims=True) + EPS) * qn_t
        t = rope_t(t) * (HEAD_DIM ** -0.5 * math.log2(math.e))
        q_ref[0, h // NSA_GROUP, h % NSA_GROUP] = t.astype(BF16)

    def store_values_t(v, out_ref):
        chunk = out_ref.shape[-1]
        ones = jnp.ones((ONES_ROWS, chunk), BF16)
        for c in range(tm // chunk):
            for g in range(NSA_KV_HEADS):
                out_ref[0, g, c, 0:HEAD_DIM, :] = v[g * HEAD_DIM:(g + 1) * HEAD_DIM,
                                                    c * chunk:(c + 1) * chunk].astype(BF16)
                out_ref[0, g, c, HEAD_DIM:HEAD_DIM + ONES_ROWS, :] = ones

    v_t = proj_t(_TV0, _TRK0)
    store_values_t(v_t[0:NSA_KV_DIM], vs_ref)
    store_values_t(v_t[NSA_KV_DIM:2 * NSA_KV_DIM], vw_ref)

    rk_t = proj_t(_TRK0, _WT_ROWS)
    for h in range(RET_HEADS):
        rows = slice(h * RET_QK_DIM, (h + 1) * RET_QK_DIM)
        t = (rope_t(rk_t[rows]) * (RET_QK_DIM ** -0.5)).astype(BF16)
        for j in range(tm // RET_CHUNK):
            rk_ref[0, j, rows, :] = t[:, j * RET_CHUNK:(j + 1) * RET_CHUNK]

    ksec = tiles(proj(_K0, _V0))
    kc = rope(head_norm(ksec[0], kn_ref[0:1, :]))
    ks = rope(head_norm(ksec[1], kn_ref[1:2, :]))
    kw = rope(head_norm(ksec[2], kn_ref[2:3, :])).astype(BF16)
    s0 = (pl.program_id(0) % tiles_per_seq) * tm
    blk = jnp.right_shift(s0 + lax.broadcasted_iota(jnp.int32, (tm, 1), 0), int(math.log2(SEL_BLOCK)))
    onehot = (lane - HEAD_DIM == blk).astype(F32)
    ks_aug = split_heads(ks, onehot)
    kc_ref[...] = kc
    for g in range(NSA_KV_HEADS):
        ks_ref[0, g] = ks_aug[g].astype(BF16)
        kw_ref[0, g] = kw[:, g * HEAD_DIM:(g + 1) * HEAD_DIM]

    vsec = tiles(proj(_V0, _RQ0))
    vc_ref[...] = vsec[0]
    gate_ref[...] = vsec[1]

    for c, t in enumerate(tiles(proj(_RQ0, _RV0))):
        rq_ref[:, c * LANES:(c + 1) * LANES] = rope(t).astype(BF16)
    rv_ref[...] = proj(_RV0, _RG0).astype(BF16)
    rg_ref[...] = proj(_RG0, _W_IN_COLS).astype(BF16)


def _in_proj(x2, ln1, w_perm, w_perm_t, cos, sin, cos_t, sin_t, qn_t, kn, B, S):
    N, D = x2.shape
    tm = TM_PROJ
    spt = S // tm
    G, R, dh = NSA_KV_HEADS, NSA_GROUP, HEAD_DIM

    def row(i):
        return (i, 0)

    def hm(i):
        return (i // spt, 0, i % spt, 0)

    hm_spec = pl.BlockSpec((1, G, tm, dh), hm)
    out_shape = [
        jax.ShapeDtypeStruct((B, G, R, dh, S), BF16),
        jax.ShapeDtypeStruct((N, LANES), F32),
        jax.ShapeDtypeStruct((N, LANES), F32),
        jax.ShapeDtypeStruct((B, G, S, LANES), BF16),
        jax.ShapeDtypeStruct((B, G, S // TK, dh + ONES_ROWS, TK), BF16),
        jax.ShapeDtypeStruct((B, G, S, dh), BF16),
        jax.ShapeDtypeStruct((B, G, S // KV_CHUNK, dh + ONES_ROWS, KV_CHUNK), BF16),
        jax.ShapeDtypeStruct((N, LANES), F32),
        jax.ShapeDtypeStruct((N, RET_QK_WIDTH), BF16),
        jax.ShapeDtypeStruct((B, S // RET_CHUNK, RET_QK_WIDTH, RET_CHUNK), BF16),
        jax.ShapeDtypeStruct((N, RET_V_WIDTH), BF16),
        jax.ShapeDtypeStruct((N, RET_V_WIDTH), BF16),
    ]
    out_specs = [
        pl.BlockSpec((1, G, R, dh, tm), lambda i: (i // spt, 0, 0, 0, i % spt)),
        pl.BlockSpec((tm, LANES), row), pl.BlockSpec((tm, LANES), row), pl.BlockSpec((1, G, tm, LANES), hm),
        pl.BlockSpec((1, G, tm // TK, dh + ONES_ROWS, TK), lambda i: (i // spt, 0, i % spt, 0, 0)),
        hm_spec,
        pl.BlockSpec((1, G, tm // KV_CHUNK, dh + ONES_ROWS, KV_CHUNK), lambda i: (i // spt, 0, i % spt, 0, 0)),
        pl.BlockSpec((tm, LANES), row),
        pl.BlockSpec((tm, RET_QK_WIDTH), row),
        pl.BlockSpec((1, tm // RET_CHUNK, RET_QK_WIDTH, RET_CHUNK), lambda i: (i // spt, i % spt, 0, 0)),
        pl.BlockSpec((tm, RET_V_WIDTH), row),
        pl.BlockSpec((tm, RET_V_WIDTH), row),
    ]
    const = lambda i: (0, 0)
    in_specs = [
        pl.BlockSpec((tm, D), row),
        pl.BlockSpec((1, D), const),
        pl.BlockSpec((D, _W_IN_COLS), const, pipeline_mode=pl.Buffered(1)),
        pl.BlockSpec((_WT_ROWS, D), const, pipeline_mode=pl.Buffered(1)),
        pl.BlockSpec((tm, LANES), lambda i: (i % spt, 0)),
        pl.BlockSpec((tm, LANES), lambda i: (i % spt, 0)),
        pl.BlockSpec((dh, tm), lambda i: (0, i % spt)),
        pl.BlockSpec((dh, tm), lambda i: (0, i % spt)),
        pl.BlockSpec((dh, tm), const),
        pl.BlockSpec((3, LANES), const),
    ]
    return pl.pallas_call(
        functools.partial(_in_proj_kernel, tiles_per_seq=spt),
        grid=(N // tm,),
        in_specs=in_specs,
        out_specs=out_specs,
        out_shape=out_shape,
        compiler_params=pltpu.CompilerParams(
            dimension_semantics=("parallel",), vmem_limit_bytes=VMEM_LIMIT_BYTES),
        name="in_proj",
    )(x2, ln1, w_perm, w_perm_t, cos, sin, cos_t, sin_t, qn_t, kn)


def _compress_kernel(k_ref, v_ref, pek_ref, pev_ref, wk1_ref, wk2_ref, wv1_ref, wv2_ref, ko_ref, vo_ref):
    n_seg = k_ref.shape[0] // CMP_STRIDE
    G = NSA_KV_HEADS

    def run(x_ref, pe_ref, w1_ref):
        first = second = None
        for l in range(CMP_STRIDE):
            x = x_ref[pl.ds(l, n_seg, stride=CMP_STRIDE), :]
            a = _dot((x + pe_ref[l:l + 1, :]).astype(BF16), w1_ref[l])
            b = _dot((x + pe_ref[CMP_STRIDE + l:CMP_STRIDE + l + 1, :]).astype(BF16), w1_ref[CMP_STRIDE + l])
            first = a if first is None else first + a
            second = b if second is None else second + b
        h = first + pltpu.roll(second, n_seg - 1, 0)
        return (h * _sigmoid(h)).astype(BF16)

    hk = run(k_ref, pek_ref, wk1_ref)
    hv = run(v_ref, pev_ref, wv1_ref)
    for g in range(G):
        cols = slice(g * CMP_HIDDEN, (g + 1) * CMP_HIDDEN)
        ko_ref[0, g] = _dot(hk[:, cols], wk2_ref[...]).astype(BF16)
        vo_ref[:, g * n_seg:(g + 1) * n_seg] = _dot(hv[:, cols], wv2_ref[...]).T[0:HEAD_DIM, :].astype(BF16)


def _compress(kc, vc, pek, pev, wk1, wk2, wv1, wv2, B, S):
    G = NSA_KV_HEADS
    n_seg = S // CMP_STRIDE
    width = kc.shape[1]
    const2 = lambda b: (0, 0)
    const3 = lambda b: (0, 0, 0)
    tok_spec = pl.BlockSpec((S, width), lambda b: (b, 0))
    return pl.pallas_call(
        _compress_kernel,
        grid=(B,),
        in_specs=[tok_spec, tok_spec,
                  pl.BlockSpec(pek.shape, const2), pl.BlockSpec(pev.shape, const2),
                  pl.BlockSpec(wk1.shape, const3), pl.BlockSpec(wk2.shape, const2),
                  pl.BlockSpec(wv1.shape, const3), pl.BlockSpec(wv2.shape, const2)],
        out_specs=[pl.BlockSpec((1, G, n_seg, HEAD_DIM), lambda b: (b, 0, 0, 0)),
                   pl.BlockSpec((HEAD_DIM, G * n_seg), lambda b: (0, b))],
        out_shape=[jax.ShapeDtypeStruct((B, G, n_seg, HEAD_DIM), BF16),
                   jax.ShapeDtypeStruct((HEAD_DIM, B * G * n_seg), BF16)],
        compiler_params=pltpu.CompilerParams(
            dimension_semantics=("parallel",), vmem_limit_bytes=VMEM_LIMIT_BYTES),
        name="compress",
    )(kc, vc, pek, pev, wk1, wk2, wv1, wv2)


def _block_diag_w1(w1):
    G = NSA_KV_HEADS
    w = w1.reshape(CMP_BLOCK, HEAD_DIM, CMP_HIDDEN)
    z = jnp.zeros_like(w)
    rows = [jnp.concatenate([w if j == g else z for j in range(G)], axis=2) for g in range(G)]
    return jnp.concatenate(rows, axis=1)


def _nsa_kernel(q_ref, kcmp_ref, vcmp_ref, ks_ref, vs_ref, kw_ref, vw_ref, gate_ref,
                o_ref, m_ref, acc_ref, s0_ref):
    G, R = NSA_KV_HEADS, NSA_GROUP
    n_cp = kcmp_ref.shape[2]
    S = ks_ref.shape[2]
    n_cmp = (S - CMP_BLOCK) // CMP_STRIDE + 1
    n_blk = S // SEL_BLOCK
    qi = pl.program_id(1)
    q0 = qi * TQ
    t_q = q0 + lax.broadcasted_iota(jnp.int32, (1, TQ), 1)
    qs = [jnp.concatenate([q_ref[0, g, r] for r in range(R)], axis=1) for g in range(G)]

    def heads(x):
        return [x[:, r * TQ:(r + 1) * TQ] for r in range(R)]

    def masked_exp(s_heads, mask):
        es = []
        for s in s_heads:
            s = jnp.where(mask, s, NEG_INF)
            es.append(jnp.exp2(s - jnp.max(s, axis=0, keepdims=True)))
        return es

    c_idx = lax.broadcasted_iota(jnp.int32, (n_cp, TQ), 0)
    cmask = ((c_idx * CMP_STRIDE + (CMP_BLOCK - 1)) <= t_q) & (c_idx < n_cmp)
    jj = lax.broadcasted_iota(jnp.int32, (n_blk, n_cp), 0)
    cc = lax.broadcasted_iota(jnp.int32, (n_blk, n_cp), 1)
    overlap = ((cc * CMP_STRIDE < (jj + 1) * SEL_BLOCK) & (cc * CMP_STRIDE + CMP_BLOCK > jj * SEL_BLOCK)
               & (cc < n_cmp)).astype(BF16)
    jb = lax.broadcasted_iota(jnp.int32, (n_blk, TQ), 0)
    jb_f = jb.astype(F32)
    cur = jnp.right_shift(t_q, int(math.log2(SEL_BLOCK)))
    forced = (jb == 0) | (jb == cur) | (jb == cur - 1)
    valid = jb <= cur

    n_sub = TQ // KV_CHUNK
    n_wc = (KV_CHUNK + WINDOW) // KV_CHUNK
    win_c0, win_mask, win_q = [], [], []
    for h in range(n_sub):
        c0 = jnp.maximum(q0 // KV_CHUNK + h - WINDOW // KV_CHUNK, 0)
        kpos = c0 * KV_CHUNK + lax.broadcasted_iota(jnp.int32, (n_wc * KV_CHUNK, KV_CHUNK), 0)
        diff = t_q[:, h * KV_CHUNK:(h + 1) * KV_CHUNK] - kpos
        win_c0.append(c0)
        win_mask.append((diff >= 0) & (diff < WINDOW))
        win_q.append([jnp.concatenate([qs[g][:, r * TQ + h * KV_CHUNK:r * TQ + (h + 1) * KV_CHUNK]
                                       for r in range(R)], axis=1) for g in range(G)])

    def window_keys(g, h):
        return kw_ref[0, g, pl.ds(pl.multiple_of(win_c0[h] * KV_CHUNK, KV_CHUNK), n_wc * KV_CHUNK), :]

    s_cmp = [_dot(kcmp_ref[0, g], qs[g]) for g in range(G)]
    s_win = [[_dot(window_keys(g, h), win_q[h][g]) for h in range(n_sub)] for g in range(G)]
    any_cmp = t_q >= CMP_BLOCK - 1
    p_cmp, o_cmp, o_win = [], [], []
    for g in range(G):
        es = masked_exp(heads(s_cmp[g]), cmask)
        p_cmp.append([e * jnp.where(any_cmp, 1.0 / jnp.sum(e, axis=0, keepdims=True), 0.0) for e in es])

    sels = []
    for g in range(G):
        p_c = p_cmp[g]
        p_sum = p_c[0]
        for r in range(1, R):
            p_sum = p_sum + p_c[r]
        p_hi = p_sum.astype(BF16)
        p_lo = (p_sum - p_hi.astype(F32)).astype(BF16)
        p_slc = _dot(overlap, p_hi) + _dot(overlap, p_lo)
        work = jnp.where(valid & jnp.logical_not(forced), p_slc, -1.0)
        picked = forced
        for _ in range(min(SEL_TOPK, n_blk) - 3):
            best = jnp.max(work, axis=0, keepdims=True)
            first = jnp.min(jnp.where(work == best, jb_f, float(n_blk)), axis=0, keepdims=True)
            hit = jb_f == first
            picked = picked | hit
            work = jnp.where(hit, -2.0, work)
        bias = jnp.where(picked, 0.0, -SEL_BIAS).astype(BF16)
        sels.append(jnp.concatenate([qs[g], jnp.concatenate([bias] * R, axis=1),
                                     jnp.zeros((LANES - HEAD_DIM - n_blk, R * TQ), BF16)], axis=0))

    m_ref[...] = jnp.full(m_ref.shape, NEG_INF, F32)
    acc_ref[...] = jnp.zeros(acc_ref.shape, F32)

    def scores(g, kt):
        return _dot(ks_ref[0, g, pl.ds(pl.multiple_of(kt * TK, TK), TK), :], sels[g])

    def accumulate(g, kt, s, causal):
        if causal is not None:
            s = jnp.concatenate([jnp.where(causal, sh, NEG_INF) for sh in heads(s)], axis=1)
        m_old = m_ref[g]
        m_new = jnp.maximum(m_old, jnp.max(s, axis=0, keepdims=True))
        alpha = jnp.exp2(m_old - m_new)
        p = jnp.exp2(s - m_new)
        acc_ref[g] = alpha * acc_ref[g] + _dot(vs_ref[0, g, kt], p.astype(BF16))
        m_ref[g] = m_new

    assert G == 2
    s0_ref[...] = scores(0, 0)

    for g in range(G):
        o_cmp.append(_dot(vcmp_ref[:, g * n_cp:(g + 1) * n_cp],
                          jnp.concatenate(p_cmp[g], axis=1).astype(BF16)))
    for g in range(G):
        sub = []
        for h in range(n_sub):
            s_heads = [s_win[g][h][:, r * KV_CHUNK:(r + 1) * KV_CHUNK] for r in range(R)]
            e_w = jnp.concatenate(masked_exp(s_heads, win_mask[h]), axis=1).astype(BF16)
            acc = _dot(vw_ref[0, g, win_c0[h]], e_w[0:KV_CHUNK])
            for j in range(1, n_wc):
                acc = acc + _dot(vw_ref[0, g, win_c0[h] + j], e_w[j * KV_CHUNK:(j + 1) * KV_CHUNK])
            sub.append(acc)
        o_win.append(jnp.concatenate([sub[h][:, r * KV_CHUNK:(r + 1) * KV_CHUNK]
                                      for r in range(R) for h in range(n_sub)], axis=1))

    def interior_tile(kt, carry):
        s1 = scores(1, kt)
        accumulate(0, kt, s0_ref[...], None)
        s0_ref[...] = scores(0, kt + 1)
        accumulate(1, kt, s1, None)
        return carry

    lax.fori_loop(0, qi, interior_tile, 0)
    causal = (q0 + lax.broadcasted_iota(jnp.int32, (TK, TQ), 0)) <= t_q
    s1 = scores(1, qi)
    accumulate(0, qi, s0_ref[...], causal)
    accumulate(1, qi, s1, causal)

    gates = _sigmoid(gate_ref[0].T)
    outs = []
    for g in range(G):
        o_sel = acc_ref[g]
        for r, (oc, os_, ow) in enumerate(zip(heads(o_cmp[g]), heads(o_sel), heads(o_win[g]))):
            g0 = g * HEAD_DIM + r
            g_sel = gates[g0 + R:g0 + R + 1] * (1.0 / os_[HEAD_DIM:HEAD_DIM + 1])
            g_win = gates[g0 + 2 * R:g0 + 2 * R + 1] * (1.0 / ow[HEAD_DIM:HEAD_DIM + 1])
            outs.append(gates[g0:g0 + 1] * oc + g_sel * os_[0:HEAD_DIM] + g_win * ow[0:HEAD_DIM])
    o_ref[0] = jnp.concatenate(outs, axis=0).T.astype(BF16)


def _nsa(q, kcmp, vcmp, ks, vs, kw, vw, gates):
    B, G, R, dh, S = q.shape
    n_cp = kcmp.shape[2]
    assert TQ == TK, "the key sweep treats exactly one tile per query tile as the diagonal"
    k_spec = lambda a: pl.BlockSpec((1,) + a.shape[1:], lambda b, i: (b, 0, 0, 0))
    vt_spec = lambda a: pl.BlockSpec((1,) + a.shape[1:], lambda b, i: (b, 0, 0, 0, 0))
    return pl.pallas_call(
        _nsa_kernel,
        grid=(B, S // TQ),
        in_specs=[
            pl.BlockSpec((1, G, R, dh, TQ), lambda b, i: (b, 0, 0, 0, i)),
            pl.BlockSpec((1, G, n_cp, dh), lambda b, i: (b, 0, 0, 0)),
            pl.BlockSpec((dh, G * n_cp), lambda b, i: (0, b)),
            k_spec(ks), vt_spec(vs), k_spec(kw), vt_spec(vw),
            pl.BlockSpec((1, TQ, LANES), lambda b, i: (b, i, 0)),
        ],
        out_specs=pl.BlockSpec((1, TQ, G * R * dh), lambda b, i: (b, i, 0)),
        out_shape=jax.ShapeDtypeStruct((B, S, G * R * dh), BF16),
        scratch_shapes=[pltpu.VMEM((G, 1, R * TQ), F32), pltpu.VMEM((G, dh + ONES_ROWS, R * TQ), F32),
                        pltpu.VMEM((TK, R * TQ), F32)],
        compiler_params=pltpu.CompilerParams(
            dimension_semantics=("parallel", "arbitrary"), vmem_limit_bytes=VMEM_LIMIT_BYTES),
        name="nsa_attention",
    )(q, kcmp, vcmp, ks, vs, kw, vw, gates)


def _retention_kernel(q_ref, kt_ref, v_ref, g_ref, w_ref, o_ref, state_ref):
    C = RET_CHUNK
    NB = q_ref.shape[0]

    @pl.when(pl.program_id(1) == 0)
    def _():
        state_ref[...] = jnp.zeros(state_ref.shape, F32)

    i_col = lax.broadcasted_iota(jnp.int32, (C, 1), 0)
    i_row = lax.broadcasted_iota(jnp.int32, (1, C), 1)
    d_int = i_col - i_row
    log_gamma = [math.log(1.0 - 2.0 ** (-5.0 - h)) for h in range(RET_HEADS)]
    dmat = [jnp.where(d_int >= 0, jnp.exp(lg * jnp.maximum(d_int, 0).astype(F32)), 0.0) for lg in log_gamma]
    xi = [jnp.exp(lg * (i_col + 1).astype(F32)) for lg in log_gamma]
    zeta = [jnp.exp(lg * (C - 1 - i_row).astype(F32)) for lg in log_gamma]
    gamma_c = [math.exp(lg * C) for lg in log_gamma]

    units = [(n, h) for n in range(NB) for h in range(RET_HEADS)]
    q, kt, v, st = {}, {}, {}, {}
    for n, h in units:
        q[n, h] = q_ref[n, :, h * RET_QK_DIM:(h + 1) * RET_QK_DIM]
        kt[n, h] = kt_ref[n, 0, h * RET_QK_DIM:(h + 1) * RET_QK_DIM, :]
        v[n, h] = v_ref[n, :, h * RET_V_DIM:(h + 1) * RET_V_DIM]
        st[n, h] = state_ref[n, h]
    inner = {u: _dot(q[u], kt[u]) for u in units}
    cross = {u: _dot(q[u], st[u].astype(BF16)) for u in units}
    kv = {u: _dot((kt[u].astype(F32) * zeta[u[1]]).astype(BF16), v[u]) for u in units}
    for n, h in units:
        u = (n, h)
        vsl = slice(h * RET_V_DIM, (h + 1) * RET_V_DIM)
        y = _dot((inner[u] * dmat[h]).astype(BF16), v[u]) + cross[u] * xi[h]
        state_ref[n, h] = gamma_c[h] * st[u] + kv[u]

        mu = jnp.mean(y, axis=-1, keepdims=True)
        yc = y - mu
        var = jnp.mean(yc * yc, axis=-1, keepdims=True)
        yn = yc * lax.rsqrt(var + EPS) * w_ref[:, vsl]
        gate = g_ref[n, :, vsl].astype(F32)
        o_ref[n, :, vsl] = (gate * _sigmoid(gate) * yn).astype(BF16)


def _retention(rq, rkt, rv, rg, w):
    B, S, _ = rq.shape
    C = RET_CHUNK
    nb = math.gcd(RET_BATCH, B)
    q_spec = pl.BlockSpec((nb, C, RET_QK_WIDTH), lambda b, c: (b, c, 0))
    kt_spec = pl.BlockSpec((nb, 1, RET_QK_WIDTH, C), lambda b, c: (b, c, 0, 0))
    v_spec = pl.BlockSpec((nb, C, RET_V_WIDTH), lambda b, c: (b, c, 0))
    return pl.pallas_call(
        _retention_kernel,
        grid=(B // nb, S // C),
        in_specs=[q_spec, kt_spec, v_spec, v_spec, pl.BlockSpec((1, RET_V_WIDTH), lambda b, c: (0, 0))],
        out_specs=v_spec,
        out_shape=jax.ShapeDtypeStruct((B, S, RET_V_WIDTH), BF16),
        scratch_shapes=[pltpu.VMEM((nb, RET_HEADS, RET_QK_DIM, RET_V_DIM), F32)],
        compiler_params=pltpu.CompilerParams(
            dimension_semantics=("parallel", "arbitrary"), vmem_limit_bytes=VMEM_LIMIT_BYTES),
        name="retention",
    )(rq, rkt, rv, rg, w)


def _out_ffn_kernel(x_ref, a_ref, r_ref, wo_ref, ln_ref, wu_ref, wd_ref, o_ref):
    na = a_ref.shape[1]
    mix = _dot(a_ref[...], wo_ref[0:na, :]) + _dot(r_ref[...], wo_ref[na:, :])
    h = x_ref[...] + mix
    ms = jnp.mean(h * h, axis=-1, keepdims=True)
    hn = (h * lax.rsqrt(ms + EPS) * ln_ref[...]).astype(BF16)
    d_ff = wu_ref.shape[1]
    acc = None
    for f in range(d_ff // FF_CHUNK):
        cols = slice(f * FF_CHUNK, (f + 1) * FF_CHUNK)
        u = jnp.maximum(_dot(hn, wu_ref[:, cols]), 0.0)
        d = _dot((u * u).astype(BF16), wd_ref[cols, :])
        acc = d if acc is None else acc + d
    o_ref[...] = h + acc


def _out_ffn(x2, o_nsa, o_ret, w_out, ln2, w_up, w_down):
    N, D = x2.shape
    tm = TM_FFN
    row = lambda i: (i, 0)
    const = lambda i: (0, 0)
    resident = functools.partial(pl.BlockSpec, index_map=const, pipeline_mode=pl.Buffered(1))
    return pl.pallas_call(
        _out_ffn_kernel,
        grid=(N // tm,),
        in_specs=[
            pl.BlockSpec((tm, D), row),
            pl.BlockSpec((tm, o_nsa.shape[1]), row),
            pl.BlockSpec((tm, o_ret.shape[1]), row),
            resident(w_out.shape),
            pl.BlockSpec((1, D), const),
            resident(w_up.shape),
            resident(w_down.shape),
        ],
        out_specs=pl.BlockSpec((tm, D), row),
        out_shape=jax.ShapeDtypeStruct((N, D), F32),
        compiler_params=pltpu.CompilerParams(
            dimension_semantics=("parallel",), vmem_limit_bytes=VMEM_LIMIT_BYTES),
        name="out_ffn",
    )(x2, o_nsa, o_ret, w_out, ln2, w_up, w_down)


def _rope_tables(S):
    half = HEAD_DIM // 2
    inv = ROPE_THETA ** (-jnp.arange(half, dtype=F32) / half)
    ang = jnp.arange(S).astype(F32)[:, None] * inv[None, :]
    cos, sin = jnp.cos(ang), jnp.sin(ang)
    reps = LANES // HEAD_DIM
    cos_h = jnp.concatenate([cos, cos], axis=-1)
    sin_h = jnp.concatenate([-sin, sin], axis=-1)
    return jnp.tile(cos_h, (1, reps)), jnp.tile(sin_h, (1, reps)), cos_h.T, sin_h.T


def _layer(h, ln1_w, w_in, q_norm_w, k_norm_w, cmp_pe_k, cmp_pe_v, cmp_wk1, cmp_wk2,
           cmp_wv1, cmp_wv2, ret_norm_w, w_out, ln2_w, w_up, w_down):
    B, S, D = h.shape
    N = B * S
    G = NSA_KV_HEADS
    x2 = h.reshape(N, D)

    w_perm, w_perm_t = _permute_w_in(w_in)
    cos_l, sin_l, cos_t, sin_t = _rope_tables(S)
    reps = LANES // HEAD_DIM
    qn_t = jnp.broadcast_to(q_norm_w[:, None], (HEAD_DIM, TM_PROJ))
    kn = jnp.tile(k_norm_w, (1, reps))

    (q, kc, vc, ks, vs, kw, vw, gates, rq, rk, rv, rg) = _in_proj(
        x2, ln1_w[None, :], w_perm.astype(BF16), w_perm_t.astype(BF16), cos_l, sin_l, cos_t, sin_t, qn_t, kn, B, S)

    kcmp, vcmp = _compress(
        kc, vc, jnp.tile(cmp_pe_k, (1, G)), jnp.tile(cmp_pe_v, (1, G)),
        _block_diag_w1(cmp_wk1).astype(BF16), cmp_wk2.astype(BF16), _block_diag_w1(cmp_wv1).astype(BF16),
        jnp.pad(cmp_wv2, ((0, 0), (0, LANES - HEAD_DIM))).astype(BF16), B, S)

    o_nsa = _nsa(q, kcmp, vcmp, ks, vs, kw, vw, gates.reshape(B, S, LANES))
    o_ret = _retention(rq.reshape(B, S, -1), rk, rv.reshape(B, S, -1),
                       rg.reshape(B, S, -1), ret_norm_w.reshape(1, RET_V_WIDTH))

    out = _out_ffn(x2, o_nsa.reshape(N, -1), o_ret.reshape(N, -1), w_out.astype(BF16),
                   ln2_w[None, :], w_up.astype(BF16), w_down.astype(BF16))
    return out.reshape(B, S, D)


def kernel(x, ln1_w, w_in, q_norm_w, k_norm_w, cmp_pe_k, cmp_pe_v, cmp_wk1, cmp_wk2, cmp_wv1, cmp_wv2,
           ret_norm_w, w_out, ln2_w, w_up, w_down):
    h = x
    for l in range(ln1_w.shape[0]):
        h = _layer(h, ln1_w[l], w_in[l], q_norm_w[l], k_norm_w[l], cmp_pe_k[l], cmp_pe_v[l],
                   cmp_wk1[l], cmp_wk2[l], cmp_wv1[l], cmp_wv2[l], ret_norm_w[l], w_out[l],
                   ln2_w[l], w_up[l], w_down[l])
    return h
```

```python
import functools
import math

import jax
import jax.numpy as jnp
import numpy as np
from jax import lax
from jax.experimental import pallas as pl
from jax.experimental.pallas import tpu as pltpu

F32 = jnp.float32
BF16 = jnp.bfloat16

NSA_HEADS = 8
NSA_KV_HEADS = 2
NSA_GROUP = NSA_HEADS // NSA_KV_HEADS
HEAD_DIM = 64
CMP_BLOCK = 32
CMP_STRIDE = 16
CMP_HIDDEN = 256
SEL_BLOCK = 64
SEL_TOPK = 8
WINDOW = 256
RET_HEADS = 4
RET_QK_DIM = 64
RET_V_DIM = 128
RET_CHUNK = 128
ROPE_THETA = 10000.0
EPS = 1e-6
NEG_INF = -1.0e30
SEL_BIAS = 2.0 ** 100

NSA_Q_DIM = NSA_HEADS * HEAD_DIM
NSA_KV_DIM = NSA_KV_HEADS * HEAD_DIM
NSA_GATE_DIM = 3 * NSA_HEADS
RET_QK_WIDTH = RET_HEADS * RET_QK_DIM
RET_V_WIDTH = RET_HEADS * RET_V_DIM

LANES = 128
VMEM_LIMIT_BYTES = 56 * 1024 * 1024

TM_PROJ = 512
TQ = 256
TK = 256
KV_CHUNK = 128
ONES_ROWS = 16
TM_FFN = 512
FF_CHUNK = 512
RET_BATCH = 8


def _dot(a, b):
    return jnp.dot(a, b, preferred_element_type=F32)


def _dot_nt(a, b):
    return lax.dot_general(a, b, (((1,), (1,)), ((), ())), preferred_element_type=F32)


def _sigmoid(x):
    return 1.0 / (1.0 + jnp.exp(-x))


_K0 = 0
_V0 = _K0 + 3 * NSA_KV_DIM
_RQ0 = _V0 + NSA_KV_DIM + LANES
_RV0 = _RQ0 + RET_QK_WIDTH
_RG0 = _RV0 + RET_V_WIDTH
_W_IN_COLS = _RG0 + RET_V_WIDTH
_TQ0 = 0
_TV0 = _TQ0 + NSA_Q_DIM
_TRK0 = _TV0 + 2 * NSA_KV_DIM
_WT_ROWS = _TRK0 + RET_QK_WIDTH


def _permute_w_in(w_in):
    sizes = [NSA_Q_DIM] + [NSA_KV_DIM] * 6 + [NSA_GATE_DIM, RET_QK_WIDTH, RET_QK_WIDTH, RET_V_WIDTH, RET_V_WIDTH]
    off = [int(o) for o in np.concatenate([[0], np.cumsum(sizes)])]
    (q, kc, vc, ks, vs, kw, vw, gate, rq, rk, rv, rg) = [w_in[:, off[i]:off[i + 1]] for i in range(12)]
    d = w_in.shape[0]
    gate = gate.reshape(d, 3, NSA_KV_HEADS, NSA_GROUP).transpose(0, 2, 1, 3).reshape(d, NSA_KV_HEADS, 3 * NSA_GROUP)
    gate = jnp.pad(gate, ((0, 0), (0, 0), (0, HEAD_DIM - 3 * NSA_GROUP))).reshape(d, LANES)
    w = jnp.concatenate([kc, ks, kw, vc, gate, rq, rv, rg], axis=1)
    wt = jnp.concatenate([q, vs, vw, rk], axis=1).T
    assert w.shape[1] == _W_IN_COLS and wt.shape[0] == _WT_ROWS
    return w, wt


def _in_proj_kernel(x_ref, ln_ref, w_ref, wt_ref, cos_ref, sin_ref, cost_ref, sint_ref, qnt_ref, kn_ref,
                    q_ref, kc_ref, vc_ref, ks_ref, vs_ref, kw_ref, vw_ref, gate_ref,
                    rq_ref, rk_ref, rv_ref, rg_ref, *, tiles_per_seq):
    x = x_ref[...]
    tm = x.shape[0]
    xn = (x * ln_ref[...]).astype(BF16)
    r_col = lax.rsqrt(jnp.mean(x * x, axis=-1, keepdims=True) + EPS)
    r_row = jnp.broadcast_to(r_col, (tm, LANES)).T[0:1, :]
    cos = cos_ref[...]
    sin = sin_ref[...]
    lane = lax.broadcasted_iota(jnp.int32, (1, LANES), 1)
    low_half = (lane & (HEAD_DIM // 2)) == 0
    first_head = lane < HEAD_DIM

    def proj(a, b):
        return _dot(xn, w_ref[:, a:b]) * r_col

    def rope(t):
        swapped = jnp.where(low_half, pltpu.roll(t, LANES - HEAD_DIM // 2, 1), pltpu.roll(t, HEAD_DIM // 2, 1))
        return t * cos + swapped * sin

    def head_norm(t, w):
        t2 = t * t
        s0 = jnp.sum(jnp.where(first_head, t2, 0.0), axis=-1, keepdims=True)
        s1 = jnp.sum(jnp.where(first_head, 0.0, t2), axis=-1, keepdims=True)
        msq = jnp.where(first_head, s0, s1) * (1.0 / HEAD_DIM)
        return t * lax.rsqrt(msq + EPS) * w

    def tiles(sec):
        return [sec[:, c * LANES:(c + 1) * LANES] for c in range(sec.shape[1] // LANES)]

    def split_heads(t, fill):
        return [jnp.where(first_head, t, fill), jnp.where(first_head, pltpu.roll(t, HEAD_DIM, 1), fill)]

    def proj_t(a, b):
        return _dot_nt(wt_ref[a:b, :], xn) * r_row

    cos_t = cost_ref[...]
    sin_t = sint_ref[...]

    def rope_t(t):
        half = HEAD_DIM // 2
        return t * cos_t + jnp.concatenate([t[half:], t[:half]], axis=0) * sin_t

    q_t = proj_t(_TQ0, _TV0)
    qn_t = qnt_ref[...]
    for h in range(NSA_HEADS):
        t = q_t[h * HEAD_DIM:(h + 1) * HEAD_DIM]
        t = t * lax.rsqrt(jnp.mean(t * t, axis=0, keepdims=True) + EPS) * qn_t
        t = rope_t(t) * (HEAD_DIM ** -0.5 * math.log2(math.e))
        q_ref[0, h // NSA_GROUP, h % NSA_GROUP] = t.astype(BF16)

    def store_values_t(v, out_ref):
        chunk = out_ref.shape[-1]
        ones = jnp.ones((ONES_ROWS, chunk), BF16)
        for c in range(tm // chunk):
            for g in range(NSA_KV_HEADS):
                out_ref[0, g, c, 0:HEAD_DIM, :] = v[g * HEAD_DIM:(g + 1) * HEAD_DIM,
                                                    c * chunk:(c + 1) * chunk].astype(BF16)
                out_ref[0, g, c, HEAD_DIM:HEAD_DIM + ONES_ROWS, :] = ones

    v_t = proj_t(_TV0, _TRK0)
    store_values_t(v_t[0:NSA_KV_DIM], vs_ref)
    store_values_t(v_t[NSA_KV_DIM:2 * NSA_KV_DIM], vw_ref)

    rk_t = proj_t(_TRK0, _WT_ROWS)
    for h in range(RET_HEADS):
        rows = slice(h * RET_QK_DIM, (h + 1) * RET_QK_DIM)
        t = (rope_t(rk_t[rows]) * (RET_QK_DIM ** -0.5)).astype(BF16)
        for j in range(tm // RET_CHUNK):
            rk_ref[0, j, rows, :] = t[:, j * RET_CHUNK:(j + 1) * RET_CHUNK]

    ksec = tiles(proj(_K0, _V0))
    kc = rope(head_norm(ksec[0], kn_ref[0:1, :]))
    ks = rope(head_norm(ksec[1], kn_ref[1:2, :]))
    kw = rope(head_norm(ksec[2], kn_ref[2:3, :])).astype(BF16)
    s0 = (pl.program_id(0) % tiles_per_seq) * tm
    blk = jnp.right_shift(s0 + lax.broadcasted_iota(jnp.int32, (tm, 1), 0), int(math.log2(SEL_BLOCK)))
    onehot = (lane - HEAD_DIM == blk).astype(F32)
    ks_aug = split_heads(ks, onehot)
    kc_ref[...] = kc
    for g in range(NSA_KV_HEADS):
        ks_ref[0, g] = ks_aug[g].astype(BF16)
        kw_ref[0, g] = kw[:, g * HEAD_DIM:(g + 1) * HEAD_DIM]

    vsec = tiles(proj(_V0, _RQ0))
    vc_ref[...] = vsec[0]
    gate_ref[...] = vsec[1]

    for c, t in enumerate(tiles(proj(_RQ0, _RV0))):
        rq_ref[:, c * LANES:(c + 1) * LANES] = rope(t).astype(BF16)
    rv_ref[...] = proj(_RV0, _RG0).astype(BF16)
    rg_ref[...] = proj(_RG0, _W_IN_COLS).astype(BF16)


def _in_proj(x2, ln1, w_perm, w_perm_t, cos, sin, cos_t, sin_t, qn_t, kn, B, S):
    N, D = x2.shape
    tm = TM_PROJ
    spt = S // tm
    G, R, dh = NSA_KV_HEADS, NSA_GROUP, HEAD_DIM

    def row(i):
        return (i, 0)

    def hm(i):
        return (i // spt, 0, i % spt, 0)

    hm_spec = pl.BlockSpec((1, G, tm, dh), hm)
    out_shape = [
        jax.ShapeDtypeStruct((B, G, R, dh, S), BF16),
        jax.ShapeDtypeStruct((N, LANES), F32),
        jax.ShapeDtypeStruct((N, LANES), F32),
        jax.ShapeDtypeStruct((B, G, S, LANES), BF16),
        jax.ShapeDtypeStruct((B, G, S // TK, dh + ONES_ROWS, TK), BF16),
        jax.ShapeDtypeStruct((B, G, S, dh), BF16),
        jax.ShapeDtypeStruct((B, G, S // KV_CHUNK, dh + ONES_ROWS, KV_CHUNK), BF16),
        jax.ShapeDtypeStruct((N, LANES), F32),
        jax.ShapeDtypeStruct((N, RET_QK_WIDTH), BF16),
        jax.ShapeDtypeStruct((B, S // RET_CHUNK, RET_QK_WIDTH, RET_CHUNK), BF16),
        jax.ShapeDtypeStruct((N, RET_V_WIDTH), BF16),
        jax.ShapeDtypeStruct((N, RET_V_WIDTH), BF16),
    ]
    out_specs = [
        pl.BlockSpec((1, G, R, dh, tm), lambda i: (i // spt, 0, 0, 0, i % spt)),
        pl.BlockSpec((tm, LANES), row), pl.BlockSpec((tm, LANES), row), pl.BlockSpec((1, G, tm, LANES), hm),
        pl.BlockSpec((1, G, tm // TK, dh + ONES_ROWS, TK), lambda i: (i // spt, 0, i % spt, 0, 0)),
        hm_spec,
        pl.BlockSpec((1, G, tm // KV_CHUNK, dh + ONES_ROWS, KV_CHUNK), lambda i: (i // spt, 0, i % spt, 0, 0)),
        pl.BlockSpec((tm, LANES), row),
        pl.BlockSpec((tm, RET_QK_WIDTH), row),
        pl.BlockSpec((1, tm // RET_CHUNK, RET_QK_WIDTH, RET_CHUNK), lambda i: (i // spt, i % spt, 0, 0)),
        pl.BlockSpec((tm, RET_V_WIDTH), row),
        pl.BlockSpec((tm, RET_V_WIDTH), row),
    ]
    const = lambda i: (0, 0)
    in_specs = [
        pl.BlockSpec((tm, D), row),
        pl.BlockSpec((1, D), const),
        pl.BlockSpec((D, _W_IN_COLS), const, pipeline_mode=pl.Buffered(1)),
        pl.BlockSpec((_WT_ROWS, D), const, pipeline_mode=pl.Buffered(1)),
        pl.BlockSpec((tm, LANES), lambda i: (i % spt, 0)),
        pl.BlockSpec((tm, LANES), lambda i: (i % spt, 0)),
        pl.BlockSpec((dh, tm), lambda i: (0, i % spt)),
        pl.BlockSpec((dh, tm), lambda i: (0, i % spt)),
        pl.BlockSpec((dh, tm), const),
        pl.BlockSpec((3, LANES), const),
    ]
    return pl.pallas_call(
        functools.partial(_in_proj_kernel, tiles_per_seq=spt),
        grid=(N // tm,),
        in_specs=in_specs,
        out_specs=out_specs,
        out_shape=out_shape,
        compiler_params=pltpu.CompilerParams(
            dimension_semantics=("parallel",), vmem_limit_bytes=VMEM_LIMIT_BYTES),
        name="in_proj",
    )(x2, ln1, w_perm, w_perm_t, cos, sin, cos_t, sin_t, qn_t, kn)


def _compress_kernel(k_ref, v_ref, pek_ref, pev_ref, wk1_ref, wk2_ref, wv1_ref, wv2_ref, ko_ref, vo_ref):
    n_seg = k_ref.shape[0] // CMP_STRIDE
    G = NSA_KV_HEADS

    def run(x_ref, pe_ref, w1_ref):
        first = second = None
        for l in range(CMP_STRIDE):
            x = x_ref[pl.ds(l, n_seg, stride=CMP_STRIDE), :]
            a = _dot((x + pe_ref[l:l + 1, :]).astype(BF16), w1_ref[l])
            b = _dot((x + pe_ref[CMP_STRIDE + l:CMP_STRIDE + l + 1, :]).astype(BF16), w1_ref[CMP_STRIDE + l])
            first = a if first is None else first + a
            second = b if second is None else second + b
        h = first + pltpu.roll(second, n_seg - 1, 0)
        return (h * _sigmoid(h)).astype(BF16)

    hk = run(k_ref, pek_ref, wk1_ref)
    hv = run(v_ref, pev_ref, wv1_ref)
    for g in range(G):
        cols = slice(g * CMP_HIDDEN, (g + 1) * CMP_HIDDEN)
        ko_ref[0, g] = _dot(hk[:, cols], wk2_ref[...]).astype(BF16)
        vo_ref[:, g * n_seg:(g + 1) * n_seg] = _dot(hv[:, cols], wv2_ref[...]).T[0:HEAD_DIM, :].astype(BF16)


def _compress(kc, vc, pek, pev, wk1, wk2, wv1, wv2, B, S):
    G = NSA_KV_HEADS
    n_seg = S // CMP_STRIDE
    width = kc.shape[1]
    const2 = lambda b: (0, 0)
    const3 = lambda b: (0, 0, 0)
    tok_spec = pl.BlockSpec((S, width), lambda b: (b, 0))
    return pl.pallas_call(
        _compress_kernel,
        grid=(B,),
        in_specs=[tok_spec, tok_spec,
                  pl.BlockSpec(pek.shape, const2), pl.BlockSpec(pev.shape, const2),
                  pl.BlockSpec(wk1.shape, const3), pl.BlockSpec(wk2.shape, const2),
                  pl.BlockSpec(wv1.shape, const3), pl.BlockSpec(wv2.shape, const2)],
        out_specs=[pl.BlockSpec((1, G, n_seg, HEAD_DIM), lambda b: (b, 0, 0, 0)),
                   pl.BlockSpec((HEAD_DIM, G * n_seg), lambda b: (0, b))],
        out_shape=[jax.ShapeDtypeStruct((B, G, n_seg, HEAD_DIM), BF16),
                   jax.ShapeDtypeStruct((HEAD_DIM, B * G * n_seg), BF16)],
        compiler_params=pltpu.CompilerParams(
            dimension_semantics=("parallel",), vmem_limit_bytes=VMEM_LIMIT_BYTES),
        name="compress",
    )(kc, vc, pek, pev, wk1, wk2, wv1, wv2)


def _block_diag_w1(w1):
    G = NSA_KV_HEADS
    w = w1.reshape(CMP_BLOCK, HEAD_DIM, CMP_HIDDEN)
    z = jnp.zeros_like(w)
    rows = [jnp.concatenate([w if j == g else z for j in range(G)], axis=2) for g in range(G)]
    return jnp.concatenate(rows, axis=1)


def _nsa_kernel(q_ref, kcmp_ref, vcmp_ref, ks_ref, vs_ref, kw_ref, vw_ref, gate_ref,
                o_ref, m_ref, acc_ref, s0_ref):
    G, R = NSA_KV_HEADS, NSA_GROUP
    n_cp = kcmp_ref.shape[2]
    S = ks_ref.shape[2]
    n_cmp = (S - CMP_BLOCK) // CMP_STRIDE + 1
    n_blk = S // SEL_BLOCK
    qi = pl.program_id(1)
    q0 = qi * TQ
    t_q = q0 + lax.broadcasted_iota(jnp.int32, (1, TQ), 1)
    qs = [jnp.concatenate([q_ref[0, g, r] for r in range(R)], axis=1) for g in range(G)]

    def heads(x):
        return [x[:, r * TQ:(r + 1) * TQ] for r in range(R)]

    def masked_exp(s_heads, mask):
        es = []
        for s in s_heads:
            s = jnp.where(mask, s, NEG_INF)
            es.append(jnp.exp2(s - jnp.max(s, axis=0, keepdims=True)))
        return es

    c_idx = lax.broadcasted_iota(jnp.int32, (n_cp, TQ), 0)
    cmask = ((c_idx * CMP_STRIDE + (CMP_BLOCK - 1)) <= t_q) & (c_idx < n_cmp)
    jj = lax.broadcasted_iota(jnp.int32, (n_blk, n_cp), 0)
    cc = lax.broadcasted_iota(jnp.int32, (n_blk, n_cp), 1)
    overlap = ((cc * CMP_STRIDE < (jj + 1) * SEL_BLOCK) & (cc * CMP_STRIDE + CMP_BLOCK > jj * SEL_BLOCK)
               & (cc < n_cmp)).astype(BF16)
    jb = lax.broadcasted_iota(jnp.int32, (n_blk, TQ), 0)
    jb_f = jb.astype(F32)
    cur = jnp.right_shift(t_q, int(math.log2(SEL_BLOCK)))
    forced = (jb == 0) | (jb == cur) | (jb == cur - 1)
    valid = jb <= cur

    n_sub = TQ // KV_CHUNK
    n_wc = (KV_CHUNK + WINDOW) // KV_CHUNK
    win_c0, win_mask, win_q = [], [], []
    for h in range(n_sub):
        c0 = jnp.maximum(q0 // KV_CHUNK + h - WINDOW // KV_CHUNK, 0)
        kpos = c0 * KV_CHUNK + lax.broadcasted_iota(jnp.int32, (n_wc * KV_CHUNK, KV_CHUNK), 0)
        diff = t_q[:, h * KV_CHUNK:(h + 1) * KV_CHUNK] - kpos
        win_c0.append(c0)
        win_mask.append((diff >= 0) & (diff < WINDOW))
        win_q.append([jnp.concatenate([qs[g][:, r * TQ + h * KV_CHUNK:r * TQ + (h + 1) * KV_CHUNK]
                                       for r in range(R)], axis=1) for g in range(G)])

    def window_keys(g, h):
        return kw_ref[0, g, pl.ds(pl.multiple_of(win_c0[h] * KV_CHUNK, KV_CHUNK), n_wc * KV_CHUNK), :]

    s_cmp = [_dot(kcmp_ref[0, g], qs[g]) for g in range(G)]
    s_win = [[_dot(window_keys(g, h), win_q[h][g]) for h in range(n_sub)] for g in range(G)]
    any_cmp = t_q >= CMP_BLOCK - 1
    p_cmp, o_cmp, o_win = [], [], []
    for g in range(G):
        es = masked_exp(heads(s_cmp[g]), cmask)
        p_cmp.append([e * jnp.where(any_cmp, 1.0 / jnp.sum(e, axis=0, keepdims=True), 0.0) for e in es])

    sels = []
    for g in range(G):
        p_c = p_cmp[g]
        p_sum = p_c[0]
        for r in range(1, R):
            p_sum = p_sum + p_c[r]
        p_hi = p_sum.astype(BF16)
        p_lo = (p_sum - p_hi.astype(F32)).astype(BF16)
        p_slc = _dot(overlap, p_hi) + _dot(overlap, p_lo)
        work = jnp.where(valid & jnp.logical_not(forced), p_slc, -1.0)
        picked = forced
        for _ in range(min(SEL_TOPK, n_blk) - 3):
            best = jnp.max(work, axis=0, keepdims=True)
            first = jnp.min(jnp.where(work == best, jb_f, float(n_blk)), axis=0, keepdims=True)
            hit = jb_f == first
            picked = picked | hit
            work = jnp.where(hit, -2.0, work)
        bias = jnp.where(picked, 0.0, -SEL_BIAS).astype(BF16)
        sels.append(jnp.concatenate([qs[g], jnp.concatenate([bias] * R, axis=1),
                                     jnp.zeros((LANES - HEAD_DIM - n_blk, R * TQ), BF16)], axis=0))

    m_ref[...] = jnp.full(m_ref.shape, NEG_INF, F32)
    acc_ref[...] = jnp.zeros(acc_ref.shape, F32)

    def scores(g, kt):
        return _dot(ks_ref[0, g, pl.ds(pl.multiple_of(kt * TK, TK), TK), :], sels[g])

    def accumulate(g, kt, s, causal):
        if causal is not None:
            s = jnp.concatenate([jnp.where(causal, sh, NEG_INF) for sh in heads(s)], axis=1)
        m_old = m_ref[g]
        m_new = jnp.maximum(m_old, jnp.max(s, axis=0, keepdims=True))
        alpha = jnp.exp2(m_old - m_new)
        p = jnp.exp2(s - m_new)
        acc_ref[g] = alpha * acc_ref[g] + _dot(vs_ref[0, g, kt], p.astype(BF16))
        m_ref[g] = m_new

    assert G == 2
    s0_ref[...] = scores(0, 0)

    for g in range(G):
        o_cmp.append(_dot(vcmp_ref[:, g * n_cp:(g + 1) * n_cp],
                          jnp.concatenate(p_cmp[g], axis=1).astype(BF16)))
    for g in range(G):
        sub = []
        for h in range(n_sub):
            s_heads = [s_win[g][h][:, r * KV_CHUNK:(r + 1) * KV_CHUNK] for r in range(R)]
            e_w = jnp.concatenate(masked_exp(s_heads, win_mask[h]), axis=1).astype(BF16)
            v_w = jnp.concatenate([vw_ref[0, g, win_c0[h] + j] for j in range(n_wc)], axis=1)
            sub.append(_dot(v_w, e_w))
        o_win.append(jnp.concatenate([sub[h][:, r * KV_CHUNK:(r + 1) * KV_CHUNK]
                                      for r in range(R) for h in range(n_sub)], axis=1))

    def interior_tile(kt, carry):
        s1 = scores(1, kt)
        accumulate(0, kt, s0_ref[...], None)
        s0_ref[...] = scores(0, kt + 1)
        accumulate(1, kt, s1, None)
        return carry

    lax.fori_loop(0, qi, interior_tile, 0)
    causal = (q0 + lax.broadcasted_iota(jnp.int32, (TK, TQ), 0)) <= t_q
    s1 = scores(1, qi)
    accumulate(0, qi, s0_ref[...], causal)
    accumulate(1, qi, s1, causal)

    gates = _sigmoid(gate_ref[0].T)
    outs = []
    for g in range(G):
        o_sel = acc_ref[g]
        for r, (oc, os_, ow) in enumerate(zip(heads(o_cmp[g]), heads(o_sel), heads(o_win[g]))):
            g0 = g * HEAD_DIM + r
            g_sel = gates[g0 + R:g0 + R + 1] * (1.0 / os_[HEAD_DIM:HEAD_DIM + 1])
            g_win = gates[g0 + 2 * R:g0 + 2 * R + 1] * (1.0 / ow[HEAD_DIM:HEAD_DIM + 1])
            outs.append(gates[g0:g0 + 1] * oc + g_sel * os_[0:HEAD_DIM] + g_win * ow[0:HEAD_DIM])
    o_ref[0] = jnp.concatenate(outs, axis=0).T.astype(BF16)


def _nsa(q, kcmp, vcmp, ks, vs, kw, vw, gates):
    B, G, R, dh, S = q.shape
    n_cp = kcmp.shape[2]
    assert TQ == TK, "the key sweep treats exactly one tile per query tile as the diagonal"
    k_spec = lambda a: pl.BlockSpec((1,) + a.shape[1:], lambda b, i: (b, 0, 0, 0))
    vt_spec = lambda a: pl.BlockSpec((1,) + a.shape[1:], lambda b, i: (b, 0, 0, 0, 0))
    return pl.pallas_call(
        _nsa_kernel,
        grid=(B, S // TQ),
        in_specs=[
            pl.BlockSpec((1, G, R, dh, TQ), lambda b, i: (b, 0, 0, 0, i)),
            pl.BlockSpec((1, G, n_cp, dh), lambda b, i: (b, 0, 0, 0)),
            pl.BlockSpec((dh, G * n_cp), lambda b, i: (0, b)),
            k_spec(ks), vt_spec(vs), k_spec(kw), vt_spec(vw),
            pl.BlockSpec((1, TQ, LANES), lambda b, i: (b, i, 0)),
        ],
        out_specs=pl.BlockSpec((1, TQ, G * R * dh), lambda b, i: (b, i, 0)),
        out_shape=jax.ShapeDtypeStruct((B, S, G * R * dh), BF16),
        scratch_shapes=[pltpu.VMEM((G, 1, R * TQ), F32), pltpu.VMEM((G, dh + ONES_ROWS, R * TQ), F32),
                        pltpu.VMEM((TK, R * TQ), F32)],
        compiler_params=pltpu.CompilerParams(
            dimension_semantics=("parallel", "arbitrary"), vmem_limit_bytes=VMEM_LIMIT_BYTES),
        name="nsa_attention",
    )(q, kcmp, vcmp, ks, vs, kw, vw, gates)


def _retention_kernel(q_ref, kt_ref, v_ref, g_ref, w_ref, o_ref, state_ref):
    C = RET_CHUNK
    NB = q_ref.shape[0]

    @pl.when(pl.program_id(1) == 0)
    def _():
        state_ref[...] = jnp.zeros(state_ref.shape, F32)

    i_col = lax.broadcasted_iota(jnp.int32, (C, 1), 0)
    i_row = lax.broadcasted_iota(jnp.int32, (1, C), 1)
    d_int = i_col - i_row
    log_gamma = [math.log(1.0 - 2.0 ** (-5.0 - h)) for h in range(RET_HEADS)]
    dmat = [jnp.where(d_int >= 0, jnp.exp(lg * jnp.maximum(d_int, 0).astype(F32)), 0.0) for lg in log_gamma]
    xi = [jnp.exp(lg * (i_col + 1).astype(F32)) for lg in log_gamma]
    zeta = [jnp.exp(lg * (C - 1 - i_row).astype(F32)) for lg in log_gamma]
    gamma_c = [math.exp(lg * C) for lg in log_gamma]

    units = [(n, h) for n in range(NB) for h in range(RET_HEADS)]
    q, kt, v, st = {}, {}, {}, {}
    for n, h in units:
        q[n, h] = q_ref[n, :, h * RET_QK_DIM:(h + 1) * RET_QK_DIM]
        kt[n, h] = kt_ref[n, 0, h * RET_QK_DIM:(h + 1) * RET_QK_DIM, :]
        v[n, h] = v_ref[n, :, h * RET_V_DIM:(h + 1) * RET_V_DIM]
        st[n, h] = state_ref[n, h]
    inner = {u: _dot(q[u], kt[u]) for u in units}
    cross = {u: _dot(q[u], st[u].astype(BF16)) for u in units}
    kv = {u: _dot((kt[u].astype(F32) * zeta[u[1]]).astype(BF16), v[u]) for u in units}
    for n, h in units:
        u = (n, h)
        vsl = slice(h * RET_V_DIM, (h + 1) * RET_V_DIM)
        y = _dot((inner[u] * dmat[h]).astype(BF16), v[u]) + cross[u] * xi[h]
        state_ref[n, h] = gamma_c[h] * st[u] + kv[u]

        mu = jnp.mean(y, axis=-1, keepdims=True)
        yc = y - mu
        var = jnp.mean(yc * yc, axis=-1, keepdims=True)
        yn = yc * lax.rsqrt(var + EPS) * w_ref[:, vsl]
        gate = g_ref[n, :, vsl].astype(F32)
        o_ref[n, :, vsl] = (gate * _sigmoid(gate) * yn).astype(BF16)


def _retention(rq, rkt, rv, rg, w):
    B, S, _ = rq.shape
    C = RET_CHUNK
    nb = math.gcd(RET_BATCH, B)
    q_spec = pl.BlockSpec((nb, C, RET_QK_WIDTH), lambda b, c: (b, c, 0))
    kt_spec = pl.BlockSpec((nb, 1, RET_QK_WIDTH, C), lambda b, c: (b, c, 0, 0))
    v_spec = pl.BlockSpec((nb, C, RET_V_WIDTH), lambda b, c: (b, c, 0))
    return pl.pallas_call(
        _retention_kernel,
        grid=(B // nb, S // C),
        in_specs=[q_spec, kt_spec, v_spec, v_spec, pl.BlockSpec((1, RET_V_WIDTH), lambda b, c: (0, 0))],
        out_specs=v_spec,
        out_shape=jax.ShapeDtypeStruct((B, S, RET_V_WIDTH), BF16),
        scratch_shapes=[pltpu.VMEM((nb, RET_HEADS, RET_QK_DIM, RET_V_DIM), F32)],
        compiler_params=pltpu.CompilerParams(
            dimension_semantics=("parallel", "arbitrary"), vmem_limit_bytes=VMEM_LIMIT_BYTES),
        name="retention",
    )(rq, rkt, rv, rg, w)


def _out_ffn_kernel(x_ref, a_ref, r_ref, wo_ref, ln_ref, wu_ref, wd_ref, o_ref):
    na = a_ref.shape[1]
    mix = _dot(a_ref[...], wo_ref[0:na, :]) + _dot(r_ref[...], wo_ref[na:, :])
    h = x_ref[...] + mix
    ms = jnp.mean(h * h, axis=-1, keepdims=True)
    hn = (h * lax.rsqrt(ms + EPS) * ln_ref[...]).astype(BF16)
    d_ff = wu_ref.shape[1]
    acc = None
    for f in range(d_ff // FF_CHUNK):
        cols = slice(f * FF_CHUNK, (f + 1) * FF_CHUNK)
        u = jnp.maximum(_dot(hn, wu_ref[:, cols]), 0.0)
        d = _dot((u * u).astype(BF16), wd_ref[cols, :])
        acc = d if acc is None else acc + d
    o_ref[...] = h + acc


def _out_ffn(x2, o_nsa, o_ret, w_out, ln2, w_up, w_down):
    N, D = x2.shape
    tm = TM_FFN
    row = lambda i: (i, 0)
    const = lambda i: (0, 0)
    resident = functools.partial(pl.BlockSpec, index_map=const, pipeline_mode=pl.Buffered(1))
    return pl.pallas_call(
        _out_ffn_kernel,
        grid=(N // tm,),
        in_specs=[
            pl.BlockSpec((tm, D), row),
            pl.BlockSpec((tm, o_nsa.shape[1]), row),
            pl.BlockSpec((tm, o_ret.shape[1]), row),
            resident(w_out.shape),
            pl.BlockSpec((1, D), const),
            resident(w_up.shape),
            resident(w_down.shape),
        ],
        out_specs=pl.BlockSpec((tm, D), row),
        out_shape=jax.ShapeDtypeStruct((N, D), F32),
        compiler_params=pltpu.CompilerParams(
            dimension_semantics=("parallel",), vmem_limit_bytes=VMEM_LIMIT_BYTES),
        name="out_ffn",
    )(x2, o_nsa, o_ret, w_out, ln2, w_up, w_down)


def _rope_tables(S):
    half = HEAD_DIM // 2
    inv = ROPE_THETA ** (-jnp.arange(half, dtype=F32) / half)
    ang = jnp.arange(S).astype(F32)[:, None] * inv[None, :]
    cos, sin = jnp.cos(ang), jnp.sin(ang)
    reps = LANES // HEAD_DIM
    cos_h = jnp.concatenate([cos, cos], axis=-1)
    sin_h = jnp.concatenate([-sin, sin], axis=-1)
    return jnp.tile(cos_h, (1, reps)), jnp.tile(sin_h, (1, reps)), cos_h.T, sin_h.T


def _layer(h, ln1_w, w_in, q_norm_w, k_norm_w, cmp_pe_k, cmp_pe_v, cmp_wk1, cmp_wk2,
           cmp_wv1, cmp_wv2, ret_norm_w, w_out, ln2_w, w_up, w_down):
    B, S, D = h.shape
    N = B * S
    G = NSA_KV_HEADS
    x2 = h.reshape(N, D)

    w_perm, w_perm_t = _permute_w_in(w_in)
    cos_l, sin_l, cos_t, sin_t = _rope_tables(S)
    reps = LANES // HEAD_DIM
    qn_t = jnp.broadcast_to(q_norm_w[:, None], (HEAD_DIM, TM_PROJ))
    kn = jnp.tile(k_norm_w, (1, reps))

    (q, kc, vc, ks, vs, kw, vw, gates, rq, rk, rv, rg) = _in_proj(
        x2, ln1_w[None, :], w_perm.astype(BF16), w_perm_t.astype(BF16), cos_l, sin_l, cos_t, sin_t, qn_t, kn, B, S)

    kcmp, vcmp = _compress(
        kc, vc, jnp.tile(cmp_pe_k, (1, G)), jnp.tile(cmp_pe_v, (1, G)),
        _block_diag_w1(cmp_wk1).astype(BF16), cmp_wk2.astype(BF16), _block_diag_w1(cmp_wv1).astype(BF16),
        jnp.pad(cmp_wv2, ((0, 0), (0, LANES - HEAD_DIM))).astype(BF16), B, S)

    o_nsa = _nsa(q, kcmp, vcmp, ks, vs, kw, vw, gates.reshape(B, S, LANES))
    o_ret = _retention(rq.reshape(B, S, -1), rk, rv.reshape(B, S, -1),
                       rg.reshape(B, S, -1), ret_norm_w.reshape(1, RET_V_WIDTH))

    out = _out_ffn(x2, o_nsa.reshape(N, -1), o_ret.reshape(N, -1), w_out.astype(BF16),
                   ln2_w[None, :], w_up.astype(BF16), w_down.astype(BF16))
    return out.reshape(B, S, D)


def kernel(x, ln1_w, w_in, q_norm_w, k_norm_w, cmp_pe_k, cmp_pe_v, cmp_wk1, cmp_wk2, cmp_wv1, cmp_wv2,
           ret_norm_w, w_out, ln2_w, w_up, w_down):
    h = x
    for l in range(ln1_w.shape[0]):
        h = _layer(h, ln1_w[l], w_in[l], q_norm_w[l], k_norm_w[l], cmp_pe_k[l], cmp_pe_v[l],
                   cmp_wk1[l], cmp_wk2[l], cmp_wv1[l], cmp_wv2[l], ret_norm_w[l], w_out[l],
                   ln2_w[l], w_up[l], w_down[l])
    return h
```

```python
import functools
import math

import jax
import jax.numpy as jnp
import numpy as np
from jax import lax
from jax.experimental import pallas as pl
from jax.experimental.pallas import tpu as pltpu

F32 = jnp.float32
BF16 = jnp.bfloat16

NSA_HEADS = 8
NSA_KV_HEADS = 2
NSA_GROUP = NSA_HEADS // NSA_KV_HEADS
HEAD_DIM = 64
CMP_BLOCK = 32
CMP_STRIDE = 16
CMP_HIDDEN = 256
SEL_BLOCK = 64
SEL_TOPK = 8
SEL_FORCED = 3
WINDOW = 256
RET_HEADS = 4
RET_QK_DIM = 64
RET_V_DIM = 128
RET_CHUNK = 128
ROPE_THETA = 10000.0
EPS = 1e-6
NEG_INF = -1.0e30
SEL_BIAS = 2.0 ** 100

NSA_Q_DIM = NSA_HEADS * HEAD_DIM
NSA_KV_DIM = NSA_KV_HEADS * HEAD_DIM
NSA_GATE_DIM = 3 * NSA_HEADS
RET_QK_WIDTH = RET_HEADS * RET_QK_DIM
RET_V_WIDTH = RET_HEADS * RET_V_DIM

LANES = 128
VMEM_LIMIT_BYTES = 56 * 1024 * 1024

TM_PROJ = 512
TQ = 256
TK = 256
KV_CHUNK = 128
ONES_ROWS = 16
TM_FFN = 512
FF_CHUNK = 512
RET_BATCH = 8


def _dot(a, b):
    return jnp.dot(a, b, preferred_element_type=F32)


def _dot_nt(a, b):
    return lax.dot_general(a, b, (((1,), (1,)), ((), ())), preferred_element_type=F32)


def _sigmoid(x):
    return 1.0 / (1.0 + jnp.exp(-x))


_K0 = 0
_V0 = _K0 + 3 * NSA_KV_DIM
_RQ0 = _V0 + NSA_KV_DIM + LANES
_RV0 = _RQ0 + RET_QK_WIDTH
_RG0 = _RV0 + RET_V_WIDTH
_W_IN_COLS = _RG0 + RET_V_WIDTH
_TQ0 = 0
_TV0 = _TQ0 + NSA_Q_DIM
_TRK0 = _TV0 + 2 * NSA_KV_DIM
_WT_ROWS = _TRK0 + RET_QK_WIDTH


def _permute_w_in(w_in):
    sizes = [NSA_Q_DIM] + [NSA_KV_DIM] * 6 + [NSA_GATE_DIM, RET_QK_WIDTH, RET_QK_WIDTH, RET_V_WIDTH, RET_V_WIDTH]
    off = [int(o) for o in np.concatenate([[0], np.cumsum(sizes)])]
    (q, kc, vc, ks, vs, kw, vw, gate, rq, rk, rv, rg) = [w_in[:, off[i]:off[i + 1]] for i in range(12)]
    d = w_in.shape[0]
    gate = gate.reshape(d, 3, NSA_KV_HEADS, NSA_GROUP).transpose(0, 2, 1, 3).reshape(d, NSA_KV_HEADS, 3 * NSA_GROUP)
    gate = jnp.pad(gate, ((0, 0), (0, 0), (0, HEAD_DIM - 3 * NSA_GROUP))).reshape(d, LANES)
    w = jnp.concatenate([kc, ks, kw, vc, gate, rq, rv, rg], axis=1)
    wt = jnp.concatenate([q, vs, vw, rk], axis=1).T
    assert w.shape[1] == _W_IN_COLS and wt.shape[0] == _WT_ROWS
    return w, wt


def _in_proj_kernel(x_ref, ln_ref, w_ref, wt_ref, cos_ref, sin_ref, cost_ref, sint_ref, qnt_ref, kn_ref,
                    q_ref, kc_ref, vc_ref, ks_ref, vs_ref, kw_ref, vw_ref, gate_ref,
                    rq_ref, rk_ref, rv_ref, rg_ref, *, tiles_per_seq):
    x = x_ref[...]
    tm = x.shape[0]
    ms = jnp.mean(x * x, axis=-1, keepdims=True)
    xn = (x * lax.rsqrt(ms + EPS) * ln_ref[...]).astype(BF16)
    cos = cos_ref[...]
    sin = sin_ref[...]
    lane = lax.broadcasted_iota(jnp.int32, (1, LANES), 1)
    low_half = (lane & (HEAD_DIM // 2)) == 0
    first_head = lane < HEAD_DIM

    def proj(a, b):
        return _dot(xn, w_ref[:, a:b])

    def rope(t):
        swapped = jnp.where(low_half, pltpu.roll(t, LANES - HEAD_DIM // 2, 1), pltpu.roll(t, HEAD_DIM // 2, 1))
        return t * cos + swapped * sin

    def head_norm(t, w):
        t2 = t * t
        s0 = jnp.sum(jnp.where(first_head, t2, 0.0), axis=-1, keepdims=True)
        s1 = jnp.sum(jnp.where(first_head, 0.0, t2), axis=-1, keepdims=True)
        msq = jnp.where(first_head, s0, s1) * (1.0 / HEAD_DIM)
        return t * lax.rsqrt(msq + EPS) * w

    def tiles(sec):
        return [sec[:, c * LANES:(c + 1) * LANES] for c in range(sec.shape[1] // LANES)]

    def split_heads(t, fill):
        return [jnp.where(first_head, t, fill), jnp.where(first_head, pltpu.roll(t, HEAD_DIM, 1), fill)]

    def proj_t(a, b):
        return _dot_nt(wt_ref[a:b, :], xn)

    cos_t = cost_ref[...]
    sin_t = sint_ref[...]

    def rope_t(t):
        half = HEAD_DIM // 2
        return t * cos_t + jnp.concatenate([t[half:], t[:half]], axis=0) * sin_t

    q_t = proj_t(_TQ0, _TV0)
    qn_t = qnt_ref[...]
    for h in range(NSA_HEADS):
        t = q_t[h * HEAD_DIM:(h + 1) * HEAD_DIM]
        t = t * lax.rsqrt(jnp.mean(t * t, axis=0, keepdims=True) + EPS) * qn_t
        t = rope_t(t) * (HEAD_DIM ** -0.5 * math.log2(math.e))
        q_ref[0, h // NSA_GROUP, h % NSA_GROUP] = t.astype(BF16)

    def store_values_t(v, out_ref):
        chunk = out_ref.shape[-1]
        ones = jnp.ones((ONES_ROWS, chunk), BF16)
        for c in range(tm // chunk):
            for g in range(NSA_KV_HEADS):
                out_ref[0, g, c, 0:HEAD_DIM, :] = v[g * HEAD_DIM:(g + 1) * HEAD_DIM,
                                                    c * chunk:(c + 1) * chunk].astype(BF16)
                out_ref[0, g, c, HEAD_DIM:HEAD_DIM + ONES_ROWS, :] = ones

    v_t = proj_t(_TV0, _TRK0)
    store_values_t(v_t[0:NSA_KV_DIM], vs_ref)
    store_values_t(v_t[NSA_KV_DIM:2 * NSA_KV_DIM], vw_ref)

    rk_t = proj_t(_TRK0, _WT_ROWS)
    for h in range(RET_HEADS):
        rows = slice(h * RET_QK_DIM, (h + 1) * RET_QK_DIM)
        t = (rope_t(rk_t[rows]) * (RET_QK_DIM ** -0.5)).astype(BF16)
        for j in range(tm // RET_CHUNK):
            rk_ref[0, j, rows, :] = t[:, j * RET_CHUNK:(j + 1) * RET_CHUNK]

    ksec = tiles(proj(_K0, _V0))
    kc = rope(head_norm(ksec[0], kn_ref[0:1, :]))
    ks = rope(head_norm(ksec[1], kn_ref[1:2, :]))
    kw = rope(head_norm(ksec[2], kn_ref[2:3, :])).astype(BF16)
    s0 = (pl.program_id(0) % tiles_per_seq) * tm
    blk = jnp.right_shift(s0 + lax.broadcasted_iota(jnp.int32, (tm, 1), 0), int(math.log2(SEL_BLOCK)))
    onehot = (lane - HEAD_DIM == blk).astype(F32)
    ks_aug = split_heads(ks, onehot)
    kc_ref[...] = kc
    for g in range(NSA_KV_HEADS):
        ks_ref[0, g] = ks_aug[g].astype(BF16)
        kw_ref[0, g] = kw[:, g * HEAD_DIM:(g + 1) * HEAD_DIM]

    vsec = tiles(proj(_V0, _RQ0))
    vc_ref[...] = vsec[0]
    gate_ref[...] = vsec[1]

    for c, t in enumerate(tiles(proj(_RQ0, _RV0))):
        rq_ref[:, c * LANES:(c + 1) * LANES] = rope(t).astype(BF16)
    rv_ref[...] = proj(_RV0, _RG0).astype(BF16)
    rg_ref[...] = proj(_RG0, _W_IN_COLS).astype(BF16)


def _in_proj(x2, ln1, w_perm, w_perm_t, cos, sin, cos_t, sin_t, qn_t, kn, B, S):
    N, D = x2.shape
    tm = TM_PROJ
    spt = S // tm
    G, R, dh = NSA_KV_HEADS, NSA_GROUP, HEAD_DIM

    def row(i):
        return (i, 0)

    def hm(i):
        return (i // spt, 0, i % spt, 0)

    hm_spec = pl.BlockSpec((1, G, tm, dh), hm)
    out_shape = [
        jax.ShapeDtypeStruct((B, G, R, dh, S), BF16),
        jax.ShapeDtypeStruct((N, LANES), F32),
        jax.ShapeDtypeStruct((N, LANES), F32),
        jax.ShapeDtypeStruct((B, G, S, LANES), BF16),
        jax.ShapeDtypeStruct((B, G, S // TK, dh + ONES_ROWS, TK), BF16),
        jax.ShapeDtypeStruct((B, G, S, dh), BF16),
        jax.ShapeDtypeStruct((B, G, S // KV_CHUNK, dh + ONES_ROWS, KV_CHUNK), BF16),
        jax.ShapeDtypeStruct((N, LANES), F32),
        jax.ShapeDtypeStruct((N, RET_QK_WIDTH), BF16),
        jax.ShapeDtypeStruct((B, S // RET_CHUNK, RET_QK_WIDTH, RET_CHUNK), BF16),
        jax.ShapeDtypeStruct((N, RET_V_WIDTH), BF16),
        jax.ShapeDtypeStruct((N, RET_V_WIDTH), BF16),
    ]
    out_specs = [
        pl.BlockSpec((1, G, R, dh, tm), lambda i: (i // spt, 0, 0, 0, i % spt)),
        pl.BlockSpec((tm, LANES), row), pl.BlockSpec((tm, LANES), row), pl.BlockSpec((1, G, tm, LANES), hm),
        pl.BlockSpec((1, G, tm // TK, dh + ONES_ROWS, TK), lambda i: (i // spt, 0, i % spt, 0, 0)),
        hm_spec,
        pl.BlockSpec((1, G, tm // KV_CHUNK, dh + ONES_ROWS, KV_CHUNK), lambda i: (i // spt, 0, i % spt, 0, 0)),
        pl.BlockSpec((tm, LANES), row),
        pl.BlockSpec((tm, RET_QK_WIDTH), row),
        pl.BlockSpec((1, tm // RET_CHUNK, RET_QK_WIDTH, RET_CHUNK), lambda i: (i // spt, i % spt, 0, 0)),
        pl.BlockSpec((tm, RET_V_WIDTH), row),
        pl.BlockSpec((tm, RET_V_WIDTH), row),
    ]
    const = lambda i: (0, 0)
    in_specs = [
        pl.BlockSpec((tm, D), row),
        pl.BlockSpec((1, D), const),
        pl.BlockSpec((D, _W_IN_COLS), const, pipeline_mode=pl.Buffered(1)),
        pl.BlockSpec((_WT_ROWS, D), const, pipeline_mode=pl.Buffered(1)),
        pl.BlockSpec((tm, LANES), lambda i: (i % spt, 0)),
        pl.BlockSpec((tm, LANES), lambda i: (i % spt, 0)),
        pl.BlockSpec((dh, tm), lambda i: (0, i % spt)),
        pl.BlockSpec((dh, tm), lambda i: (0, i % spt)),
        pl.BlockSpec((dh, tm), const),
        pl.BlockSpec((3, LANES), const),
    ]
    return pl.pallas_call(
        functools.partial(_in_proj_kernel, tiles_per_seq=spt),
        grid=(N // tm,),
        in_specs=in_specs,
        out_specs=out_specs,
        out_shape=out_shape,
        compiler_params=pltpu.CompilerParams(
            dimension_semantics=("parallel",), vmem_limit_bytes=VMEM_LIMIT_BYTES),
        name="in_proj",
    )(x2, ln1, w_perm, w_perm_t, cos, sin, cos_t, sin_t, qn_t, kn)


def _compress_kernel(k_ref, v_ref, pek_ref, pev_ref, wk1_ref, wk2_ref, wv1_ref, wv2_ref, ko_ref, vo_ref):
    n_seg = k_ref.shape[0] // CMP_STRIDE
    G = NSA_KV_HEADS

    def run(x_ref, pe_ref, w1_ref):
        first = second = None
        for l in range(CMP_STRIDE):
            x = x_ref[pl.ds(l, n_seg, stride=CMP_STRIDE), :]
            a = _dot((x + pe_ref[l:l + 1, :]).astype(BF16), w1_ref[l])
            b = _dot((x + pe_ref[CMP_STRIDE + l:CMP_STRIDE + l + 1, :]).astype(BF16), w1_ref[CMP_STRIDE + l])
            first = a if first is None else first + a
            second = b if second is None else second + b
        h = first + pltpu.roll(second, n_seg - 1, 0)
        return (h * _sigmoid(h)).astype(BF16)

    hk = run(k_ref, pek_ref, wk1_ref)
    hv = run(v_ref, pev_ref, wv1_ref)
    for g in range(G):
        cols = slice(g * CMP_HIDDEN, (g + 1) * CMP_HIDDEN)
        ko_ref[0, g] = _dot(hk[:, cols], wk2_ref[...]).astype(BF16)
        vo_ref[:, g * n_seg:(g + 1) * n_seg] = _dot(hv[:, cols], wv2_ref[...]).T[0:HEAD_DIM, :].astype(BF16)


def _compress(kc, vc, pek, pev, wk1, wk2, wv1, wv2, B, S):
    G = NSA_KV_HEADS
    n_seg = S // CMP_STRIDE
    width = kc.shape[1]
    const2 = lambda b: (0, 0)
    const3 = lambda b: (0, 0, 0)
    tok_spec = pl.BlockSpec((S, width), lambda b: (b, 0))
    return pl.pallas_call(
        _compress_kernel,
        grid=(B,),
        in_specs=[tok_spec, tok_spec,
                  pl.BlockSpec(pek.shape, const2), pl.BlockSpec(pev.shape, const2),
                  pl.BlockSpec(wk1.shape, const3), pl.BlockSpec(wk2.shape, const2),
                  pl.BlockSpec(wv1.shape, const3), pl.BlockSpec(wv2.shape, const2)],
        out_specs=[pl.BlockSpec((1, G, n_seg, HEAD_DIM), lambda b: (b, 0, 0, 0)),
                   pl.BlockSpec((HEAD_DIM, G * n_seg), lambda b: (0, b))],
        out_shape=[jax.ShapeDtypeStruct((B, G, n_seg, HEAD_DIM), BF16),
                   jax.ShapeDtypeStruct((HEAD_DIM, B * G * n_seg), BF16)],
        compiler_params=pltpu.CompilerParams(
            dimension_semantics=("parallel",), vmem_limit_bytes=VMEM_LIMIT_BYTES),
        name="compress",
    )(kc, vc, pek, pev, wk1, wk2, wv1, wv2)


def _block_diag_w1(w1):
    G = NSA_KV_HEADS
    w = w1.reshape(CMP_BLOCK, HEAD_DIM, CMP_HIDDEN)
    z = jnp.zeros_like(w)
    rows = [jnp.concatenate([w if j == g else z for j in range(G)], axis=2) for g in range(G)]
    return jnp.concatenate(rows, axis=1)


def _nsa_kernel(q_ref, kcmp_ref, vcmp_ref, ks_ref, vs_ref, kw_ref, vw_ref, gate_ref,
                o_ref, m_ref, acc_ref, s0_ref):
    G, R = NSA_KV_HEADS, NSA_GROUP
    n_cp = kcmp_ref.shape[2]
    S = ks_ref.shape[2]
    n_cmp = (S - CMP_BLOCK) // CMP_STRIDE + 1
    n_blk = S // SEL_BLOCK
    qi = pl.program_id(1)
    q0 = qi * TQ
    t_q = q0 + lax.broadcasted_iota(jnp.int32, (1, TQ), 1)
    qs = [jnp.concatenate([q_ref[0, g, r] for r in range(R)], axis=1) for g in range(G)]

    def heads(x):
        return [x[:, r * TQ:(r + 1) * TQ] for r in range(R)]

    def masked_exp(s_heads, mask):
        es = []
        for s in s_heads:
            s = jnp.where(mask, s, NEG_INF)
            es.append(jnp.exp2(s - jnp.max(s, axis=0, keepdims=True)))
        return es

    c_idx = lax.broadcasted_iota(jnp.int32, (n_cp, TQ), 0)
    cmask = ((c_idx * CMP_STRIDE + (CMP_BLOCK - 1)) <= t_q) & (c_idx < n_cmp)
    jj = lax.broadcasted_iota(jnp.int32, (n_blk, n_cp), 0)
    cc = lax.broadcasted_iota(jnp.int32, (n_blk, n_cp), 1)
    overlap = ((cc * CMP_STRIDE < (jj + 1) * SEL_BLOCK) & (cc * CMP_STRIDE + CMP_BLOCK > jj * SEL_BLOCK)
               & (cc < n_cmp)).astype(BF16)
    jb = lax.broadcasted_iota(jnp.int32, (n_blk, TQ), 0)
    jb_f = jb.astype(F32)
    cur = jnp.right_shift(t_q, int(math.log2(SEL_BLOCK)))
    forced = (jb == 0) | (jb == cur) | (jb == cur - 1)
    valid = jb <= cur

    n_sub = TQ // KV_CHUNK
    n_wc = (KV_CHUNK + WINDOW) // KV_CHUNK
    win_c0, win_mask, win_q = [], [], []
    for h in range(n_sub):
        c0 = jnp.maximum(q0 // KV_CHUNK + h - WINDOW // KV_CHUNK, 0)
        kpos = c0 * KV_CHUNK + lax.broadcasted_iota(jnp.int32, (n_wc * KV_CHUNK, KV_CHUNK), 0)
        diff = t_q[:, h * KV_CHUNK:(h + 1) * KV_CHUNK] - kpos
        win_c0.append(c0)
        win_mask.append((diff >= 0) & (diff < WINDOW))
        win_q.append([jnp.concatenate([qs[g][:, r * TQ + h * KV_CHUNK:r * TQ + (h + 1) * KV_CHUNK]
                                       for r in range(R)], axis=1) for g in range(G)])

    def window_keys(g, h):
        return kw_ref[0, g, pl.ds(pl.multiple_of(win_c0[h] * KV_CHUNK, KV_CHUNK), n_wc * KV_CHUNK), :]

    s_cmp = [_dot(kcmp_ref[0, g], qs[g]) for g in range(G)]
    s_win = [[_dot(window_keys(g, h), win_q[h][g]) for h in range(n_sub)] for g in range(G)]
    any_cmp = t_q >= CMP_BLOCK - 1
    p_cmp, o_cmp, o_win = [], [], []
    for g in range(G):
        es = masked_exp(heads(s_cmp[g]), cmask)
        p_cmp.append([e * jnp.where(any_cmp, 1.0 / jnp.sum(e, axis=0, keepdims=True), 0.0) for e in es])

    sels = []
    for g in range(G):
        p_c = p_cmp[g]
        p_sum = p_c[0]
        for r in range(1, R):
            p_sum = p_sum + p_c[r]
        p_hi = p_sum.astype(BF16)
        p_lo = (p_sum - p_hi.astype(F32)).astype(BF16)
        p_slc = _dot(overlap, p_hi) + _dot(overlap, p_lo)
        work = jnp.where(valid & jnp.logical_not(forced), p_slc, -1.0)
        picked = forced
        for _ in range(min(SEL_TOPK, n_blk) - SEL_FORCED):
            best = jnp.max(work, axis=0, keepdims=True)
            first = jnp.min(jnp.where(work == best, jb_f, float(n_blk)), axis=0, keepdims=True)
            hit = jb_f == first
            picked = picked | hit
            work = jnp.where(hit, -2.0, work)
        bias = jnp.where(picked, 0.0, -SEL_BIAS).astype(BF16)
        sels.append(jnp.concatenate([qs[g], jnp.concatenate([bias] * R, axis=1),
                                     jnp.zeros((LANES - HEAD_DIM - n_blk, R * TQ), BF16)], axis=0))

    m_ref[...] = jnp.full(m_ref.shape, NEG_INF, F32)
    acc_ref[...] = jnp.zeros(acc_ref.shape, F32)

    def scores(g, kt):
        return _dot(ks_ref[0, g, pl.ds(pl.multiple_of(kt * TK, TK), TK), :], sels[g])

    def accumulate(g, kt, s, causal):
        if causal is not None:
            s = jnp.concatenate([jnp.where(causal, sh, NEG_INF) for sh in heads(s)], axis=1)
        m_old = m_ref[g]
        m_new = jnp.maximum(m_old, jnp.max(s, axis=0, keepdims=True))
        alpha = jnp.exp2(m_old - m_new)
        p = jnp.exp2(s - m_new)
        acc_ref[g] = alpha * acc_ref[g] + _dot(vs_ref[0, g, kt], p.astype(BF16))
        m_ref[g] = m_new

    assert G == 2
    s0_ref[...] = scores(0, 0)

    for g in range(G):
        o_cmp.append(_dot(vcmp_ref[:, g * n_cp:(g + 1) * n_cp],
                          jnp.concatenate(p_cmp[g], axis=1).astype(BF16)))
    for g in range(G):
        sub = []
        for h in range(n_sub):
            s_heads = [s_win[g][h][:, r * KV_CHUNK:(r + 1) * KV_CHUNK] for r in range(R)]
            e_w = jnp.concatenate(masked_exp(s_heads, win_mask[h]), axis=1).astype(BF16)
            v_w = jnp.concatenate([vw_ref[0, g, win_c0[h] + j] for j in range(n_wc)], axis=1)
            sub.append(_dot(v_w, e_w))
        o_win.append(jnp.concatenate([sub[h][:, r * KV_CHUNK:(r + 1) * KV_CHUNK]
                                      for r in range(R) for h in range(n_sub)], axis=1))

    def interior_tile(kt, carry):
        s1 = scores(1, kt)
        accumulate(0, kt, s0_ref[...], None)
        s0_ref[...] = scores(0, kt + 1)
        accumulate(1, kt, s1, None)
        return carry

    lax.fori_loop(0, qi, interior_tile, 0)
    causal = (q0 + lax.broadcasted_iota(jnp.int32, (TK, TQ), 0)) <= t_q
    s1 = scores(1, qi)
    accumulate(0, qi, s0_ref[...], causal)
    accumulate(1, qi, s1, causal)

    gates = _sigmoid(gate_ref[0].T)
    outs = []
    for g in range(G):
        o_sel = acc_ref[g]
        for r, (oc, os_, ow) in enumerate(zip(heads(o_cmp[g]), heads(o_sel), heads(o_win[g]))):
            g0 = g * HEAD_DIM + r
            g_sel = gates[g0 + R:g0 + R + 1] * (1.0 / os_[HEAD_DIM:HEAD_DIM + 1])
            g_win = gates[g0 + 2 * R:g0 + 2 * R + 1] * (1.0 / ow[HEAD_DIM:HEAD_DIM + 1])
            outs.append(gates[g0:g0 + 1] * oc + g_sel * os_[0:HEAD_DIM] + g_win * ow[0:HEAD_DIM])
    o_ref[0] = jnp.concatenate(outs, axis=0).T.astype(BF16)


def _nsa(q, kcmp, vcmp, ks, vs, kw, vw, gates):
    B, G, R, dh, S = q.shape
    n_cp = kcmp.shape[2]
    assert TQ == TK, "the key sweep treats exactly one tile per query tile as the diagonal"
    k_spec = lambda a: pl.BlockSpec((1,) + a.shape[1:], lambda b, i: (b, 0, 0, 0))
    vt_spec = lambda a: pl.BlockSpec((1,) + a.shape[1:], lambda b, i: (b, 0, 0, 0, 0))
    return pl.pallas_call(
        _nsa_kernel,
        grid=(B, S // TQ),
        in_specs=[
            pl.BlockSpec((1, G, R, dh, TQ), lambda b, i: (b, 0, 0, 0, i)),
            pl.BlockSpec((1, G, n_cp, dh), lambda b, i: (b, 0, 0, 0)),
            pl.BlockSpec((dh, G * n_cp), lambda b, i: (0, b)),
            k_spec(ks), vt_spec(vs), k_spec(kw), vt_spec(vw),
            pl.BlockSpec((1, TQ, LANES), lambda b, i: (b, i, 0)),
        ],
        out_specs=pl.BlockSpec((1, TQ, G * R * dh), lambda b, i: (b, i, 0)),
        out_shape=jax.ShapeDtypeStruct((B, S, G * R * dh), BF16),
        scratch_shapes=[pltpu.VMEM((G, 1, R * TQ), F32), pltpu.VMEM((G, dh + ONES_ROWS, R * TQ), F32),
                        pltpu.VMEM((TK, R * TQ), F32)],
        compiler_params=pltpu.CompilerParams(
            dimension_semantics=("parallel", "arbitrary"), vmem_limit_bytes=VMEM_LIMIT_BYTES),
        name="nsa_attention",
    )(q, kcmp, vcmp, ks, vs, kw, vw, gates)


def _retention_kernel(q_ref, kt_ref, v_ref, g_ref, w_ref, o_ref, state_ref):
    C = RET_CHUNK
    NB = q_ref.shape[0]

    @pl.when(pl.program_id(1) == 0)
    def _():
        state_ref[...] = jnp.zeros(state_ref.shape, F32)

    i_col = lax.broadcasted_iota(jnp.int32, (C, 1), 0)
    i_row = lax.broadcasted_iota(jnp.int32, (1, C), 1)
    d_int = i_col - i_row
    log_gamma = [math.log(1.0 - 2.0 ** (-5.0 - h)) for h in range(RET_HEADS)]
    dmat = [jnp.where(d_int >= 0, jnp.exp(lg * jnp.maximum(d_int, 0).astype(F32)), 0.0) for lg in log_gamma]
    xi = [jnp.exp(lg * (i_col + 1).astype(F32)) for lg in log_gamma]
    zeta = [jnp.exp(lg * (C - 1 - i_row).astype(F32)) for lg in log_gamma]
    gamma_c = [math.exp(lg * C) for lg in log_gamma]

    units = [(n, h) for n in range(NB) for h in range(RET_HEADS)]
    q, kt, v, st = {}, {}, {}, {}
    for n, h in units:
        q[n, h] = q_ref[n, :, h * RET_QK_DIM:(h + 1) * RET_QK_DIM]
        kt[n, h] = kt_ref[n, 0, h * RET_QK_DIM:(h + 1) * RET_QK_DIM, :]
        v[n, h] = v_ref[n, :, h * RET_V_DIM:(h + 1) * RET_V_DIM]
        st[n, h] = state_ref[n, h]
    inner = {u: _dot(q[u], kt[u]) for u in units}
    cross = {u: _dot(q[u], st[u].astype(BF16)) for u in units}
    kv = {u: _dot((kt[u].astype(F32) * zeta[u[1]]).astype(BF16), v[u]) for u in units}
    for n, h in units:
        u = (n, h)
        vsl = slice(h * RET_V_DIM, (h + 1) * RET_V_DIM)
        y = _dot((inner[u] * dmat[h]).astype(BF16), v[u]) + cross[u] * xi[h]
        state_ref[n, h] = gamma_c[h] * st[u] + kv[u]

        mu = jnp.mean(y, axis=-1, keepdims=True)
        yc = y - mu
        var = jnp.mean(yc * yc, axis=-1, keepdims=True)
        yn = yc * lax.rsqrt(var + EPS) * w_ref[:, vsl]
        gate = g_ref[n, :, vsl].astype(F32)
        o_ref[n, :, vsl] = (gate * _sigmoid(gate) * yn).astype(BF16)


def _retention(rq, rkt, rv, rg, w):
    B, S, _ = rq.shape
    C = RET_CHUNK
    nb = math.gcd(RET_BATCH, B)
    q_spec = pl.BlockSpec((nb, C, RET_QK_WIDTH), lambda b, c: (b, c, 0))
    kt_spec = pl.BlockSpec((nb, 1, RET_QK_WIDTH, C), lambda b, c: (b, c, 0, 0))
    v_spec = pl.BlockSpec((nb, C, RET_V_WIDTH), lambda b, c: (b, c, 0))
    return pl.pallas_call(
        _retention_kernel,
        grid=(B // nb, S // C),
        in_specs=[q_spec, kt_spec, v_spec, v_spec, pl.BlockSpec((1, RET_V_WIDTH), lambda b, c: (0, 0))],
        out_specs=v_spec,
        out_shape=jax.ShapeDtypeStruct((B, S, RET_V_WIDTH), BF16),
        scratch_shapes=[pltpu.VMEM((nb, RET_HEADS, RET_QK_DIM, RET_V_DIM), F32)],
        compiler_params=pltpu.CompilerParams(
            dimension_semantics=("parallel", "arbitrary"), vmem_limit_bytes=VMEM_LIMIT_BYTES),
        name="retention",
    )(rq, rkt, rv, rg, w)


def _out_ffn_kernel(x_ref, a_ref, r_ref, wo_ref, ln_ref, wu_ref, wd_ref, o_ref):
    na = a_ref.shape[1]
    mix = _dot(a_ref[...], wo_ref[0:na, :]) + _dot(r_ref[...], wo_ref[na:, :])
    h = x_ref[...] + mix
    ms = jnp.mean(h * h, axis=-1, keepdims=True)
    hn = (h * lax.rsqrt(ms + EPS) * ln_ref[...]).astype(BF16)
    d_ff = wu_ref.shape[1]
    acc = None
    for f in range(d_ff // FF_CHUNK):
        cols = slice(f * FF_CHUNK, (f + 1) * FF_CHUNK)
        u = jnp.maximum(_dot(hn, wu_ref[:, cols]), 0.0)
        d = _dot((u * u).astype(BF16), wd_ref[cols, :])
        acc = d if acc is None else acc + d
    o_ref[...] = h + acc


def _out_ffn(x2, o_nsa, o_ret, w_out, ln2, w_up, w_down):
    N, D = x2.shape
    tm = TM_FFN
    row = lambda i: (i, 0)
    const = lambda i: (0, 0)
    resident = functools.partial(pl.BlockSpec, index_map=const, pipeline_mode=pl.Buffered(1))
    return pl.pallas_call(
        _out_ffn_kernel,
        grid=(N // tm,),
        in_specs=[
            pl.BlockSpec((tm, D), row),
            pl.BlockSpec((tm, o_nsa.shape[1]), row),
            pl.BlockSpec((tm, o_ret.shape[1]), row),
            resident(w_out.shape),
            pl.BlockSpec((1, D), const),
            resident(w_up.shape),
            resident(w_down.shape),
        ],
        out_specs=pl.BlockSpec((tm, D), row),
        out_shape=jax.ShapeDtypeStruct((N, D), F32),
        compiler_params=pltpu.CompilerParams(
            dimension_semantics=("parallel",), vmem_limit_bytes=VMEM_LIMIT_BYTES),
        name="out_ffn",
    )(x2, o_nsa, o_ret, w_out, ln2, w_up, w_down)


def _rope_tables(S):
    half = HEAD_DIM // 2
    inv = ROPE_THETA ** (-jnp.arange(half, dtype=F32) / half)
    ang = jnp.arange(S).astype(F32)[:, None] * inv[None, :]
    cos, sin = jnp.cos(ang), jnp.sin(ang)
    reps = LANES // HEAD_DIM
    cos_h = jnp.concatenate([cos, cos], axis=-1)
    sin_h = jnp.concatenate([-sin, sin], axis=-1)
    return jnp.tile(cos_h, (1, reps)), jnp.tile(sin_h, (1, reps)), cos_h.T, sin_h.T


def _layer(h, ln1_w, w_in, q_norm_w, k_norm_w, cmp_pe_k, cmp_pe_v, cmp_wk1, cmp_wk2,
           cmp_wv1, cmp_wv2, ret_norm_w, w_out, ln2_w, w_up, w_down):
    B, S, D = h.shape
    N = B * S
    G = NSA_KV_HEADS
    x2 = h.reshape(N, D)

    w_perm, w_perm_t = _permute_w_in(w_in)
    cos_l, sin_l, cos_t, sin_t = _rope_tables(S)
    reps = LANES // HEAD_DIM
    qn_t = jnp.broadcast_to(q_norm_w[:, None], (HEAD_DIM, TM_PROJ))
    kn = jnp.tile(k_norm_w, (1, reps))

    (q, kc, vc, ks, vs, kw, vw, gates, rq, rk, rv, rg) = _in_proj(
        x2, ln1_w[None, :], w_perm.astype(BF16), w_perm_t.astype(BF16), cos_l, sin_l, cos_t, sin_t, qn_t, kn, B, S)

    kcmp, vcmp = _compress(
        kc, vc, jnp.tile(cmp_pe_k, (1, G)), jnp.tile(cmp_pe_v, (1, G)),
        _block_diag_w1(cmp_wk1).astype(BF16), cmp_wk2.astype(BF16), _block_diag_w1(cmp_wv1).astype(BF16),
        jnp.pad(cmp_wv2, ((0, 0), (0, LANES - HEAD_DIM))).astype(BF16), B, S)

    o_nsa = _nsa(q, kcmp, vcmp, ks, vs, kw, vw, gates.reshape(B, S, LANES))
    o_ret = _retention(rq.reshape(B, S, -1), rk, rv.reshape(B, S, -1),
                       rg.reshape(B, S, -1), ret_norm_w.reshape(1, RET_V_WIDTH))

    out = _out_ffn(x2, o_nsa.reshape(N, -1), o_ret.reshape(N, -1), w_out.astype(BF16),
                   ln2_w[None, :], w_up.astype(BF16), w_down.astype(BF16))
    return out.reshape(B, S, D)


def kernel(x, ln1_w, w_in, q_norm_w, k_norm_w, cmp_pe_k, cmp_pe_v, cmp_wk1, cmp_wk2, cmp_wv1, cmp_wv2,
           ret_norm_w, w_out, ln2_w, w_up, w_down):
    h = x
    for l in range(ln1_w.shape[0]):
        h = _layer(h, ln1_w[l], w_in[l], q_norm_w[l], k_norm_w[l], cmp_pe_k[l], cmp_pe_v[l],
                   cmp_wk1[l], cmp_wk2[l], cmp_wv1[l], cmp_wv2[l], ret_norm_w[l], w_out[l],
                   ln2_w[l], w_up[l], w_down[l])
    return h
```

```python
import functools
import math

import jax
import jax.numpy as jnp
import numpy as np
from jax import lax
from jax.experimental import pallas as pl
from jax.experimental.pallas import tpu as pltpu

F32 = jnp.float32
BF16 = jnp.bfloat16

NSA_HEADS = 8
NSA_KV_HEADS = 2
NSA_GROUP = NSA_HEADS // NSA_KV_HEADS
HEAD_DIM = 64
CMP_BLOCK = 32
CMP_STRIDE = 16
CMP_HIDDEN = 256
SEL_BLOCK = 64
SEL_TOPK = 8
SEL_FORCED = 3
WINDOW = 256
RET_HEADS = 4
RET_QK_DIM = 64
RET_V_DIM = 128
RET_CHUNK = 128
ROPE_THETA = 10000.0
EPS = 1e-6
NEG_INF = -1.0e30
SEL_BIAS = 2.0 ** 100

NSA_Q_DIM = NSA_HEADS * HEAD_DIM
NSA_KV_DIM = NSA_KV_HEADS * HEAD_DIM
NSA_GATE_DIM = 3 * NSA_HEADS
RET_QK_WIDTH = RET_HEADS * RET_QK_DIM
RET_V_WIDTH = RET_HEADS * RET_V_DIM

LANES = 128
VMEM_LIMIT_BYTES = 56 * 1024 * 1024

TM_PROJ = 512
TQ = 256
TK = 256
KV_CHUNK = 128
ONES_ROWS = 16
TM_FFN = 512
FF_CHUNK = 512
RET_BATCH = 8


def _dot(a, b):
    return jnp.dot(a, b, preferred_element_type=F32)


def _dot_nt(a, b):
    return lax.dot_general(a, b, (((1,), (1,)), ((), ())), preferred_element_type=F32)


def _sigmoid(x):
    return 1.0 / (1.0 + jnp.exp(-x))


_K0 = 0
_V0 = _K0 + 3 * NSA_KV_DIM
_RQ0 = _V0 + NSA_KV_DIM + LANES
_RV0 = _RQ0 + RET_QK_WIDTH
_RG0 = _RV0 + RET_V_WIDTH
_W_IN_COLS = _RG0 + RET_V_WIDTH
_TQ0 = 0
_TV0 = _TQ0 + NSA_Q_DIM
_TRK0 = _TV0 + 2 * NSA_KV_DIM
_WT_ROWS = _TRK0 + RET_QK_WIDTH


def _permute_w_in(w_in):
    sizes = [NSA_Q_DIM] + [NSA_KV_DIM] * 6 + [NSA_GATE_DIM, RET_QK_WIDTH, RET_QK_WIDTH, RET_V_WIDTH, RET_V_WIDTH]
    off = [int(o) for o in np.concatenate([[0], np.cumsum(sizes)])]
    (q, kc, vc, ks, vs, kw, vw, gate, rq, rk, rv, rg) = [w_in[:, off[i]:off[i + 1]] for i in range(12)]
    d = w_in.shape[0]
    gate = gate.reshape(d, 3, NSA_KV_HEADS, NSA_GROUP).transpose(0, 2, 1, 3).reshape(d, NSA_KV_HEADS, 3 * NSA_GROUP)
    gate = jnp.pad(gate, ((0, 0), (0, 0), (0, HEAD_DIM - 3 * NSA_GROUP))).reshape(d, LANES)
    w = jnp.concatenate([kc, ks, kw, vc, gate, rq, rv, rg], axis=1)
    wt = jnp.concatenate([q, vs, vw, rk], axis=1).T
    assert w.shape[1] == _W_IN_COLS and wt.shape[0] == _WT_ROWS
    return w, wt


def _in_proj_kernel(x_ref, ln_ref, w_ref, wt_ref, cos_ref, sin_ref, cost_ref, sint_ref, qnt_ref, kn_ref,
                    q_ref, kc_ref, vc_ref, ks_ref, vs_ref, kw_ref, vw_ref, gate_ref,
                    rq_ref, rk_ref, rv_ref, rg_ref, *, tiles_per_seq):
    x = x_ref[...]
    tm = x.shape[0]
    ms = jnp.mean(x * x, axis=-1, keepdims=True)
    xn = (x * lax.rsqrt(ms + EPS) * ln_ref[...]).astype(BF16)
    cos = cos_ref[...]
    sin = sin_ref[...]
    lane = lax.broadcasted_iota(jnp.int32, (1, LANES), 1)
    low_half = (lane & (HEAD_DIM // 2)) == 0
    first_head = lane < HEAD_DIM

    def proj(a, b):
        return _dot(xn, w_ref[:, a:b])

    def rope(t):
        swapped = jnp.where(low_half, pltpu.roll(t, LANES - HEAD_DIM // 2, 1), pltpu.roll(t, HEAD_DIM // 2, 1))
        return t * cos + swapped * sin

    def head_norm(t, w):
        t2 = t * t
        s0 = jnp.sum(jnp.where(first_head, t2, 0.0), axis=-1, keepdims=True)
        s1 = jnp.sum(jnp.where(first_head, 0.0, t2), axis=-1, keepdims=True)
        msq = jnp.where(first_head, s0, s1) * (1.0 / HEAD_DIM)
        return t * lax.rsqrt(msq + EPS) * w

    def tiles(sec):
        return [sec[:, c * LANES:(c + 1) * LANES] for c in range(sec.shape[1] // LANES)]

    def split_heads(t, fill):
        return [jnp.where(first_head, t, fill), jnp.where(first_head, pltpu.roll(t, HEAD_DIM, 1), fill)]

    def proj_t(a, b):
        return _dot_nt(wt_ref[a:b, :], xn)

    cos_t = cost_ref[...]
    sin_t = sint_ref[...]

    def rope_t(t):
        half = HEAD_DIM // 2
        return t * cos_t + jnp.concatenate([t[half:], t[:half]], axis=0) * sin_t

    q_t = proj_t(_TQ0, _TV0)
    qn_t = qnt_ref[...]
    for h in range(NSA_HEADS):
        t = q_t[h * HEAD_DIM:(h + 1) * HEAD_DIM]
        t = t * lax.rsqrt(jnp.mean(t * t, axis=0, keepdims=True) + EPS) * qn_t
        t = rope_t(t) * (HEAD_DIM ** -0.5 * math.log2(math.e))
        q_ref[0, h // NSA_GROUP, h % NSA_GROUP] = t.astype(BF16)

    def store_values_t(v, out_ref):
        chunk = out_ref.shape[-1]
        ones = jnp.ones((ONES_ROWS, chunk), BF16)
        for c in range(tm // chunk):
            for g in range(NSA_KV_HEADS):
                out_ref[0, g, c, 0:HEAD_DIM, :] = v[g * HEAD_DIM:(g + 1) * HEAD_DIM,
                                                    c * chunk:(c + 1) * chunk].astype(BF16)
                out_ref[0, g, c, HEAD_DIM:HEAD_DIM + ONES_ROWS, :] = ones

    v_t = proj_t(_TV0, _TRK0)
    store_values_t(v_t[0:NSA_KV_DIM], vs_ref)
    store_values_t(v_t[NSA_KV_DIM:2 * NSA_KV_DIM], vw_ref)

    rk_t = proj_t(_TRK0, _WT_ROWS)
    for h in range(RET_HEADS):
        rows = slice(h * RET_QK_DIM, (h + 1) * RET_QK_DIM)
        t = (rope_t(rk_t[rows]) * (RET_QK_DIM ** -0.5)).astype(BF16)
        for j in range(tm // RET_CHUNK):
            rk_ref[0, j, rows, :] = t[:, j * RET_CHUNK:(j + 1) * RET_CHUNK]

    ksec = tiles(proj(_K0, _V0))
    kc = rope(head_norm(ksec[0], kn_ref[0:1, :]))
    ks = rope(head_norm(ksec[1], kn_ref[1:2, :]))
    kw = rope(head_norm(ksec[2], kn_ref[2:3, :])).astype(BF16)
    s0 = (pl.program_id(0) % tiles_per_seq) * tm
    blk = jnp.right_shift(s0 + lax.broadcasted_iota(jnp.int32, (tm, 1), 0), int(math.log2(SEL_BLOCK)))
    onehot = (lane - HEAD_DIM == blk).astype(F32)
    ks_aug = split_heads(ks, onehot)
    kc_ref[...] = kc
    for g in range(NSA_KV_HEADS):
        ks_ref[0, g] = ks_aug[g].astype(BF16)
        kw_ref[0, g] = kw[:, g * HEAD_DIM:(g + 1) * HEAD_DIM]

    vsec = tiles(proj(_V0, _RQ0))
    vc_ref[...] = vsec[0]
    gate_ref[...] = vsec[1]

    for c, t in enumerate(tiles(proj(_RQ0, _RV0))):
        rq_ref[:, c * LANES:(c + 1) * LANES] = rope(t).astype(BF16)
    rv_ref[...] = proj(_RV0, _RG0).astype(BF16)
    rg_ref[...] = proj(_RG0, _W_IN_COLS).astype(BF16)


def _in_proj(x2, ln1, w_perm, w_perm_t, cos, sin, cos_t, sin_t, qn_t, kn, B, S):
    N, D = x2.shape
    tm = TM_PROJ
    spt = S // tm
    G, R, dh = NSA_KV_HEADS, NSA_GROUP, HEAD_DIM

    def row(i):
        return (i, 0)

    def hm(i):
        return (i // spt, 0, i % spt, 0)

    hm_spec = pl.BlockSpec((1, G, tm, dh), hm)
    out_shape = [
        jax.ShapeDtypeStruct((B, G, R, dh, S), BF16),
        jax.ShapeDtypeStruct((N, LANES), F32),
        jax.ShapeDtypeStruct((N, LANES), F32),
        jax.ShapeDtypeStruct((B, G, S, LANES), BF16),
        jax.ShapeDtypeStruct((B, G, S // TK, dh + ONES_ROWS, TK), BF16),
        jax.ShapeDtypeStruct((B, G, S, dh), BF16),
        jax.ShapeDtypeStruct((B, G, S // KV_CHUNK, dh + ONES_ROWS, KV_CHUNK), BF16),
        jax.ShapeDtypeStruct((N, LANES), F32),
        jax.ShapeDtypeStruct((N, RET_QK_WIDTH), BF16),
        jax.ShapeDtypeStruct((B, S // RET_CHUNK, RET_QK_WIDTH, RET_CHUNK), BF16),
        jax.ShapeDtypeStruct((N, RET_V_WIDTH), BF16),
        jax.ShapeDtypeStruct((N, RET_V_WIDTH), BF16),
    ]
    out_specs = [
        pl.BlockSpec((1, G, R, dh, tm), lambda i: (i // spt, 0, 0, 0, i % spt)),
        pl.BlockSpec((tm, LANES), row), pl.BlockSpec((tm, LANES), row), pl.BlockSpec((1, G, tm, LANES), hm),
        pl.BlockSpec((1, G, tm // TK, dh + ONES_ROWS, TK), lambda i: (i // spt, 0, i % spt, 0, 0)),
        hm_spec,
        pl.BlockSpec((1, G, tm // KV_CHUNK, dh + ONES_ROWS, KV_CHUNK), lambda i: (i // spt, 0, i % spt, 0, 0)),
        pl.BlockSpec((tm, LANES), row),
        pl.BlockSpec((tm, RET_QK_WIDTH), row),
        pl.BlockSpec((1, tm // RET_CHUNK, RET_QK_WIDTH, RET_CHUNK), lambda i: (i // spt, i % spt, 0, 0)),
        pl.BlockSpec((tm, RET_V_WIDTH), row),
        pl.BlockSpec((tm, RET_V_WIDTH), row),
    ]
    const = lambda i: (0, 0)
    in_specs = [
        pl.BlockSpec((tm, D), row),
        pl.BlockSpec((1, D), const),
        pl.BlockSpec((D, _W_IN_COLS), const, pipeline_mode=pl.Buffered(1)),
        pl.BlockSpec((_WT_ROWS, D), const, pipeline_mode=pl.Buffered(1)),
        pl.BlockSpec((tm, LANES), lambda i: (i % spt, 0)),
        pl.BlockSpec((tm, LANES), lambda i: (i % spt, 0)),
        pl.BlockSpec((dh, tm), lambda i: (0, i % spt)),
        pl.BlockSpec((dh, tm), lambda i: (0, i % spt)),
        pl.BlockSpec((dh, tm), const),
        pl.BlockSpec((3, LANES), const),
    ]
    return pl.pallas_call(
        functools.partial(_in_proj_kernel, tiles_per_seq=spt),
        grid=(N // tm,),
        in_specs=in_specs,
        out_specs=out_specs,
        out_shape=out_shape,
        compiler_params=pltpu.CompilerParams(
            dimension_semantics=("parallel",), vmem_limit_bytes=VMEM_LIMIT_BYTES),
        name="in_proj",
    )(x2, ln1, w_perm, w_perm_t, cos, sin, cos_t, sin_t, qn_t, kn)


def _compress_kernel(k_ref, v_ref, pek_ref, pev_ref, wk1_ref, wk2_ref, wv1_ref, wv2_ref, ko_ref, vo_ref):
    n_seg = k_ref.shape[0] // CMP_STRIDE
    G = NSA_KV_HEADS

    def run(x_ref, pe_ref, w1_ref):
        first = second = None
        for l in range(CMP_STRIDE):
            x = x_ref[pl.ds(l, n_seg, stride=CMP_STRIDE), :]
            a = _dot((x + pe_ref[l:l + 1, :]).astype(BF16), w1_ref[l])
            b = _dot((x + pe_ref[CMP_STRIDE + l:CMP_STRIDE + l + 1, :]).astype(BF16), w1_ref[CMP_STRIDE + l])
            first = a if first is None else first + a
            second = b if second is None else second + b
        h = first + pltpu.roll(second, n_seg - 1, 0)
        return (h * _sigmoid(h)).astype(BF16)

    hk = run(k_ref, pek_ref, wk1_ref)
    hv = run(v_ref, pev_ref, wv1_ref)
    for g in range(G):
        cols = slice(g * CMP_HIDDEN, (g + 1) * CMP_HIDDEN)
        ko_ref[0, g] = _dot(hk[:, cols], wk2_ref[...]).astype(BF16)
        vo_ref[:, g * n_seg:(g + 1) * n_seg] = _dot(hv[:, cols], wv2_ref[...]).T[0:HEAD_DIM, :].astype(BF16)


def _compress(kc, vc, pek, pev, wk1, wk2, wv1, wv2, B, S):
    G = NSA_KV_HEADS
    n_seg = S // CMP_STRIDE
    width = kc.shape[1]
    const2 = lambda b: (0, 0)
    const3 = lambda b: (0, 0, 0)
    tok_spec = pl.BlockSpec((S, width), lambda b: (b, 0))
    return pl.pallas_call(
        _compress_kernel,
        grid=(B,),
        in_specs=[tok_spec, tok_spec,
                  pl.BlockSpec(pek.shape, const2), pl.BlockSpec(pev.shape, const2),
                  pl.BlockSpec(wk1.shape, const3), pl.BlockSpec(wk2.shape, const2),
                  pl.BlockSpec(wv1.shape, const3), pl.BlockSpec(wv2.shape, const2)],
        out_specs=[pl.BlockSpec((1, G, n_seg, HEAD_DIM), lambda b: (b, 0, 0, 0)),
                   pl.BlockSpec((HEAD_DIM, G * n_seg), lambda b: (0, b))],
        out_shape=[jax.ShapeDtypeStruct((B, G, n_seg, HEAD_DIM), BF16),
                   jax.ShapeDtypeStruct((HEAD_DIM, B * G * n_seg), BF16)],
        compiler_params=pltpu.CompilerParams(
            dimension_semantics=("parallel",), vmem_limit_bytes=VMEM_LIMIT_BYTES),
        name="compress",
    )(kc, vc, pek, pev, wk1, wk2, wv1, wv2)


def _block_diag_w1(w1):
    G = NSA_KV_HEADS
    w = w1.reshape(CMP_BLOCK, HEAD_DIM, CMP_HIDDEN)
    z = jnp.zeros_like(w)
    rows = [jnp.concatenate([w if j == g else z for j in range(G)], axis=2) for g in range(G)]
    return jnp.concatenate(rows, axis=1)


def _nsa_kernel(q_ref, kcmp_ref, vcmp_ref, ks_ref, vs_ref, kw_ref, vw_ref, gate_ref,
                o_ref, m_ref, acc_ref, s0_ref):
    G, R = NSA_KV_HEADS, NSA_GROUP
    n_cp = kcmp_ref.shape[2]
    S = ks_ref.shape[2]
    n_cmp = (S - CMP_BLOCK) // CMP_STRIDE + 1
    n_blk = S // SEL_BLOCK
    qi = pl.program_id(1)
    q0 = qi * TQ
    t_q = q0 + lax.broadcasted_iota(jnp.int32, (1, TQ), 1)
    qs = [jnp.concatenate([q_ref[0, g, r] for r in range(R)], axis=1) for g in range(G)]

    def heads(x):
        return [x[:, r * TQ:(r + 1) * TQ] for r in range(R)]

    def masked_exp(s_heads, mask):
        es = []
        for s in s_heads:
            s = jnp.where(mask, s, NEG_INF)
            es.append(jnp.exp2(s - jnp.max(s, axis=0, keepdims=True)))
        return es

    c_idx = lax.broadcasted_iota(jnp.int32, (n_cp, TQ), 0)
    cmask = ((c_idx * CMP_STRIDE + (CMP_BLOCK - 1)) <= t_q) & (c_idx < n_cmp)
    jj = lax.broadcasted_iota(jnp.int32, (n_blk, n_cp), 0)
    cc = lax.broadcasted_iota(jnp.int32, (n_blk, n_cp), 1)
    overlap = ((cc * CMP_STRIDE < (jj + 1) * SEL_BLOCK) & (cc * CMP_STRIDE + CMP_BLOCK > jj * SEL_BLOCK)
               & (cc < n_cmp)).astype(BF16)
    jb = lax.broadcasted_iota(jnp.int32, (n_blk, TQ), 0)
    jb_f = jb.astype(F32)
    cur = jnp.right_shift(t_q, int(math.log2(SEL_BLOCK)))
    forced = (jb == 0) | (jb == cur) | (jb == cur - 1)
    valid = jb <= cur

    n_sub = TQ // KV_CHUNK
    n_wc = (KV_CHUNK + WINDOW) // KV_CHUNK
    win_c0, win_mask, win_q = [], [], []
    for h in range(n_sub):
        c0 = jnp.maximum(q0 // KV_CHUNK + h - WINDOW // KV_CHUNK, 0)
        kpos = c0 * KV_CHUNK + lax.broadcasted_iota(jnp.int32, (n_wc * KV_CHUNK, KV_CHUNK), 0)
        diff = t_q[:, h * KV_CHUNK:(h + 1) * KV_CHUNK] - kpos
        win_c0.append(c0)
        win_mask.append((diff >= 0) & (diff < WINDOW))
        win_q.append([jnp.concatenate([qs[g][:, r * TQ + h * KV_CHUNK:r * TQ + (h + 1) * KV_CHUNK]
                                       for r in range(R)], axis=1) for g in range(G)])

    def window_keys(g, h):
        return kw_ref[0, g, pl.ds(pl.multiple_of(win_c0[h] * KV_CHUNK, KV_CHUNK), n_wc * KV_CHUNK), :]

    s_cmp = [_dot(kcmp_ref[0, g], qs[g]) for g in range(G)]
    s_win = [[_dot(window_keys(g, h), win_q[h][g]) for h in range(n_sub)] for g in range(G)]
    any_cmp = t_q >= CMP_BLOCK - 1
    p_cmp, o_cmp, o_win = [], [], []
    for g in range(G):
        es = masked_exp(heads(s_cmp[g]), cmask)
        p_cmp.append([e * jnp.where(any_cmp, 1.0 / jnp.sum(e, axis=0, keepdims=True), 0.0) for e in es])

    sels = []
    for g in range(G):
        p_c = p_cmp[g]
        p_sum = p_c[0]
        for r in range(1, R):
            p_sum = p_sum + p_c[r]
        p_hi = p_sum.astype(BF16)
        p_lo = (p_sum - p_hi.astype(F32)).astype(BF16)
        p_slc = _dot(overlap, p_hi) + _dot(overlap, p_lo)
        work = jnp.where(valid & jnp.logical_not(forced), p_slc, -1.0)
        picked = forced
        for _ in range(min(SEL_TOPK, n_blk) - SEL_FORCED):
            best = jnp.max(work, axis=0, keepdims=True)
            first = jnp.min(jnp.where(work == best, jb_f, float(n_blk)), axis=0, keepdims=True)
            hit = jb_f == first
            picked = picked | hit
            work = jnp.where(hit, -2.0, work)
        bias = jnp.where(picked, 0.0, -SEL_BIAS).astype(BF16)
        sels.append(jnp.concatenate([qs[g], jnp.concatenate([bias] * R, axis=1),
                                     jnp.zeros((LANES - HEAD_DIM - n_blk, R * TQ), BF16)], axis=0))

    m_ref[...] = jnp.full(m_ref.shape, NEG_INF, F32)
    acc_ref[...] = jnp.zeros(acc_ref.shape, F32)

    def scores(g, kt):
        return _dot(ks_ref[0, g, pl.ds(pl.multiple_of(kt * TK, TK), TK), :], sels[g])

    def accumulate(g, kt, s, causal):
        if causal is not None:
            s = jnp.concatenate([jnp.where(causal, sh, NEG_INF) for sh in heads(s)], axis=1)
        m_old = m_ref[g]
        m_new = jnp.maximum(m_old, jnp.max(s, axis=0, keepdims=True))
        alpha = jnp.exp2(m_old - m_new)
        p = jnp.exp2(s - m_new)
        acc_ref[g] = alpha * acc_ref[g] + _dot(vs_ref[0, g, kt], p.astype(BF16))
        m_ref[g] = m_new

    assert G == 2
    s0_ref[...] = scores(0, 0)

    for g in range(G):
        o_cmp.append(_dot(vcmp_ref[:, g * n_cp:(g + 1) * n_cp],
                          jnp.concatenate(p_cmp[g], axis=1).astype(BF16)))
    for g in range(G):
        sub = []
        for h in range(n_sub):
            s_heads = [s_win[g][h][:, r * KV_CHUNK:(r + 1) * KV_CHUNK] for r in range(R)]
            e_w = jnp.concatenate(masked_exp(s_heads, win_mask[h]), axis=1).astype(BF16)
            v_w = jnp.concatenate([vw_ref[0, g, win_c0[h] + j] for j in range(n_wc)], axis=1)
            sub.append(_dot(v_w, e_w))
        o_win.append(jnp.concatenate([sub[h][:, r * KV_CHUNK:(r + 1) * KV_CHUNK]
                                      for r in range(R) for h in range(n_sub)], axis=1))

    def interior_tile(kt, carry):
        s1 = scores(1, kt)
        accumulate(0, kt, s0_ref[...], None)
        s0_ref[...] = scores(0, kt + 1)
        accumulate(1, kt, s1, None)
        return carry

    lax.fori_loop(0, qi, interior_tile, 0)
    causal = (q0 + lax.broadcasted_iota(jnp.int32, (TK, TQ), 0)) <= t_q
    s1 = scores(1, qi)
    accumulate(0, qi, s0_ref[...], causal)
    accumulate(1, qi, s1, causal)

    gates = _sigmoid(gate_ref[0].T)
    outs = []
    for g in range(G):
        o_sel = acc_ref[g]
        for r, (oc, os_, ow) in enumerate(zip(heads(o_cmp[g]), heads(o_sel), heads(o_win[g]))):
            g0 = g * HEAD_DIM + r
            g_sel = gates[g0 + R:g0 + R + 1] * (1.0 / os_[HEAD_DIM:HEAD_DIM + 1])
            g_win = gates[g0 + 2 * R:g0 + 2 * R + 1] * (1.0 / ow[HEAD_DIM:HEAD_DIM + 1])
            outs.append(gates[g0:g0 + 1] * oc + g_sel * os_[0:HEAD_DIM] + g_win * ow[0:HEAD_DIM])
    o_ref[0] = jnp.concatenate(outs, axis=0).T.astype(BF16)


def _nsa(q, kcmp, vcmp, ks, vs, kw, vw, gates):
    B, G, R, dh, S = q.shape
    n_cp = kcmp.shape[2]
    assert TQ == TK, "the key sweep treats exactly one tile per query tile as the diagonal"
    k_spec = lambda a: pl.BlockSpec((1,) + a.shape[1:], lambda b, i: (b, 0, 0, 0))
    vt_spec = lambda a: pl.BlockSpec((1,) + a.shape[1:], lambda b, i: (b, 0, 0, 0, 0))
    return pl.pallas_call(
        _nsa_kernel,
        grid=(B, S // TQ),
        in_specs=[
            pl.BlockSpec((1, G, R, dh, TQ), lambda b, i: (b, 0, 0, 0, i)),
            pl.BlockSpec((1, G, n_cp, dh), lambda b, i: (b, 0, 0, 0)),
            pl.BlockSpec((dh, G * n_cp), lambda b, i: (0, b)),
            k_spec(ks), vt_spec(vs), k_spec(kw), vt_spec(vw),
            pl.BlockSpec((1, TQ, LANES), lambda b, i: (b, i, 0)),
        ],
        out_specs=pl.BlockSpec((1, TQ, G * R * dh), lambda b, i: (b, i, 0)),
        out_shape=jax.ShapeDtypeStruct((B, S, G * R * dh), BF16),
        scratch_shapes=[pltpu.VMEM((G, 1, R * TQ), F32), pltpu.VMEM((G, dh + ONES_ROWS, R * TQ), F32),
                        pltpu.VMEM((TK, R * TQ), F32)],
        compiler_params=pltpu.CompilerParams(
            dimension_semantics=("parallel", "arbitrary"), vmem_limit_bytes=VMEM_LIMIT_BYTES),
        name="nsa_attention",
    )(q, kcmp, vcmp, ks, vs, kw, vw, gates)


def _retention_kernel(q_ref, kt_ref, v_ref, o_ref, state_ref):
    C = RET_CHUNK
    NB = q_ref.shape[0]

    @pl.when(pl.program_id(1) == 0)
    def _():
        state_ref[...] = jnp.zeros(state_ref.shape, F32)

    i_col = lax.broadcasted_iota(jnp.int32, (C, 1), 0)
    i_row = lax.broadcasted_iota(jnp.int32, (1, C), 1)
    d_int = i_col - i_row
    log_gamma = [math.log(1.0 - 2.0 ** (-5.0 - h)) for h in range(RET_HEADS)]
    dmat = [jnp.where(d_int >= 0, jnp.exp(lg * jnp.maximum(d_int, 0).astype(F32)), 0.0) for lg in log_gamma]
    xi = [jnp.exp(lg * (i_col + 1).astype(F32)) for lg in log_gamma]
    zeta = [jnp.exp(lg * (C - 1 - i_row).astype(F32)) for lg in log_gamma]
    gamma_c = [math.exp(lg * C) for lg in log_gamma]

    units = [(n, h) for n in range(NB) for h in range(RET_HEADS)]
    q, kt, v, st = {}, {}, {}, {}
    for n, h in units:
        q[n, h] = q_ref[n, :, h * RET_QK_DIM:(h + 1) * RET_QK_DIM]
        kt[n, h] = kt_ref[n, 0, h * RET_QK_DIM:(h + 1) * RET_QK_DIM, :]
        v[n, h] = v_ref[n, :, h * RET_V_DIM:(h + 1) * RET_V_DIM]
        st[n, h] = state_ref[n, h]
    inner = {u: _dot(q[u], kt[u]) for u in units}
    cross = {u: _dot(q[u], st[u].astype(BF16)) for u in units}
    kv = {u: _dot((kt[u].astype(F32) * zeta[u[1]]).astype(BF16), v[u]) for u in units}
    for n, h in units:
        u = (n, h)
        vsl = slice(h * RET_V_DIM, (h + 1) * RET_V_DIM)
        o_ref[n, :, vsl] = _dot((inner[u] * dmat[h]).astype(BF16), v[u]) + cross[u] * xi[h]
        state_ref[n, h] = gamma_c[h] * st[u] + kv[u]


def _retention(rq, rkt, rv):
    B, S, _ = rq.shape
    C = RET_CHUNK
    nb = math.gcd(RET_BATCH, B)
    q_spec = pl.BlockSpec((nb, C, RET_QK_WIDTH), lambda b, c: (b, c, 0))
    kt_spec = pl.BlockSpec((nb, 1, RET_QK_WIDTH, C), lambda b, c: (b, c, 0, 0))
    v_spec = pl.BlockSpec((nb, C, RET_V_WIDTH), lambda b, c: (b, c, 0))
    return pl.pallas_call(
        _retention_kernel,
        grid=(B // nb, S // C),
        in_specs=[q_spec, kt_spec, v_spec],
        out_specs=v_spec,
        out_shape=jax.ShapeDtypeStruct((B, S, RET_V_WIDTH), F32),
        scratch_shapes=[pltpu.VMEM((nb, RET_HEADS, RET_QK_DIM, RET_V_DIM), F32)],
        compiler_params=pltpu.CompilerParams(
            dimension_semantics=("parallel", "arbitrary"), vmem_limit_bytes=VMEM_LIMIT_BYTES),
        name="retention",
    )(rq, rkt, rv)


def _out_ffn_kernel(x_ref, a_ref, y0_ref, g0_ref, yn_ref, gn_ref, rw_ref, wo_ref, ln_ref, wu_ref, wd_ref,
                    o_ref, ret_ref):
    na = a_ref.shape[1]

    def norm_gate(y_ref, g_ref):
        ret = []
        for h in range(RET_HEADS):
            vsl = slice(h * RET_V_DIM, (h + 1) * RET_V_DIM)
            y = y_ref[:, vsl]
            yc = y - jnp.mean(y, axis=-1, keepdims=True)
            var = jnp.mean(yc * yc, axis=-1, keepdims=True)
            gate = g_ref[:, vsl].astype(F32)
            ret.append((gate * _sigmoid(gate) * (yc * lax.rsqrt(var + EPS) * rw_ref[:, vsl])).astype(BF16))
        return jnp.concatenate(ret, axis=1)

    @pl.when(pl.program_id(0) == 0)
    def _():
        ret_ref[...] = norm_gate(y0_ref, g0_ref)

    mix = _dot(a_ref[...], wo_ref[0:na, :]) + _dot(ret_ref[...], wo_ref[na:, :])
    ret_ref[...] = norm_gate(yn_ref, gn_ref)
    h = x_ref[...] + mix
    ms = jnp.mean(h * h, axis=-1, keepdims=True)
    hn = (h * lax.rsqrt(ms + EPS) * ln_ref[...]).astype(BF16)
    d_ff = wu_ref.shape[1]
    acc = None
    for f in range(d_ff // FF_CHUNK):
        cols = slice(f * FF_CHUNK, (f + 1) * FF_CHUNK)
        u = jnp.maximum(_dot(hn, wu_ref[:, cols]), 0.0)
        d = _dot((u * u).astype(BF16), wd_ref[cols, :])
        acc = d if acc is None else acc + d
    o_ref[...] = h + acc


def _out_ffn(x2, o_nsa, y_ret, g_ret, ret_w, w_out, ln2, w_up, w_down):
    N, D = x2.shape
    tm = TM_FFN
    n = N // tm
    row = lambda i: (i, 0)
    nxt = lambda i: (jnp.minimum(i + 1, n - 1), 0)
    const = lambda i: (0, 0)
    resident = functools.partial(pl.BlockSpec, index_map=const, pipeline_mode=pl.Buffered(1))
    return pl.pallas_call(
        _out_ffn_kernel,
        grid=(n,),
        in_specs=[
            pl.BlockSpec((tm, D), row),
            pl.BlockSpec((tm, o_nsa.shape[1]), row),
            pl.BlockSpec((tm, y_ret.shape[1]), const),
            pl.BlockSpec((tm, g_ret.shape[1]), const),
            pl.BlockSpec((tm, y_ret.shape[1]), nxt),
            pl.BlockSpec((tm, g_ret.shape[1]), nxt),
            pl.BlockSpec((1, RET_V_WIDTH), const),
            resident(w_out.shape),
            pl.BlockSpec((1, D), const),
            resident(w_up.shape),
            resident(w_down.shape),
        ],
        out_specs=pl.BlockSpec((tm, D), row),
        out_shape=jax.ShapeDtypeStruct((N, D), F32),
        scratch_shapes=[pltpu.VMEM((tm, y_ret.shape[1]), BF16)],
        compiler_params=pltpu.CompilerParams(
            dimension_semantics=("arbitrary",), vmem_limit_bytes=VMEM_LIMIT_BYTES),
        name="out_ffn",
    )(x2, o_nsa, y_ret, g_ret, y_ret, g_ret, ret_w, w_out, ln2, w_up, w_down)


def _rope_tables(S):
    half = HEAD_DIM // 2
    inv = ROPE_THETA ** (-jnp.arange(half, dtype=F32) / half)
    ang = jnp.arange(S).astype(F32)[:, None] * inv[None, :]
    cos, sin = jnp.cos(ang), jnp.sin(ang)
    reps = LANES // HEAD_DIM
    cos_h = jnp.concatenate([cos, cos], axis=-1)
    sin_h = jnp.concatenate([-sin, sin], axis=-1)
    return jnp.tile(cos_h, (1, reps)), jnp.tile(sin_h, (1, reps)), cos_h.T, sin_h.T


def _layer(h, ln1_w, w_in, q_norm_w, k_norm_w, cmp_pe_k, cmp_pe_v, cmp_wk1, cmp_wk2,
           cmp_wv1, cmp_wv2, ret_norm_w, w_out, ln2_w, w_up, w_down):
    B, S, D = h.shape
    N = B * S
    G = NSA_KV_HEADS
    x2 = h.reshape(N, D)

    w_perm, w_perm_t = _permute_w_in(w_in)
    cos_l, sin_l, cos_t, sin_t = _rope_tables(S)
    reps = LANES // HEAD_DIM
    qn_t = jnp.broadcast_to(q_norm_w[:, None], (HEAD_DIM, TM_PROJ))
    kn = jnp.tile(k_norm_w, (1, reps))

    (q, kc, vc, ks, vs, kw, vw, gates, rq, rk, rv, rg) = _in_proj(
        x2, ln1_w[None, :], w_perm.astype(BF16), w_perm_t.astype(BF16), cos_l, sin_l, cos_t, sin_t, qn_t, kn, B, S)

    kcmp, vcmp = _compress(
        kc, vc, jnp.tile(cmp_pe_k, (1, G)), jnp.tile(cmp_pe_v, (1, G)),
        _block_diag_w1(cmp_wk1).astype(BF16), cmp_wk2.astype(BF16), _block_diag_w1(cmp_wv1).astype(BF16),
        jnp.pad(cmp_wv2, ((0, 0), (0, LANES - HEAD_DIM))).astype(BF16), B, S)

    o_nsa = _nsa(q, kcmp, vcmp, ks, vs, kw, vw, gates.reshape(B, S, LANES))
    y_ret = _retention(rq.reshape(B, S, -1), rk, rv.reshape(B, S, -1))

    out = _out_ffn(x2, o_nsa.reshape(N, -1), y_ret.reshape(N, -1), rg, ret_norm_w.reshape(1, RET_V_WIDTH),
                   w_out.astype(BF16), ln2_w[None, :], w_up.astype(BF16), w_down.astype(BF16))
    return out.reshape(B, S, D)


def kernel(x, ln1_w, w_in, q_norm_w, k_norm_w, cmp_pe_k, cmp_pe_v, cmp_wk1, cmp_wk2, cmp_wv1, cmp_wv2,
           ret_norm_w, w_out, ln2_w, w_up, w_down):
    h = x
    for l in range(ln1_w.shape[0]):
        h = _layer(h, ln1_w[l], w_in[l], q_norm_w[l], k_norm_w[l], cmp_pe_k[l], cmp_pe_v[l],
                   cmp_wk1[l], cmp_wk2[l], cmp_wv1[l], cmp_wv2[l], ret_norm_w[l], w_out[l],
                   ln2_w[l], w_up[l], w_down[l])
    return h
```

```python
import functools
import math

import jax
import jax.numpy as jnp
import numpy as np
from jax import lax
from jax.experimental import pallas as pl
from jax.experimental.pallas import tpu as pltpu

F32 = jnp.float32
BF16 = jnp.bfloat16

NSA_HEADS = 8
NSA_KV_HEADS = 2
NSA_GROUP = NSA_HEADS // NSA_KV_HEADS
HEAD_DIM = 64
CMP_BLOCK = 32
CMP_STRIDE = 16
CMP_HIDDEN = 256
SEL_BLOCK = 64
SEL_TOPK = 8
SEL_FORCED = 3
WINDOW = 256
RET_HEADS = 4
RET_QK_DIM = 64
RET_V_DIM = 128
RET_CHUNK = 128
ROPE_THETA = 10000.0
EPS = 1e-6
NEG_INF = -1.0e30
SEL_BIAS = 2.0 ** 100

NSA_Q_DIM = NSA_HEADS * HEAD_DIM
NSA_KV_DIM = NSA_KV_HEADS * HEAD_DIM
NSA_GATE_DIM = 3 * NSA_HEADS
RET_QK_WIDTH = RET_HEADS * RET_QK_DIM
RET_V_WIDTH = RET_HEADS * RET_V_DIM

LANES = 128
VMEM_LIMIT_BYTES = 56 * 1024 * 1024

TM_PROJ = 512
TQ = 256
TK = 256
KV_CHUNK = 128
ONES_ROWS = 16
TM_FFN = 512
FF_CHUNK = 512
RET_BATCH = 8


def _dot(a, b):
    return jnp.dot(a, b, preferred_element_type=F32)


def _dot_nt(a, b):
    return lax.dot_general(a, b, (((1,), (1,)), ((), ())), preferred_element_type=F32)


def _sigmoid(x):
    return 1.0 / (1.0 + jnp.exp(-x))


_K0 = 0
_V0 = _K0 + 3 * NSA_KV_DIM
_RQ0 = _V0 + NSA_KV_DIM + LANES
_RV0 = _RQ0 + RET_QK_WIDTH
_RG0 = _RV0 + RET_V_WIDTH
_W_IN_COLS = _RG0 + RET_V_WIDTH
_TQ0 = 0
_TV0 = _TQ0 + NSA_Q_DIM
_TRK0 = _TV0 + 2 * NSA_KV_DIM
_WT_ROWS = _TRK0 + RET_QK_WIDTH


def _permute_w_in(w_in):
    sizes = [NSA_Q_DIM] + [NSA_KV_DIM] * 6 + [NSA_GATE_DIM, RET_QK_WIDTH, RET_QK_WIDTH, RET_V_WIDTH, RET_V_WIDTH]
    off = [int(o) for o in np.concatenate([[0], np.cumsum(sizes)])]
    (q, kc, vc, ks, vs, kw, vw, gate, rq, rk, rv, rg) = [w_in[:, off[i]:off[i + 1]] for i in range(12)]
    d = w_in.shape[0]
    gate = gate.reshape(d, 3, NSA_KV_HEADS, NSA_GROUP).transpose(0, 2, 1, 3).reshape(d, NSA_KV_HEADS, 3 * NSA_GROUP)
    gate = jnp.pad(gate, ((0, 0), (0, 0), (0, HEAD_DIM - 3 * NSA_GROUP))).reshape(d, LANES)
    w = jnp.concatenate([kc, ks, kw, vc, gate, rq, rv, rg], axis=1)
    wt = jnp.concatenate([q, vs, vw, rk], axis=1).T
    assert w.shape[1] == _W_IN_COLS and wt.shape[0] == _WT_ROWS
    return w, wt


def _in_proj_kernel(x_ref, ln_ref, w_ref, wt_ref, cos_ref, sin_ref, cost_ref, sint_ref, qnt_ref, kn_ref,
                    q_ref, kc_ref, vc_ref, ks_ref, vs_ref, kw_ref, vw_ref, gate_ref,
                    rq_ref, rk_ref, rv_ref, rg_ref, *, tiles_per_seq):
    x = x_ref[...]
    tm = x.shape[0]
    ms = jnp.mean(x * x, axis=-1, keepdims=True)
    xn = (x * lax.rsqrt(ms + EPS) * ln_ref[...]).astype(BF16)
    cos = cos_ref[...]
    sin = sin_ref[...]
    lane = lax.broadcasted_iota(jnp.int32, (1, LANES), 1)
    low_half = (lane & (HEAD_DIM // 2)) == 0
    first_head = lane < HEAD_DIM

    def proj(a, b):
        return _dot(xn, w_ref[:, a:b])

    def rope(t):
        swapped = jnp.where(low_half, pltpu.roll(t, LANES - HEAD_DIM // 2, 1), pltpu.roll(t, HEAD_DIM // 2, 1))
        return t * cos + swapped * sin

    def head_norm(t, w):
        t2 = t * t
        s0 = jnp.sum(jnp.where(first_head, t2, 0.0), axis=-1, keepdims=True)
        s1 = jnp.sum(jnp.where(first_head, 0.0, t2), axis=-1, keepdims=True)
        msq = jnp.where(first_head, s0, s1) * (1.0 / HEAD_DIM)
        return t * lax.rsqrt(msq + EPS) * w

    def tiles(sec):
        return [sec[:, c * LANES:(c + 1) * LANES] for c in range(sec.shape[1] // LANES)]

    def split_heads(t, fill):
        return [jnp.where(first_head, t, fill), jnp.where(first_head, pltpu.roll(t, HEAD_DIM, 1), fill)]

    def proj_t(a, b):
        return _dot_nt(wt_ref[a:b, :], xn)

    cos_t = cost_ref[...]
    sin_t = sint_ref[...]

    def rope_t(t):
        half = HEAD_DIM // 2
        return t * cos_t + jnp.concatenate([t[half:], t[:half]], axis=0) * sin_t

    q_t = proj_t(_TQ0, _TV0)
    qn_t = qnt_ref[...]
    for h in range(NSA_HEADS):
        t = q_t[h * HEAD_DIM:(h + 1) * HEAD_DIM]
        t = t * lax.rsqrt(jnp.mean(t * t, axis=0, keepdims=True) + EPS) * qn_t
        t = rope_t(t) * (HEAD_DIM ** -0.5 * math.log2(math.e))
        q_ref[0, h // NSA_GROUP, h % NSA_GROUP] = t.astype(BF16)

    def store_values_t(v, out_ref):
        chunk = out_ref.shape[-1]
        ones = jnp.ones((ONES_ROWS, chunk), BF16)
        for c in range(tm // chunk):
            for g in range(NSA_KV_HEADS):
                out_ref[0, g, c, 0:HEAD_DIM, :] = v[g * HEAD_DIM:(g + 1) * HEAD_DIM,
                                                    c * chunk:(c + 1) * chunk].astype(BF16)
                out_ref[0, g, c, HEAD_DIM:HEAD_DIM + ONES_ROWS, :] = ones

    v_t = proj_t(_TV0, _TRK0)
    store_values_t(v_t[0:NSA_KV_DIM], vs_ref)
    store_values_t(v_t[NSA_KV_DIM:2 * NSA_KV_DIM], vw_ref)

    rk_t = proj_t(_TRK0, _WT_ROWS)
    for h in range(RET_HEADS):
        rows = slice(h * RET_QK_DIM, (h + 1) * RET_QK_DIM)
        t = (rope_t(rk_t[rows]) * (RET_QK_DIM ** -0.5)).astype(BF16)
        for j in range(tm // RET_CHUNK):
            rk_ref[0, j, rows, :] = t[:, j * RET_CHUNK:(j + 1) * RET_CHUNK]

    ksec = tiles(proj(_K0, _V0))
    kc = rope(head_norm(ksec[0], kn_ref[0:1, :]))
    ks = rope(head_norm(ksec[1], kn_ref[1:2, :]))
    kw = rope(head_norm(ksec[2], kn_ref[2:3, :])).astype(BF16)
    s0 = (pl.program_id(0) % tiles_per_seq) * tm
    blk = jnp.right_shift(s0 + lax.broadcasted_iota(jnp.int32, (tm, 1), 0), int(math.log2(SEL_BLOCK)))
    onehot = (lane - HEAD_DIM == blk).astype(F32)
    ks_aug = split_heads(ks, onehot)
    kc_ref[...] = kc
    for g in range(NSA_KV_HEADS):
        ks_ref[0, g] = ks_aug[g].astype(BF16)
        kw_ref[0, g] = kw[:, g * HEAD_DIM:(g + 1) * HEAD_DIM]

    vsec = tiles(proj(_V0, _RQ0))
    vc_ref[...] = vsec[0]
    gate_ref[...] = vsec[1]

    for c, t in enumerate(tiles(proj(_RQ0, _RV0))):
        rq_ref[:, c * LANES:(c + 1) * LANES] = rope(t).astype(BF16)
    rv_ref[...] = proj(_RV0, _RG0).astype(BF16)
    rg_ref[...] = proj(_RG0, _W_IN_COLS).astype(BF16)


def _in_proj(x2, ln1, w_perm, w_perm_t, cos, sin, cos_t, sin_t, qn_t, kn, B, S):
    N, D = x2.shape
    tm = TM_PROJ
    spt = S // tm
    G, R, dh = NSA_KV_HEADS, NSA_GROUP, HEAD_DIM

    def row(i):
        return (i, 0)

    def hm(i):
        return (i // spt, 0, i % spt, 0)

    hm_spec = pl.BlockSpec((1, G, tm, dh), hm)
    out_shape = [
        jax.ShapeDtypeStruct((B, G, R, dh, S), BF16),
        jax.ShapeDtypeStruct((N, LANES), F32),
        jax.ShapeDtypeStruct((N, LANES), F32),
        jax.ShapeDtypeStruct((B, G, S, LANES), BF16),
        jax.ShapeDtypeStruct((B, G, S // TK, dh + ONES_ROWS, TK), BF16),
        jax.ShapeDtypeStruct((B, G, S, dh), BF16),
        jax.ShapeDtypeStruct((B, G, S // KV_CHUNK, dh + ONES_ROWS, KV_CHUNK), BF16),
        jax.ShapeDtypeStruct((N, LANES), F32),
        jax.ShapeDtypeStruct((N, RET_QK_WIDTH), BF16),
        jax.ShapeDtypeStruct((B, S // RET_CHUNK, RET_QK_WIDTH, RET_CHUNK), BF16),
        jax.ShapeDtypeStruct((N, RET_V_WIDTH), BF16),
        jax.ShapeDtypeStruct((N, RET_V_WIDTH), BF16),
    ]
    out_specs = [
        pl.BlockSpec((1, G, R, dh, tm), lambda i: (i // spt, 0, 0, 0, i % spt)),
        pl.BlockSpec((tm, LANES), row), pl.BlockSpec((tm, LANES), row), pl.BlockSpec((1, G, tm, LANES), hm),
        pl.BlockSpec((1, G, tm // TK, dh + ONES_ROWS, TK), lambda i: (i // spt, 0, i % spt, 0, 0)),
        hm_spec,
        pl.BlockSpec((1, G, tm // KV_CHUNK, dh + ONES_ROWS, KV_CHUNK), lambda i: (i // spt, 0, i % spt, 0, 0)),
        pl.BlockSpec((tm, LANES), row),
        pl.BlockSpec((tm, RET_QK_WIDTH), row),
        pl.BlockSpec((1, tm // RET_CHUNK, RET_QK_WIDTH, RET_CHUNK), lambda i: (i // spt, i % spt, 0, 0)),
        pl.BlockSpec((tm, RET_V_WIDTH), row),
        pl.BlockSpec((tm, RET_V_WIDTH), row),
    ]
    const = lambda i: (0, 0)
    in_specs = [
        pl.BlockSpec((tm, D), row),
        pl.BlockSpec((1, D), const),
        pl.BlockSpec((D, _W_IN_COLS), const, pipeline_mode=pl.Buffered(1)),
        pl.BlockSpec((_WT_ROWS, D), const, pipeline_mode=pl.Buffered(1)),
        pl.BlockSpec((tm, LANES), lambda i: (i % spt, 0)),
        pl.BlockSpec((tm, LANES), lambda i: (i % spt, 0)),
        pl.BlockSpec((dh, tm), lambda i: (0, i % spt)),
        pl.BlockSpec((dh, tm), lambda i: (0, i % spt)),
        pl.BlockSpec((dh, tm), const),
        pl.BlockSpec((3, LANES), const),
    ]
    return pl.pallas_call(
        functools.partial(_in_proj_kernel, tiles_per_seq=spt),
        grid=(N // tm,),
        in_specs=in_specs,
        out_specs=out_specs,
        out_shape=out_shape,
        compiler_params=pltpu.CompilerParams(
            dimension_semantics=("parallel",), vmem_limit_bytes=VMEM_LIMIT_BYTES),
        name="in_proj",
    )(x2, ln1, w_perm, w_perm_t, cos, sin, cos_t, sin_t, qn_t, kn)


def _compress_kernel(k_ref, v_ref, pek_ref, pev_ref, wk1_ref, wk2_ref, wv1_ref, wv2_ref, ko_ref, vo_ref):
    n_seg = k_ref.shape[0] // CMP_STRIDE
    G = NSA_KV_HEADS
    assert G == 2
    first_group = lax.broadcasted_iota(jnp.int32, (1, LANES), 1) < HEAD_DIM

    def run(x_ref, pe_ref, w1_ref):
        acc = {}
        for p in range(CMP_STRIDE // 2):
            xa = x_ref[pl.ds(2 * p, n_seg, stride=CMP_STRIDE), :]
            xb = x_ref[pl.ds(2 * p + 1, n_seg, stride=CMP_STRIDE), :]
            xa_r = pltpu.roll(xa, HEAD_DIM, 1)
            xb_r = pltpu.roll(xb, HEAD_DIM, 1)
            for half in range(2):
                l0 = half * CMP_STRIDE + 2 * p
                pe_a = pe_ref[l0:l0 + 1, :]
                pe_b = pe_ref[l0 + 1:l0 + 2, :]
                w = w1_ref[l0 * HEAD_DIM:(l0 + 2) * HEAD_DIM, :]
                pairs = [jnp.where(first_group, xa + pe_a, xb_r + pe_b),
                         jnp.where(first_group, xa_r + pe_a, xb + pe_b)]
                for g in range(G):
                    d = _dot(pairs[g].astype(BF16), w)
                    acc[half, g] = d if p == 0 else acc[half, g] + d
        out = []
        for g in range(G):
            h = acc[0, g] + pltpu.roll(acc[1, g], n_seg - 1, 0)
            out.append((h * _sigmoid(h)).astype(BF16))
        return out

    hk = run(k_ref, pek_ref, wk1_ref)
    hv = run(v_ref, pev_ref, wv1_ref)
    for g in range(G):
        ko_ref[0, g] = _dot(hk[g], wk2_ref[...]).astype(BF16)
        vo_ref[:, g * n_seg:(g + 1) * n_seg] = _dot(hv[g], wv2_ref[...]).T[0:HEAD_DIM, :].astype(BF16)


def _compress(kc, vc, pek, pev, wk1, wk2, wv1, wv2, B, S):
    G = NSA_KV_HEADS
    n_seg = S // CMP_STRIDE
    width = kc.shape[1]
    const2 = lambda b: (0, 0)
    tok_spec = pl.BlockSpec((S, width), lambda b: (b, 0))
    return pl.pallas_call(
        _compress_kernel,
        grid=(B,),
        in_specs=[tok_spec, tok_spec,
                  pl.BlockSpec(pek.shape, const2), pl.BlockSpec(pev.shape, const2),
                  pl.BlockSpec(wk1.shape, const2), pl.BlockSpec(wk2.shape, const2),
                  pl.BlockSpec(wv1.shape, const2), pl.BlockSpec(wv2.shape, const2)],
        out_specs=[pl.BlockSpec((1, G, n_seg, HEAD_DIM), lambda b: (b, 0, 0, 0)),
                   pl.BlockSpec((HEAD_DIM, G * n_seg), lambda b: (0, b))],
        out_shape=[jax.ShapeDtypeStruct((B, G, n_seg, HEAD_DIM), BF16),
                   jax.ShapeDtypeStruct((HEAD_DIM, B * G * n_seg), BF16)],
        compiler_params=pltpu.CompilerParams(
            dimension_semantics=("parallel",), vmem_limit_bytes=VMEM_LIMIT_BYTES),
        name="compress",
    )(kc, vc, pek, pev, wk1, wk2, wv1, wv2)


def _nsa_kernel(q_ref, kcmp_ref, vcmp_ref, ks_ref, vs_ref, kw_ref, vw_ref, gate_ref,
                o_ref, m_ref, acc_ref, s0_ref):
    G, R = NSA_KV_HEADS, NSA_GROUP
    n_cp = kcmp_ref.shape[2]
    S = ks_ref.shape[2]
    n_cmp = (S - CMP_BLOCK) // CMP_STRIDE + 1
    n_blk = S // SEL_BLOCK
    qi = pl.program_id(1)
    q0 = qi * TQ
    t_q = q0 + lax.broadcasted_iota(jnp.int32, (1, TQ), 1)
    qs = [jnp.concatenate([q_ref[0, g, r] for r in range(R)], axis=1) for g in range(G)]

    def heads(x):
        return [x[:, r * TQ:(r + 1) * TQ] for r in range(R)]

    def masked_exp(s_heads, mask):
        es = []
        for s in s_heads:
            s = jnp.where(mask, s, NEG_INF)
            es.append(jnp.exp2(s - jnp.max(s, axis=0, keepdims=True)))
        return es

    c_idx = lax.broadcasted_iota(jnp.int32, (n_cp, TQ), 0)
    cmask = ((c_idx * CMP_STRIDE + (CMP_BLOCK - 1)) <= t_q) & (c_idx < n_cmp)
    jj = lax.broadcasted_iota(jnp.int32, (n_blk, n_cp), 0)
    cc = lax.broadcasted_iota(jnp.int32, (n_blk, n_cp), 1)
    overlap = ((cc * CMP_STRIDE < (jj + 1) * SEL_BLOCK) & (cc * CMP_STRIDE + CMP_BLOCK > jj * SEL_BLOCK)
               & (cc < n_cmp)).astype(BF16)
    jb = lax.broadcasted_iota(jnp.int32, (n_blk, TQ), 0)
    jb_f = jb.astype(F32)
    cur = jnp.right_shift(t_q, int(math.log2(SEL_BLOCK)))
    forced = (jb == 0) | (jb == cur) | (jb == cur - 1)
    valid = jb <= cur

    n_sub = TQ // KV_CHUNK
    n_wc = (KV_CHUNK + WINDOW) // KV_CHUNK
    win_c0, win_mask, win_q = [], [], []
    for h in range(n_sub):
        c0 = jnp.maximum(q0 // KV_CHUNK + h - WINDOW // KV_CHUNK, 0)
        kpos = c0 * KV_CHUNK + lax.broadcasted_iota(jnp.int32, (n_wc * KV_CHUNK, KV_CHUNK), 0)
        diff = t_q[:, h * KV_CHUNK:(h + 1) * KV_CHUNK] - kpos
        win_c0.append(c0)
        win_mask.append((diff >= 0) & (diff < WINDOW))
        win_q.append([jnp.concatenate([qs[g][:, r * TQ + h * KV_CHUNK:r * TQ + (h + 1) * KV_CHUNK]
                                       for r in range(R)], axis=1) for g in range(G)])

    def window_keys(g, h):
        return kw_ref[0, g, pl.ds(pl.multiple_of(win_c0[h] * KV_CHUNK, KV_CHUNK), n_wc * KV_CHUNK), :]

    s_cmp = [_dot(kcmp_ref[0, g], qs[g]) for g in range(G)]
    s_win = [[_dot(window_keys(g, h), win_q[h][g]) for h in range(n_sub)] for g in range(G)]
    any_cmp = t_q >= CMP_BLOCK - 1
    p_cmp, o_cmp, o_win = [], [], []
    for g in range(G):
        es = masked_exp(heads(s_cmp[g]), cmask)
        p_cmp.append([e * jnp.where(any_cmp, 1.0 / jnp.sum(e, axis=0, keepdims=True), 0.0) for e in es])

    sels = []
    for g in range(G):
        p_c = p_cmp[g]
        p_sum = p_c[0]
        for r in range(1, R):
            p_sum = p_sum + p_c[r]
        p_hi = p_sum.astype(BF16)
        p_lo = (p_sum - p_hi.astype(F32)).astype(BF16)
        p_slc = _dot(overlap, p_hi) + _dot(overlap, p_lo)
        work = jnp.where(valid & jnp.logical_not(forced), p_slc, -1.0)
        picked = forced
        for _ in range(min(SEL_TOPK, n_blk) - SEL_FORCED):
            best = jnp.max(work, axis=0, keepdims=True)
            first = jnp.min(jnp.where(work == best, jb_f, float(n_blk)), axis=0, keepdims=True)
            hit = jb_f == first
            picked = picked | hit
            work = jnp.where(hit, -2.0, work)
        bias = jnp.where(picked, 0.0, -SEL_BIAS).astype(BF16)
        sels.append(jnp.concatenate([qs[g], jnp.concatenate([bias] * R, axis=1),
                                     jnp.zeros((LANES - HEAD_DIM - n_blk, R * TQ), BF16)], axis=0))

    m_ref[...] = jnp.full(m_ref.shape, NEG_INF, F32)
    acc_ref[...] = jnp.zeros(acc_ref.shape, F32)

    def scores(g, kt):
        return _dot(ks_ref[0, g, pl.ds(pl.multiple_of(kt * TK, TK), TK), :], sels[g])

    def accumulate(g, kt, s, causal):
        if causal is not None:
            s = jnp.concatenate([jnp.where(causal, sh, NEG_INF) for sh in heads(s)], axis=1)
        m_old = m_ref[g]
        m_new = jnp.maximum(m_old, jnp.max(s, axis=0, keepdims=True))
        alpha = jnp.exp2(m_old - m_new)
        p = jnp.exp2(s - m_new)
        acc_ref[g] = alpha * acc_ref[g] + _dot(vs_ref[0, g, kt], p.astype(BF16))
        m_ref[g] = m_new

    assert G == 2
    s0_ref[...] = scores(0, 0)

    for g in range(G):
        o_cmp.append(_dot(vcmp_ref[:, g * n_cp:(g + 1) * n_cp],
                          jnp.concatenate(p_cmp[g], axis=1).astype(BF16)))
    for g in range(G):
        sub = []
        for h in range(n_sub):
            s_heads = [s_win[g][h][:, r * KV_CHUNK:(r + 1) * KV_CHUNK] for r in range(R)]
            e_w = jnp.concatenate(masked_exp(s_heads, win_mask[h]), axis=1).astype(BF16)
            v_w = jnp.concatenate([vw_ref[0, g, win_c0[h] + j] for j in range(n_wc)], axis=1)
            sub.append(_dot(v_w, e_w))
        o_win.append(jnp.concatenate([sub[h][:, r * KV_CHUNK:(r + 1) * KV_CHUNK]
                                      for r in range(R) for h in range(n_sub)], axis=1))

    def interior_tile(kt, carry):
        s1 = scores(1, kt)
        accumulate(0, kt, s0_ref[...], None)
        s0_ref[...] = scores(0, kt + 1)
        accumulate(1, kt, s1, None)
        return carry

    lax.fori_loop(0, qi, interior_tile, 0)
    causal = (q0 + lax.broadcasted_iota(jnp.int32, (TK, TQ), 0)) <= t_q
    s1 = scores(1, qi)
    accumulate(0, qi, s0_ref[...], causal)
    accumulate(1, qi, s1, causal)

    gates = _sigmoid(gate_ref[0].T)
    outs = []
    for g in range(G):
        o_sel = acc_ref[g]
        for r, (oc, os_, ow) in enumerate(zip(heads(o_cmp[g]), heads(o_sel), heads(o_win[g]))):
            g0 = g * HEAD_DIM + r
            g_sel = gates[g0 + R:g0 + R + 1] * (1.0 / os_[HEAD_DIM:HEAD_DIM + 1])
            g_win = gates[g0 + 2 * R:g0 + 2 * R + 1] * (1.0 / ow[HEAD_DIM:HEAD_DIM + 1])
            outs.append(gates[g0:g0 + 1] * oc + g_sel * os_[0:HEAD_DIM] + g_win * ow[0:HEAD_DIM])
    o_ref[0] = jnp.concatenate(outs, axis=0).T.astype(BF16)


def _nsa(q, kcmp, vcmp, ks, vs, kw, vw, gates):
    B, G, R, dh, S = q.shape
    n_cp = kcmp.shape[2]
    assert TQ == TK, "the key sweep treats exactly one tile per query tile as the diagonal"
    k_spec = lambda a: pl.BlockSpec((1,) + a.shape[1:], lambda b, i: (b, 0, 0, 0))
    vt_spec = lambda a: pl.BlockSpec((1,) + a.shape[1:], lambda b, i: (b, 0, 0, 0, 0))
    return pl.pallas_call(
        _nsa_kernel,
        grid=(B, S // TQ),
        in_specs=[
            pl.BlockSpec((1, G, R, dh, TQ), lambda b, i: (b, 0, 0, 0, i)),
            pl.BlockSpec((1, G, n_cp, dh), lambda b, i: (b, 0, 0, 0)),
            pl.BlockSpec((dh, G * n_cp), lambda b, i: (0, b)),
            k_spec(ks), vt_spec(vs), k_spec(kw), vt_spec(vw),
            pl.BlockSpec((1, TQ, LANES), lambda b, i: (b, i, 0)),
        ],
        out_specs=pl.BlockSpec((1, TQ, G * R * dh), lambda b, i: (b, i, 0)),
        out_shape=jax.ShapeDtypeStruct((B, S, G * R * dh), BF16),
        scratch_shapes=[pltpu.VMEM((G, 1, R * TQ), F32), pltpu.VMEM((G, dh + ONES_ROWS, R * TQ), F32),
                        pltpu.VMEM((TK, R * TQ), F32)],
        compiler_params=pltpu.CompilerParams(
            dimension_semantics=("parallel", "arbitrary"), vmem_limit_bytes=VMEM_LIMIT_BYTES),
        name="nsa_attention",
    )(q, kcmp, vcmp, ks, vs, kw, vw, gates)


def _retention_kernel(q_ref, kt_ref, v_ref, g_ref, w_ref, o_ref, state_ref):
    C = RET_CHUNK
    NB = q_ref.shape[0]

    @pl.when(pl.program_id(1) == 0)
    def _():
        state_ref[...] = jnp.zeros(state_ref.shape, F32)

    i_col = lax.broadcasted_iota(jnp.int32, (C, 1), 0)
    i_row = lax.broadcasted_iota(jnp.int32, (1, C), 1)
    d_int = i_col - i_row
    log_gamma = [math.log(1.0 - 2.0 ** (-5.0 - h)) for h in range(RET_HEADS)]
    dmat = [jnp.where(d_int >= 0, jnp.exp(lg * jnp.maximum(d_int, 0).astype(F32)), 0.0) for lg in log_gamma]
    xi = [jnp.exp(lg * (i_col + 1).astype(F32)) for lg in log_gamma]
    zeta = [jnp.exp(lg * (C - 1 - i_row).astype(F32)) for lg in log_gamma]
    gamma_c = [math.exp(lg * C) for lg in log_gamma]

    units = [(n, h) for n in range(NB) for h in range(RET_HEADS)]
    q, kt, v, st = {}, {}, {}, {}
    for n, h in units:
        q[n, h] = q_ref[n, :, h * RET_QK_DIM:(h + 1) * RET_QK_DIM]
        kt[n, h] = kt_ref[n, 0, h * RET_QK_DIM:(h + 1) * RET_QK_DIM, :]
        v[n, h] = v_ref[n, :, h * RET_V_DIM:(h + 1) * RET_V_DIM]
        st[n, h] = state_ref[n, h]
    inner = {u: _dot(q[u], kt[u]) for u in units}
    cross = {u: _dot(q[u], st[u].astype(BF16)) for u in units}
    kv = {u: _dot((kt[u].astype(F32) * zeta[u[1]]).astype(BF16), v[u]) for u in units}
    for n, h in units:
        u = (n, h)
        vsl = slice(h * RET_V_DIM, (h + 1) * RET_V_DIM)
        y = _dot((inner[u] * dmat[h]).astype(BF16), v[u]) + cross[u] * xi[h]
        state_ref[n, h] = gamma_c[h] * st[u] + kv[u]

        mu = jnp.mean(y, axis=-1, keepdims=True)
        yc = y - mu
        var = jnp.mean(yc * yc, axis=-1, keepdims=True)
        yn = yc * lax.rsqrt(var + EPS) * w_ref[:, vsl]
        gate = g_ref[n, :, vsl].astype(F32)
        o_ref[n, :, vsl] = (gate * _sigmoid(gate) * yn).astype(BF16)


def _retention(rq, rkt, rv, rg, w):
    B, S, _ = rq.shape
    C = RET_CHUNK
    nb = math.gcd(RET_BATCH, B)
    q_spec = pl.BlockSpec((nb, C, RET_QK_WIDTH), lambda b, c: (b, c, 0))
    kt_spec = pl.BlockSpec((nb, 1, RET_QK_WIDTH, C), lambda b, c: (b, c, 0, 0))
    v_spec = pl.BlockSpec((nb, C, RET_V_WIDTH), lambda b, c: (b, c, 0))
    return pl.pallas_call(
        _retention_kernel,
        grid=(B // nb, S // C),
        in_specs=[q_spec, kt_spec, v_spec, v_spec, pl.BlockSpec((1, RET_V_WIDTH), lambda b, c: (0, 0))],
        out_specs=v_spec,
        out_shape=jax.ShapeDtypeStruct((B, S, RET_V_WIDTH), BF16),
        scratch_shapes=[pltpu.VMEM((nb, RET_HEADS, RET_QK_DIM, RET_V_DIM), F32)],
        compiler_params=pltpu.CompilerParams(
            dimension_semantics=("parallel", "arbitrary"), vmem_limit_bytes=VMEM_LIMIT_BYTES),
        name="retention",
    )(rq, rkt, rv, rg, w)


def _out_ffn_kernel(x_ref, a_ref, r_ref, wo_ref, ln_ref, wu_ref, wd_ref, o_ref):
    na = a_ref.shape[1]
    mix = _dot(a_ref[...], wo_ref[0:na, :]) + _dot(r_ref[...], wo_ref[na:, :])
    h = x_ref[...] + mix
    ms = jnp.mean(h * h, axis=-1, keepdims=True)
    hn = (h * lax.rsqrt(ms + EPS) * ln_ref[...]).astype(BF16)
    d_ff = wu_ref.shape[1]
    acc = None
    for f in range(d_ff // FF_CHUNK):
        cols = slice(f * FF_CHUNK, (f + 1) * FF_CHUNK)
        u = jnp.maximum(_dot(hn, wu_ref[:, cols]), 0.0)
        d = _dot((u * u).astype(BF16), wd_ref[cols, :])
        acc = d if acc is None else acc + d
    o_ref[...] = h + acc


def _out_ffn(x2, o_nsa, o_ret, w_out, ln2, w_up, w_down):
    N, D = x2.shape
    tm = TM_FFN
    row = lambda i: (i, 0)
    const = lambda i: (0, 0)
    resident = functools.partial(pl.BlockSpec, index_map=const, pipeline_mode=pl.Buffered(1))
    return pl.pallas_call(
        _out_ffn_kernel,
        grid=(N // tm,),
        in_specs=[
            pl.BlockSpec((tm, D), row),
            pl.BlockSpec((tm, o_nsa.shape[1]), row),
            pl.BlockSpec((tm, o_ret.shape[1]), row),
            resident(w_out.shape),
            pl.BlockSpec((1, D), const),
            resident(w_up.shape),
            resident(w_down.shape),
        ],
        out_specs=pl.BlockSpec((tm, D), row),
        out_shape=jax.ShapeDtypeStruct((N, D), F32),
        compiler_params=pltpu.CompilerParams(
            dimension_semantics=("parallel",), vmem_limit_bytes=VMEM_LIMIT_BYTES),
        name="out_ffn",
    )(x2, o_nsa, o_ret, w_out, ln2, w_up, w_down)


def _rope_tables(S):
    half = HEAD_DIM // 2
    inv = ROPE_THETA ** (-jnp.arange(half, dtype=F32) / half)
    ang = jnp.arange(S).astype(F32)[:, None] * inv[None, :]
    cos, sin = jnp.cos(ang), jnp.sin(ang)
    reps = LANES // HEAD_DIM
    cos_h = jnp.concatenate([cos, cos], axis=-1)
    sin_h = jnp.concatenate([-sin, sin], axis=-1)
    return jnp.tile(cos_h, (1, reps)), jnp.tile(sin_h, (1, reps)), cos_h.T, sin_h.T


def _layer(h, ln1_w, w_in, q_norm_w, k_norm_w, cmp_pe_k, cmp_pe_v, cmp_wk1, cmp_wk2,
           cmp_wv1, cmp_wv2, ret_norm_w, w_out, ln2_w, w_up, w_down):
    B, S, D = h.shape
    N = B * S
    G = NSA_KV_HEADS
    x2 = h.reshape(N, D)

    w_perm, w_perm_t = _permute_w_in(w_in)
    cos_l, sin_l, cos_t, sin_t = _rope_tables(S)
    reps = LANES // HEAD_DIM
    qn_t = jnp.broadcast_to(q_norm_w[:, None], (HEAD_DIM, TM_PROJ))
    kn = jnp.tile(k_norm_w, (1, reps))

    (q, kc, vc, ks, vs, kw, vw, gates, rq, rk, rv, rg) = _in_proj(
        x2, ln1_w[None, :], w_perm.astype(BF16), w_perm_t.astype(BF16), cos_l, sin_l, cos_t, sin_t, qn_t, kn, B, S)

    kcmp, vcmp = _compress(
        kc, vc, jnp.tile(cmp_pe_k, (1, G)), jnp.tile(cmp_pe_v, (1, G)),
        cmp_wk1.astype(BF16), cmp_wk2.astype(BF16), cmp_wv1.astype(BF16),
        jnp.pad(cmp_wv2, ((0, 0), (0, LANES - HEAD_DIM))).astype(BF16), B, S)

    o_nsa = _nsa(q, kcmp, vcmp, ks, vs, kw, vw, gates.reshape(B, S, LANES))
    o_ret = _retention(rq.reshape(B, S, -1), rk, rv.reshape(B, S, -1),
                       rg.reshape(B, S, -1), ret_norm_w.reshape(1, RET_V_WIDTH))

    out = _out_ffn(x2, o_nsa.reshape(N, -1), o_ret.reshape(N, -1), w_out.astype(BF16),
                   ln2_w[None, :], w_up.astype(BF16), w_down.astype(BF16))
    return out.reshape(B, S, D)


def kernel(x, ln1_w, w_in, q_norm_w, k_norm_w, cmp_pe_k, cmp_pe_v, cmp_wk1, cmp_wk2, cmp_wv1, cmp_wv2,
           ret_norm_w, w_out, ln2_w, w_up, w_down):
    h = x
    for l in range(ln1_w.shape[0]):
        h = _layer(h, ln1_w[l], w_in[l], q_norm_w[l], k_norm_w[l], cmp_pe_k[l], cmp_pe_v[l],
                   cmp_wk1[l], cmp_wk2[l], cmp_wv1[l], cmp_wv2[l], ret_norm_w[l], w_out[l],
                   ln2_w[l], w_up[l], w_down[l])
    return h
```

```python
import functools
import math

import jax
import jax.numpy as jnp
import numpy as np
from jax import lax
from jax.experimental import pallas as pl
from jax.experimental.pallas import tpu as pltpu

F32 = jnp.float32
BF16 = jnp.bfloat16

NSA_HEADS = 8
NSA_KV_HEADS = 2
NSA_GROUP = NSA_HEADS // NSA_KV_HEADS
HEAD_DIM = 64
CMP_BLOCK = 32
CMP_STRIDE = 16
CMP_HIDDEN = 256
SEL_BLOCK = 64
SEL_TOPK = 8
SEL_FORCED = 3
WINDOW = 256
RET_HEADS = 4
RET_QK_DIM = 64
RET_V_DIM = 128
RET_CHUNK = 128
ROPE_THETA = 10000.0
EPS = 1e-6
NEG_INF = -1.0e30
SEL_BIAS = 2.0 ** 100

NSA_Q_DIM = NSA_HEADS * HEAD_DIM
NSA_KV_DIM = NSA_KV_HEADS * HEAD_DIM
NSA_GATE_DIM = 3 * NSA_HEADS
RET_QK_WIDTH = RET_HEADS * RET_QK_DIM
RET_V_WIDTH = RET_HEADS * RET_V_DIM

LANES = 128
VMEM_LIMIT_BYTES = 56 * 1024 * 1024

TM_PROJ = 512
TQ = 256
TK = 256
KV_CHUNK = 128
ONES_ROWS = 16
TM_FFN = 1024
FF_CHUNK = 512
RET_BATCH = 16


def _dot(a, b):
    return jnp.dot(a, b, preferred_element_type=F32)


def _dot_nt(a, b):
    return lax.dot_general(a, b, (((1,), (1,)), ((), ())), preferred_element_type=F32)


def _sigmoid(x):
    return 1.0 / (1.0 + jnp.exp(-x))


_K0 = 0
_V0 = _K0 + 3 * NSA_KV_DIM
_RQ0 = _V0 + NSA_KV_DIM + LANES
_RV0 = _RQ0 + RET_QK_WIDTH
_RG0 = _RV0 + RET_V_WIDTH
_W_IN_COLS = _RG0 + RET_V_WIDTH
_TQ0 = 0
_TV0 = _TQ0 + NSA_Q_DIM
_TRK0 = _TV0 + 2 * NSA_KV_DIM
_WT_ROWS = _TRK0 + RET_QK_WIDTH


def _permute_w_in(w_in):
    sizes = [NSA_Q_DIM] + [NSA_KV_DIM] * 6 + [NSA_GATE_DIM, RET_QK_WIDTH, RET_QK_WIDTH, RET_V_WIDTH, RET_V_WIDTH]
    off = [int(o) for o in np.concatenate([[0], np.cumsum(sizes)])]
    (q, kc, vc, ks, vs, kw, vw, gate, rq, rk, rv, rg) = [w_in[:, off[i]:off[i + 1]] for i in range(12)]
    d = w_in.shape[0]
    gate = gate.reshape(d, 3, NSA_KV_HEADS, NSA_GROUP).transpose(0, 2, 1, 3).reshape(d, NSA_KV_HEADS, 3 * NSA_GROUP)
    gate = jnp.pad(gate, ((0, 0), (0, 0), (0, HEAD_DIM - 3 * NSA_GROUP))).reshape(d, LANES)
    w = jnp.concatenate([kc, ks, kw, vc, gate, rq, rv, rg], axis=1)
    wt = jnp.concatenate([q, vs, vw, rk], axis=1).T
    assert w.shape[1] == _W_IN_COLS and wt.shape[0] == _WT_ROWS
    return w, wt


def _in_proj_kernel(x_ref, ln_ref, w_ref, wt_ref, cos_ref, sin_ref, cost_ref, sint_ref, qnt_ref, kn_ref,
                    q_ref, kc_ref, vc_ref, ks_ref, vs_ref, kw_ref, vw_ref, gate_ref,
                    rq_ref, rk_ref, rv_ref, rg_ref, *, tiles_per_seq):
    x = x_ref[...]
    tm = x.shape[0]
    ms = jnp.mean(x * x, axis=-1, keepdims=True)
    xn = (x * lax.rsqrt(ms + EPS) * ln_ref[...]).astype(BF16)
    cos = cos_ref[...]
    sin = sin_ref[...]
    lane = lax.broadcasted_iota(jnp.int32, (1, LANES), 1)
    low_half = (lane & (HEAD_DIM // 2)) == 0
    first_head = lane < HEAD_DIM

    def proj(a, b):
        return _dot(xn, w_ref[:, a:b])

    def rope(t):
        swapped = jnp.where(low_half, pltpu.roll(t, LANES - HEAD_DIM // 2, 1), pltpu.roll(t, HEAD_DIM // 2, 1))
        return t * cos + swapped * sin

    def head_norm(t, w):
        t2 = t * t
        s0 = jnp.sum(jnp.where(first_head, t2, 0.0), axis=-1, keepdims=True)
        s1 = jnp.sum(jnp.where(first_head, 0.0, t2), axis=-1, keepdims=True)
        msq = jnp.where(first_head, s0, s1) * (1.0 / HEAD_DIM)
        return t * lax.rsqrt(msq + EPS) * w

    def tiles(sec):
        return [sec[:, c * LANES:(c + 1) * LANES] for c in range(sec.shape[1] // LANES)]

    def split_heads(t, fill):
        return [jnp.where(first_head, t, fill), jnp.where(first_head, pltpu.roll(t, HEAD_DIM, 1), fill)]

    def proj_t(a, b):
        return _dot_nt(wt_ref[a:b, :], xn)

    cos_t = cost_ref[...]
    sin_t = sint_ref[...]

    def rope_t(t):
        half = HEAD_DIM // 2
        return t * cos_t + jnp.concatenate([t[half:], t[:half]], axis=0) * sin_t

    q_t = proj_t(_TQ0, _TV0)
    qn_t = qnt_ref[...]
    for h in range(NSA_HEADS):
        t = q_t[h * HEAD_DIM:(h + 1) * HEAD_DIM]
        t = t * lax.rsqrt(jnp.mean(t * t, axis=0, keepdims=True) + EPS) * qn_t
        t = rope_t(t) * (HEAD_DIM ** -0.5 * math.log2(math.e))
        q_ref[0, h // NSA_GROUP, h % NSA_GROUP] = t.astype(BF16)

    def store_values_t(v, out_ref):
        chunk = out_ref.shape[-1]
        ones = jnp.ones((ONES_ROWS, chunk), BF16)
        for c in range(tm // chunk):
            for g in range(NSA_KV_HEADS):
                out_ref[0, g, c, 0:HEAD_DIM, :] = v[g * HEAD_DIM:(g + 1) * HEAD_DIM,
                                                    c * chunk:(c + 1) * chunk].astype(BF16)
                out_ref[0, g, c, HEAD_DIM:HEAD_DIM + ONES_ROWS, :] = ones

    v_t = proj_t(_TV0, _TRK0)
    store_values_t(v_t[0:NSA_KV_DIM], vs_ref)
    store_values_t(v_t[NSA_KV_DIM:2 * NSA_KV_DIM], vw_ref)

    rk_t = proj_t(_TRK0, _WT_ROWS)
    for h in range(RET_HEADS):
        rows = slice(h * RET_QK_DIM, (h + 1) * RET_QK_DIM)
        t = (rope_t(rk_t[rows]) * (RET_QK_DIM ** -0.5)).astype(BF16)
        for j in range(tm // RET_CHUNK):
            rk_ref[0, j, rows, :] = t[:, j * RET_CHUNK:(j + 1) * RET_CHUNK]

    ksec = tiles(proj(_K0, _V0))
    kc = rope(head_norm(ksec[0], kn_ref[0:1, :]))
    ks = rope(head_norm(ksec[1], kn_ref[1:2, :]))
    kw = rope(head_norm(ksec[2], kn_ref[2:3, :])).astype(BF16)
    s0 = (pl.program_id(0) % tiles_per_seq) * tm
    blk = jnp.right_shift(s0 + lax.broadcasted_iota(jnp.int32, (tm, 1), 0), int(math.log2(SEL_BLOCK)))
    onehot = (lane - HEAD_DIM == blk).astype(F32)
    ks_aug = split_heads(ks, onehot)
    kc_ref[...] = kc
    for g in range(NSA_KV_HEADS):
        ks_ref[0, g] = ks_aug[g].astype(BF16)
        kw_ref[0, g] = kw[:, g * HEAD_DIM:(g + 1) * HEAD_DIM]

    vsec = tiles(proj(_V0, _RQ0))
    vc_ref[...] = vsec[0]
    gate_ref[...] = vsec[1]

    for c, t in enumerate(tiles(proj(_RQ0, _RV0))):
        rq_ref[:, c * LANES:(c + 1) * LANES] = rope(t).astype(BF16)
    rv_ref[...] = proj(_RV0, _RG0).astype(BF16)
    rg_ref[...] = proj(_RG0, _W_IN_COLS).astype(BF16)


def _in_proj(x2, ln1, w_perm, w_perm_t, cos, sin, cos_t, sin_t, qn_t, kn, B, S):
    N, D = x2.shape
    tm = TM_PROJ
    spt = S // tm
    G, R, dh = NSA_KV_HEADS, NSA_GROUP, HEAD_DIM

    def row(i):
        return (i, 0)

    def hm(i):
        return (i // spt, 0, i % spt, 0)

    hm_spec = pl.BlockSpec((1, G, tm, dh), hm)
    out_shape = [
        jax.ShapeDtypeStruct((B, G, R, dh, S), BF16),
        jax.ShapeDtypeStruct((N, LANES), F32),
        jax.ShapeDtypeStruct((N, LANES), F32),
        jax.ShapeDtypeStruct((B, G, S, LANES), BF16),
        jax.ShapeDtypeStruct((B, G, S // TK, dh + ONES_ROWS, TK), BF16),
        jax.ShapeDtypeStruct((B, G, S, dh), BF16),
        jax.ShapeDtypeStruct((B, G, S // KV_CHUNK, dh + ONES_ROWS, KV_CHUNK), BF16),
        jax.ShapeDtypeStruct((N, LANES), F32),
        jax.ShapeDtypeStruct((N, RET_QK_WIDTH), BF16),
        jax.ShapeDtypeStruct((B, S // RET_CHUNK, RET_QK_WIDTH, RET_CHUNK), BF16),
        jax.ShapeDtypeStruct((N, RET_V_WIDTH), BF16),
        jax.ShapeDtypeStruct((N, RET_V_WIDTH), BF16),
    ]
    out_specs = [
        pl.BlockSpec((1, G, R, dh, tm), lambda i: (i // spt, 0, 0, 0, i % spt)),
        pl.BlockSpec((tm, LANES), row), pl.BlockSpec((tm, LANES), row), pl.BlockSpec((1, G, tm, LANES), hm),
        pl.BlockSpec((1, G, tm // TK, dh + ONES_ROWS, TK), lambda i: (i // spt, 0, i % spt, 0, 0)),
        hm_spec,
        pl.BlockSpec((1, G, tm // KV_CHUNK, dh + ONES_ROWS, KV_CHUNK), lambda i: (i // spt, 0, i % spt, 0, 0)),
        pl.BlockSpec((tm, LANES), row),
        pl.BlockSpec((tm, RET_QK_WIDTH), row),
        pl.BlockSpec((1, tm // RET_CHUNK, RET_QK_WIDTH, RET_CHUNK), lambda i: (i // spt, i % spt, 0, 0)),
        pl.BlockSpec((tm, RET_V_WIDTH), row),
        pl.BlockSpec((tm, RET_V_WIDTH), row),
    ]
    const = lambda i: (0, 0)
    in_specs = [
        pl.BlockSpec((tm, D), row),
        pl.BlockSpec((1, D), const),
        pl.BlockSpec((D, _W_IN_COLS), const, pipeline_mode=pl.Buffered(1)),
        pl.BlockSpec((_WT_ROWS, D), const, pipeline_mode=pl.Buffered(1)),
        pl.BlockSpec((tm, LANES), lambda i: (i % spt, 0)),
        pl.BlockSpec((tm, LANES), lambda i: (i % spt, 0)),
        pl.BlockSpec((dh, tm), lambda i: (0, i % spt)),
        pl.BlockSpec((dh, tm), lambda i: (0, i % spt)),
        pl.BlockSpec((dh, tm), const),
        pl.BlockSpec((3, LANES), const),
    ]
    return pl.pallas_call(
        functools.partial(_in_proj_kernel, tiles_per_seq=spt),
        grid=(N // tm,),
        in_specs=in_specs,
        out_specs=out_specs,
        out_shape=out_shape,
        compiler_params=pltpu.CompilerParams(
            dimension_semantics=("parallel",), vmem_limit_bytes=VMEM_LIMIT_BYTES),
        name="in_proj",
    )(x2, ln1, w_perm, w_perm_t, cos, sin, cos_t, sin_t, qn_t, kn)


def _compress_kernel(k_ref, v_ref, pek_ref, pev_ref, wk1_ref, wk2_ref, wv1_ref, wv2_ref, ko_ref, vo_ref):
    n_seg = k_ref.shape[0] // CMP_STRIDE
    G = NSA_KV_HEADS
    assert G == 2
    first_group = lax.broadcasted_iota(jnp.int32, (1, LANES), 1) < HEAD_DIM

    def run(x_ref, pe_ref, w1_ref):
        acc = {}
        for p in range(CMP_STRIDE // 2):
            xa = x_ref[pl.ds(2 * p, n_seg, stride=CMP_STRIDE), :]
            xb = x_ref[pl.ds(2 * p + 1, n_seg, stride=CMP_STRIDE), :]
            xa_r = pltpu.roll(xa, HEAD_DIM, 1)
            xb_r = pltpu.roll(xb, HEAD_DIM, 1)
            for half in range(2):
                l0 = half * CMP_STRIDE + 2 * p
                pe_a = pe_ref[l0:l0 + 1, :]
                pe_b = pe_ref[l0 + 1:l0 + 2, :]
                w = w1_ref[l0 * HEAD_DIM:(l0 + 2) * HEAD_DIM, :]
                pairs = [jnp.where(first_group, xa + pe_a, xb_r + pe_b),
                         jnp.where(first_group, xa_r + pe_a, xb + pe_b)]
                for g in range(G):
                    d = _dot(pairs[g].astype(BF16), w)
                    acc[half, g] = d if p == 0 else acc[half, g] + d
        out = []
        for g in range(G):
            h = acc[0, g] + pltpu.roll(acc[1, g], n_seg - 1, 0)
            out.append((h * _sigmoid(h)).astype(BF16))
        return out

    hk = run(k_ref, pek_ref, wk1_ref)
    hv = run(v_ref, pev_ref, wv1_ref)
    for g in range(G):
        ko_ref[0, g] = _dot(hk[g], wk2_ref[...]).astype(BF16)
        vo_ref[:, g * n_seg:(g + 1) * n_seg] = _dot(hv[g], wv2_ref[...]).T[0:HEAD_DIM, :].astype(BF16)


def _compress(kc, vc, pek, pev, wk1, wk2, wv1, wv2, B, S):
    G = NSA_KV_HEADS
    n_seg = S // CMP_STRIDE
    width = kc.shape[1]
    const2 = lambda b: (0, 0)
    tok_spec = pl.BlockSpec((S, width), lambda b: (b, 0))
    return pl.pallas_call(
        _compress_kernel,
        grid=(B,),
        in_specs=[tok_spec, tok_spec,
                  pl.BlockSpec(pek.shape, const2), pl.BlockSpec(pev.shape, const2),
                  pl.BlockSpec(wk1.shape, const2), pl.BlockSpec(wk2.shape, const2),
                  pl.BlockSpec(wv1.shape, const2), pl.BlockSpec(wv2.shape, const2)],
        out_specs=[pl.BlockSpec((1, G, n_seg, HEAD_DIM), lambda b: (b, 0, 0, 0)),
                   pl.BlockSpec((HEAD_DIM, G * n_seg), lambda b: (0, b))],
        out_shape=[jax.ShapeDtypeStruct((B, G, n_seg, HEAD_DIM), BF16),
                   jax.ShapeDtypeStruct((HEAD_DIM, B * G * n_seg), BF16)],
        compiler_params=pltpu.CompilerParams(
            dimension_semantics=("parallel",), vmem_limit_bytes=VMEM_LIMIT_BYTES),
        name="compress",
    )(kc, vc, pek, pev, wk1, wk2, wv1, wv2)


def _nsa_kernel(q_ref, kcmp_ref, vcmp_ref, ks_ref, vs_ref, kw_ref, vw_ref, gate_ref,
                o_ref, m_ref, acc_ref, s0_ref):
    G, R = NSA_KV_HEADS, NSA_GROUP
    n_cp = kcmp_ref.shape[2]
    S = ks_ref.shape[2]
    n_cmp = (S - CMP_BLOCK) // CMP_STRIDE + 1
    n_blk = S // SEL_BLOCK
    qi = pl.program_id(1)
    q0 = qi * TQ
    t_q = q0 + lax.broadcasted_iota(jnp.int32, (1, TQ), 1)
    qs = [jnp.concatenate([q_ref[0, g, r] for r in range(R)], axis=1) for g in range(G)]

    def heads(x):
        return [x[:, r * TQ:(r + 1) * TQ] for r in range(R)]

    def masked_exp(s_heads, mask):
        es = []
        for s in s_heads:
            s = jnp.where(mask, s, NEG_INF)
            es.append(jnp.exp2(s - jnp.max(s, axis=0, keepdims=True)))
        return es

    c_idx = lax.broadcasted_iota(jnp.int32, (n_cp, TQ), 0)
    cmask = ((c_idx * CMP_STRIDE + (CMP_BLOCK - 1)) <= t_q) & (c_idx < n_cmp)
    jj = lax.broadcasted_iota(jnp.int32, (n_blk, n_cp), 0)
    cc = lax.broadcasted_iota(jnp.int32, (n_blk, n_cp), 1)
    overlap = ((cc * CMP_STRIDE < (jj + 1) * SEL_BLOCK) & (cc * CMP_STRIDE + CMP_BLOCK > jj * SEL_BLOCK)
               & (cc < n_cmp)).astype(BF16)
    jb = lax.broadcasted_iota(jnp.int32, (n_blk, TQ), 0)
    jb_f = jb.astype(F32)
    cur = jnp.right_shift(t_q, int(math.log2(SEL_BLOCK)))
    forced = (jb == 0) | (jb == cur) | (jb == cur - 1)
    valid = jb <= cur

    n_sub = TQ // KV_CHUNK
    n_wc = (KV_CHUNK + WINDOW) // KV_CHUNK
    win_c0, win_mask, win_q = [], [], []
    for h in range(n_sub):
        c0 = jnp.maximum(q0 // KV_CHUNK + h - WINDOW // KV_CHUNK, 0)
        kpos = c0 * KV_CHUNK + lax.broadcasted_iota(jnp.int32, (n_wc * KV_CHUNK, KV_CHUNK), 0)
        diff = t_q[:, h * KV_CHUNK:(h + 1) * KV_CHUNK] - kpos
        win_c0.append(c0)
        win_mask.append((diff >= 0) & (diff < WINDOW))
        win_q.append([jnp.concatenate([qs[g][:, r * TQ + h * KV_CHUNK:r * TQ + (h + 1) * KV_CHUNK]
                                       for r in range(R)], axis=1) for g in range(G)])

    def window_keys(g, h):
        return kw_ref[0, g, pl.ds(pl.multiple_of(win_c0[h] * KV_CHUNK, KV_CHUNK), n_wc * KV_CHUNK), :]

    s_cmp = [_dot(kcmp_ref[0, g], qs[g]) for g in range(G)]
    s_win = [[_dot(window_keys(g, h), win_q[h][g]) for h in range(n_sub)] for g in range(G)]
    any_cmp = t_q >= CMP_BLOCK - 1
    p_cmp, o_cmp, o_win = [], [], []
    for g in range(G):
        es = masked_exp(heads(s_cmp[g]), cmask)
        p_cmp.append([e * jnp.where(any_cmp, 1.0 / jnp.sum(e, axis=0, keepdims=True), 0.0) for e in es])

    sels = []
    for g in range(G):
        p_c = p_cmp[g]
        p_sum = p_c[0]
        for r in range(1, R):
            p_sum = p_sum + p_c[r]
        p_hi = p_sum.astype(BF16)
        p_lo = (p_sum - p_hi.astype(F32)).astype(BF16)
        p_slc = _dot(overlap, p_hi) + _dot(overlap, p_lo)
        work = jnp.where(valid & jnp.logical_not(forced), p_slc, -1.0)
        picked = forced
        for _ in range(min(SEL_TOPK, n_blk) - SEL_FORCED):
            best = jnp.max(work, axis=0, keepdims=True)
            first = jnp.min(jnp.where(work == best, jb_f, float(n_blk)), axis=0, keepdims=True)
            hit = jb_f == first
            picked = picked | hit
            work = jnp.where(hit, -2.0, work)
        bias = jnp.where(picked, 0.0, -SEL_BIAS).astype(BF16)
        sels.append(jnp.concatenate([qs[g], jnp.concatenate([bias] * R, axis=1),
                                     jnp.zeros((LANES - HEAD_DIM - n_blk, R * TQ), BF16)], axis=0))

    m_ref[...] = jnp.full(m_ref.shape, NEG_INF, F32)
    acc_ref[...] = jnp.zeros(acc_ref.shape, F32)

    def scores(g, kt):
        return _dot(ks_ref[0, g, pl.ds(pl.multiple_of(kt * TK, TK), TK), :], sels[g])

    def accumulate(g, kt, s, causal):
        if causal is not None:
            s = jnp.concatenate([jnp.where(causal, sh, NEG_INF) for sh in heads(s)], axis=1)
        m_old = m_ref[g]
        m_new = jnp.maximum(m_old, jnp.max(s, axis=0, keepdims=True))
        alpha = jnp.exp2(m_old - m_new)
        p = jnp.exp2(s - m_new)
        acc_ref[g] = alpha * acc_ref[g] + _dot(vs_ref[0, g, kt], p.astype(BF16))
        m_ref[g] = m_new

    assert G == 2
    s0_ref[...] = scores(0, 0)

    for g in range(G):
        o_cmp.append(_dot(vcmp_ref[:, g * n_cp:(g + 1) * n_cp],
                          jnp.concatenate(p_cmp[g], axis=1).astype(BF16)))
    for g in range(G):
        sub = []
        for h in range(n_sub):
            s_heads = [s_win[g][h][:, r * KV_CHUNK:(r + 1) * KV_CHUNK] for r in range(R)]
            e_w = jnp.concatenate(masked_exp(s_heads, win_mask[h]), axis=1).astype(BF16)
            v_w = jnp.concatenate([vw_ref[0, g, win_c0[h] + j] for j in range(n_wc)], axis=1)
            sub.append(_dot(v_w, e_w))
        o_win.append(jnp.concatenate([sub[h][:, r * KV_CHUNK:(r + 1) * KV_CHUNK]
                                      for r in range(R) for h in range(n_sub)], axis=1))

    def interior_tile(kt, carry):
        s1 = scores(1, kt)
        accumulate(0, kt, s0_ref[...], None)
        s0_ref[...] = scores(0, kt + 1)
        accumulate(1, kt, s1, None)
        return carry

    lax.fori_loop(0, qi, interior_tile, 0)
    causal = (q0 + lax.broadcasted_iota(jnp.int32, (TK, TQ), 0)) <= t_q
    s1 = scores(1, qi)
    accumulate(0, qi, s0_ref[...], causal)
    accumulate(1, qi, s1, causal)

    gates = _sigmoid(gate_ref[0].T)
    outs = []
    for g in range(G):
        o_sel = acc_ref[g]
        for r, (oc, os_, ow) in enumerate(zip(heads(o_cmp[g]), heads(o_sel), heads(o_win[g]))):
            g0 = g * HEAD_DIM + r
            g_sel = gates[g0 + R:g0 + R + 1] * (1.0 / os_[HEAD_DIM:HEAD_DIM + 1])
            g_win = gates[g0 + 2 * R:g0 + 2 * R + 1] * (1.0 / ow[HEAD_DIM:HEAD_DIM + 1])
            outs.append(gates[g0:g0 + 1] * oc + g_sel * os_[0:HEAD_DIM] + g_win * ow[0:HEAD_DIM])
    o_ref[0] = jnp.concatenate(outs, axis=0).T.astype(BF16)


def _nsa(q, kcmp, vcmp, ks, vs, kw, vw, gates):
    B, G, R, dh, S = q.shape
    n_cp = kcmp.shape[2]
    assert TQ == TK, "the key sweep treats exactly one tile per query tile as the diagonal"
    k_spec = lambda a: pl.BlockSpec((1,) + a.shape[1:], lambda b, i: (b, 0, 0, 0))
    vt_spec = lambda a: pl.BlockSpec((1,) + a.shape[1:], lambda b, i: (b, 0, 0, 0, 0))
    return pl.pallas_call(
        _nsa_kernel,
        grid=(B, S // TQ),
        in_specs=[
            pl.BlockSpec((1, G, R, dh, TQ), lambda b, i: (b, 0, 0, 0, i)),
            pl.BlockSpec((1, G, n_cp, dh), lambda b, i: (b, 0, 0, 0)),
            pl.BlockSpec((dh, G * n_cp), lambda b, i: (0, b)),
            k_spec(ks), vt_spec(vs), k_spec(kw), vt_spec(vw),
            pl.BlockSpec((1, TQ, LANES), lambda b, i: (b, i, 0)),
        ],
        out_specs=pl.BlockSpec((1, TQ, G * R * dh), lambda b, i: (b, i, 0)),
        out_shape=jax.ShapeDtypeStruct((B, S, G * R * dh), BF16),
        scratch_shapes=[pltpu.VMEM((G, 1, R * TQ), F32), pltpu.VMEM((G, dh + ONES_ROWS, R * TQ), F32),
                        pltpu.VMEM((TK, R * TQ), F32)],
        compiler_params=pltpu.CompilerParams(
            dimension_semantics=("parallel", "arbitrary"), vmem_limit_bytes=VMEM_LIMIT_BYTES),
        name="nsa_attention",
    )(q, kcmp, vcmp, ks, vs, kw, vw, gates)


def _retention_kernel(q_ref, kt_ref, v_ref, g_ref, w_ref, o_ref, state_ref):
    C = RET_CHUNK
    NB = q_ref.shape[0]

    @pl.when(pl.program_id(1) == 0)
    def _():
        state_ref[...] = jnp.zeros(state_ref.shape, F32)

    i_col = lax.broadcasted_iota(jnp.int32, (C, 1), 0)
    i_row = lax.broadcasted_iota(jnp.int32, (1, C), 1)
    d_int = i_col - i_row
    log_gamma = [math.log(1.0 - 2.0 ** (-5.0 - h)) for h in range(RET_HEADS)]
    dmat = [jnp.where(d_int >= 0, jnp.exp(lg * jnp.maximum(d_int, 0).astype(F32)), 0.0) for lg in log_gamma]
    xi = [jnp.exp(lg * (i_col + 1).astype(F32)) for lg in log_gamma]
    zeta = [jnp.exp(lg * (C - 1 - i_row).astype(F32)) for lg in log_gamma]
    gamma_c = [math.exp(lg * C) for lg in log_gamma]

    units = [(n, h) for n in range(NB) for h in range(RET_HEADS)]
    q, kt, v, st = {}, {}, {}, {}
    for n, h in units:
        q[n, h] = q_ref[n, :, h * RET_QK_DIM:(h + 1) * RET_QK_DIM]
        kt[n, h] = kt_ref[n, 0, h * RET_QK_DIM:(h + 1) * RET_QK_DIM, :]
        v[n, h] = v_ref[n, :, h * RET_V_DIM:(h + 1) * RET_V_DIM]
        st[n, h] = state_ref[n, h]
    inner = {u: _dot(q[u], kt[u]) for u in units}
    cross = {u: _dot(q[u], st[u].astype(BF16)) for u in units}
    kv = {u: _dot((kt[u].astype(F32) * zeta[u[1]]).astype(BF16), v[u]) for u in units}
    for n, h in units:
        u = (n, h)
        vsl = slice(h * RET_V_DIM, (h + 1) * RET_V_DIM)
        y = _dot((inner[u] * dmat[h]).astype(BF16), v[u]) + cross[u] * xi[h]
        state_ref[n, h] = gamma_c[h] * st[u] + kv[u]

        mu = jnp.mean(y, axis=-1, keepdims=True)
        yc = y - mu
        var = jnp.mean(yc * yc, axis=-1, keepdims=True)
        yn = yc * lax.rsqrt(var + EPS) * w_ref[:, vsl]
        gate = g_ref[n, :, vsl].astype(F32)
        o_ref[n, :, vsl] = (gate * _sigmoid(gate) * yn).astype(BF16)


def _retention(rq, rkt, rv, rg, w):
    B, S, _ = rq.shape
    C = RET_CHUNK
    nb = math.gcd(RET_BATCH, B)
    q_spec = pl.BlockSpec((nb, C, RET_QK_WIDTH), lambda b, c: (b, c, 0))
    kt_spec = pl.BlockSpec((nb, 1, RET_QK_WIDTH, C), lambda b, c: (b, c, 0, 0))
    v_spec = pl.BlockSpec((nb, C, RET_V_WIDTH), lambda b, c: (b, c, 0))
    return pl.pallas_call(
        _retention_kernel,
        grid=(B // nb, S // C),
        in_specs=[q_spec, kt_spec, v_spec, v_spec, pl.BlockSpec((1, RET_V_WIDTH), lambda b, c: (0, 0))],
        out_specs=v_spec,
        out_shape=jax.ShapeDtypeStruct((B, S, RET_V_WIDTH), BF16),
        scratch_shapes=[pltpu.VMEM((nb, RET_HEADS, RET_QK_DIM, RET_V_DIM), F32)],
        compiler_params=pltpu.CompilerParams(
            dimension_semantics=("parallel", "arbitrary"), vmem_limit_bytes=VMEM_LIMIT_BYTES),
        name="retention",
    )(rq, rkt, rv, rg, w)


def _out_ffn_kernel(x_ref, a_ref, r_ref, wo_ref, ln_ref, wu_ref, wd_ref, o_ref):
    na = a_ref.shape[1]
    mix = _dot(a_ref[...], wo_ref[0:na, :]) + _dot(r_ref[...], wo_ref[na:, :])
    h = x_ref[...] + mix
    ms = jnp.mean(h * h, axis=-1, keepdims=True)
    hn = (h * lax.rsqrt(ms + EPS) * ln_ref[...]).astype(BF16)
    d_ff = wu_ref.shape[1]
    acc = None
    for f in range(d_ff // FF_CHUNK):
        cols = slice(f * FF_CHUNK, (f + 1) * FF_CHUNK)
        u = jnp.maximum(_dot(hn, wu_ref[:, cols]), 0.0)
        d = _dot((u * u).astype(BF16), wd_ref[cols, :])
        acc = d if acc is None else acc + d
    o_ref[...] = h + acc


def _out_ffn(x2, o_nsa, o_ret, w_out, ln2, w_up, w_down):
    N, D = x2.shape
    tm = TM_FFN
    row = lambda i: (i, 0)
    const = lambda i: (0, 0)
    resident = functools.partial(pl.BlockSpec, index_map=const, pipeline_mode=pl.Buffered(1))
    return pl.pallas_call(
        _out_ffn_kernel,
        grid=(N // tm,),
        in_specs=[
            pl.BlockSpec((tm, D), row),
            pl.BlockSpec((tm, o_nsa.shape[1]), row),
            pl.BlockSpec((tm, o_ret.shape[1]), row),
            resident(w_out.shape),
            pl.BlockSpec((1, D), const),
            resident(w_up.shape),
            resident(w_down.shape),
        ],
        out_specs=pl.BlockSpec((tm, D), row),
        out_shape=jax.ShapeDtypeStruct((N, D), F32),
        compiler_params=pltpu.CompilerParams(
            dimension_semantics=("parallel",), vmem_limit_bytes=VMEM_LIMIT_BYTES),
        name="out_ffn",
    )(x2, o_nsa, o_ret, w_out, ln2, w_up, w_down)


def _rope_tables(S):
    half = HEAD_DIM // 2
    inv = ROPE_THETA ** (-jnp.arange(half, dtype=F32) / half)
    ang = jnp.arange(S).astype(F32)[:, None] * inv[None, :]
    cos, sin = jnp.cos(ang), jnp.sin(ang)
    reps = LANES // HEAD_DIM
    cos_h = jnp.concatenate([cos, cos], axis=-1)
    sin_h = jnp.concatenate([-sin, sin], axis=-1)
    return jnp.tile(cos_h, (1, reps)), jnp.tile(sin_h, (1, reps)), cos_h.T, sin_h.T


def _layer(h, ln1_w, w_in, q_norm_w, k_norm_w, cmp_pe_k, cmp_pe_v, cmp_wk1, cmp_wk2,
           cmp_wv1, cmp_wv2, ret_norm_w, w_out, ln2_w, w_up, w_down):
    B, S, D = h.shape
    N = B * S
    G = NSA_KV_HEADS
    x2 = h.reshape(N, D)

    w_perm, w_perm_t = _permute_w_in(w_in)
    cos_l, sin_l, cos_t, sin_t = _rope_tables(S)
    reps = LANES // HEAD_DIM
    qn_t = jnp.broadcast_to(q_norm_w[:, None], (HEAD_DIM, TM_PROJ))
    kn = jnp.tile(k_norm_w, (1, reps))

    (q, kc, vc, ks, vs, kw, vw, gates, rq, rk, rv, rg) = _in_proj(
        x2, ln1_w[None, :], w_perm.astype(BF16), w_perm_t.astype(BF16), cos_l, sin_l, cos_t, sin_t, qn_t, kn, B, S)

    kcmp, vcmp = _compress(
        kc, vc, jnp.tile(cmp_pe_k, (1, G)), jnp.tile(cmp_pe_v, (1, G)),
        cmp_wk1.astype(BF16), cmp_wk2.astype(BF16), cmp_wv1.astype(BF16),
        jnp.pad(cmp_wv2, ((0, 0), (0, LANES - HEAD_DIM))).astype(BF16), B, S)

    o_nsa = _nsa(q, kcmp, vcmp, ks, vs, kw, vw, gates.reshape(B, S, LANES))
    o_ret = _retention(rq.reshape(B, S, -1), rk, rv.reshape(B, S, -1),
                       rg.reshape(B, S, -1), ret_norm_w.reshape(1, RET_V_WIDTH))

    out = _out_ffn(x2, o_nsa.reshape(N, -1), o_ret.reshape(N, -1), w_out.astype(BF16),
                   ln2_w[None, :], w_up.astype(BF16), w_down.astype(BF16))
    return out.reshape(B, S, D)


def kernel(x, ln1_w, w_in, q_norm_w, k_norm_w, cmp_pe_k, cmp_pe_v, cmp_wk1, cmp_wk2, cmp_wv1, cmp_wv2,
           ret_norm_w, w_out, ln2_w, w_up, w_down):
    h = x
    for l in range(ln1_w.shape[0]):
        h = _layer(h, ln1_w[l], w_in[l], q_norm_w[l], k_norm_w[l], cmp_pe_k[l], cmp_pe_v[l],
                   cmp_wk1[l], cmp_wk2[l], cmp_wv1[l], cmp_wv2[l], ret_norm_w[l], w_out[l],
                   ln2_w[l], w_up[l], w_down[l])
    return h
```

```python
import functools
import math

import jax
import jax.numpy as jnp
import numpy as np
from jax import lax
from jax.experimental import pallas as pl
from jax.experimental.pallas import tpu as pltpu

F32 = jnp.float32
BF16 = jnp.bfloat16

NSA_HEADS = 8
NSA_KV_HEADS = 2
NSA_GROUP = NSA_HEADS // NSA_KV_HEADS
HEAD_DIM = 64
CMP_BLOCK = 32
CMP_STRIDE = 16
CMP_HIDDEN = 256
SEL_BLOCK = 64
SEL_TOPK = 8
SEL_FORCED = 3
WINDOW = 256
RET_HEADS = 4
RET_QK_DIM = 64
RET_V_DIM = 128
RET_CHUNK = 128
ROPE_THETA = 10000.0
EPS = 1e-6
NEG_INF = -1.0e30
SEL_BIAS = 2.0 ** 100

NSA_Q_DIM = NSA_HEADS * HEAD_DIM
NSA_KV_DIM = NSA_KV_HEADS * HEAD_DIM
NSA_GATE_DIM = 3 * NSA_HEADS
RET_QK_WIDTH = RET_HEADS * RET_QK_DIM
RET_V_WIDTH = RET_HEADS * RET_V_DIM

LANES = 128
VMEM_LIMIT_BYTES = 56 * 1024 * 1024

TM_PROJ = 512
TQ = 256
TK = 256
KV_CHUNK = 128
ONES_ROWS = 16
TM_FFN = 1024
FF_CHUNK = 512
RET_BATCH = 16


def _dot(a, b):
    return jnp.dot(a, b, preferred_element_type=F32)


def _dot_nt(a, b):
    return lax.dot_general(a, b, (((1,), (1,)), ((), ())), preferred_element_type=F32)


def _sigmoid(x):
    return 1.0 / (1.0 + jnp.exp(-x))


_K0 = 0
_V0 = _K0 + 3 * NSA_KV_DIM
_RQ0 = _V0 + NSA_KV_DIM + LANES
_RV0 = _RQ0 + RET_QK_WIDTH
_RG0 = _RV0 + RET_V_WIDTH
_W_IN_COLS = _RG0 + RET_V_WIDTH
_TQ0 = 0
_TV0 = _TQ0 + NSA_Q_DIM
_TRK0 = _TV0 + 2 * NSA_KV_DIM
_WT_ROWS = _TRK0 + RET_QK_WIDTH


def _permute_w_in(w_in):
    sizes = [NSA_Q_DIM] + [NSA_KV_DIM] * 6 + [NSA_GATE_DIM, RET_QK_WIDTH, RET_QK_WIDTH, RET_V_WIDTH, RET_V_WIDTH]
    off = [int(o) for o in np.concatenate([[0], np.cumsum(sizes)])]
    (q, kc, vc, ks, vs, kw, vw, gate, rq, rk, rv, rg) = [w_in[:, off[i]:off[i + 1]] for i in range(12)]
    d = w_in.shape[0]
    gate = gate.reshape(d, 3, NSA_KV_HEADS, NSA_GROUP).transpose(0, 2, 1, 3).reshape(d, NSA_KV_HEADS, 3 * NSA_GROUP)
    gate = jnp.pad(gate, ((0, 0), (0, 0), (0, HEAD_DIM - 3 * NSA_GROUP))).reshape(d, LANES)
    w = jnp.concatenate([kc, ks, kw, vc, gate, rq, rv, rg], axis=1)
    wt = jnp.concatenate([q, vs, vw, rk], axis=1).T
    assert w.shape[1] == _W_IN_COLS and wt.shape[0] == _WT_ROWS
    return w, wt


def _in_proj_kernel(x_ref, ln_ref, w_ref, wt_ref, cos_ref, sin_ref, cost_ref, sint_ref, qnt_ref, kn_ref,
                    q_ref, kc_ref, vc_ref, ks_ref, vs_ref, kw_ref, vw_ref, gate_ref,
                    rq_ref, rk_ref, rv_ref, rg_ref, *, tiles_per_seq):
    x = x_ref[...]
    tm = x.shape[0]
    ms = jnp.mean(x * x, axis=-1, keepdims=True)
    xn = (x * lax.rsqrt(ms + EPS) * ln_ref[...]).astype(BF16)
    cos = cos_ref[...]
    sin = sin_ref[...]
    lane = lax.broadcasted_iota(jnp.int32, (1, LANES), 1)
    low_half = (lane & (HEAD_DIM // 2)) == 0
    first_head = lane < HEAD_DIM

    def proj(a, b):
        return _dot(xn, w_ref[:, a:b])

    def rope(t):
        swapped = jnp.where(low_half, pltpu.roll(t, LANES - HEAD_DIM // 2, 1), pltpu.roll(t, HEAD_DIM // 2, 1))
        return t * cos + swapped * sin

    def head_norm(t, w):
        t2 = t * t
        s0 = jnp.sum(jnp.where(first_head, t2, 0.0), axis=-1, keepdims=True)
        s1 = jnp.sum(jnp.where(first_head, 0.0, t2), axis=-1, keepdims=True)
        msq = jnp.where(first_head, s0, s1) * (1.0 / HEAD_DIM)
        return t * lax.rsqrt(msq + EPS) * w

    def tiles(sec):
        return [sec[:, c * LANES:(c + 1) * LANES] for c in range(sec.shape[1] // LANES)]

    def split_heads(t, fill):
        return [jnp.where(first_head, t, fill), jnp.where(first_head, pltpu.roll(t, HEAD_DIM, 1), fill)]

    def proj_t(a, b):
        return _dot_nt(wt_ref[a:b, :], xn)

    cos_t = cost_ref[...]
    sin_t = sint_ref[...]

    def rope_t(t):
        half = HEAD_DIM // 2
        return t * cos_t + jnp.concatenate([t[half:], t[:half]], axis=0) * sin_t

    q_t = proj_t(_TQ0, _TV0)
    qn_t = qnt_ref[...]
    for h in range(NSA_HEADS):
        t = q_t[h * HEAD_DIM:(h + 1) * HEAD_DIM]
        t = t * lax.rsqrt(jnp.mean(t * t, axis=0, keepdims=True) + EPS) * qn_t
        t = rope_t(t) * (HEAD_DIM ** -0.5 * math.log2(math.e))
        q_ref[0, h // NSA_GROUP, h % NSA_GROUP] = t.astype(BF16)

    def store_values_t(v, out_ref):
        chunk = out_ref.shape[-1]
        ones = jnp.ones((ONES_ROWS, chunk), BF16)
        for c in range(tm // chunk):
            for g in range(NSA_KV_HEADS):
                out_ref[0, g, c, 0:HEAD_DIM, :] = v[g * HEAD_DIM:(g + 1) * HEAD_DIM,
                                                    c * chunk:(c + 1) * chunk].astype(BF16)
                out_ref[0, g, c, HEAD_DIM:HEAD_DIM + ONES_ROWS, :] = ones

    v_t = proj_t(_TV0, _TRK0)
    store_values_t(v_t[0:NSA_KV_DIM], vs_ref)
    store_values_t(v_t[NSA_KV_DIM:2 * NSA_KV_DIM], vw_ref)

    rk_t = proj_t(_TRK0, _WT_ROWS)
    for h in range(RET_HEADS):
        rows = slice(h * RET_QK_DIM, (h + 1) * RET_QK_DIM)
        t = (rope_t(rk_t[rows]) * (RET_QK_DIM ** -0.5)).astype(BF16)
        for j in range(tm // RET_CHUNK):
            rk_ref[0, j, rows, :] = t[:, j * RET_CHUNK:(j + 1) * RET_CHUNK]

    ksec = tiles(proj(_K0, _V0))
    kc = rope(head_norm(ksec[0], kn_ref[0:1, :]))
    ks = rope(head_norm(ksec[1], kn_ref[1:2, :]))
    kw = rope(head_norm(ksec[2], kn_ref[2:3, :])).astype(BF16)
    s0 = (pl.program_id(0) % tiles_per_seq) * tm
    blk = jnp.right_shift(s0 + lax.broadcasted_iota(jnp.int32, (tm, 1), 0), int(math.log2(SEL_BLOCK)))
    onehot = (lane - HEAD_DIM == blk).astype(F32)
    ks_aug = split_heads(ks, onehot)
    kc_ref[...] = kc
    for g in range(NSA_KV_HEADS):
        ks_ref[0, g] = ks_aug[g].astype(BF16)
        kw_ref[0, g] = kw[:, g * HEAD_DIM:(g + 1) * HEAD_DIM]

    vsec = tiles(proj(_V0, _RQ0))
    vc_ref[...] = vsec[0]
    gate_ref[...] = vsec[1]

    for c, t in enumerate(tiles(proj(_RQ0, _RV0))):
        rq_ref[:, c * LANES:(c + 1) * LANES] = rope(t).astype(BF16)
    rv_ref[...] = proj(_RV0, _RG0).astype(BF16)
    rg_ref[...] = proj(_RG0, _W_IN_COLS).astype(BF16)


def _in_proj(x2, ln1, w_perm, w_perm_t, cos, sin, cos_t, sin_t, qn_t, kn, B, S):
    N, D = x2.shape
    tm = TM_PROJ
    spt = S // tm
    G, R, dh = NSA_KV_HEADS, NSA_GROUP, HEAD_DIM

    def row(i):
        return (i, 0)

    def hm(i):
        return (i // spt, 0, i % spt, 0)

    hm_spec = pl.BlockSpec((1, G, tm, dh), hm)
    out_shape = [
        jax.ShapeDtypeStruct((B, G, R, dh, S), BF16),
        jax.ShapeDtypeStruct((N, LANES), F32),
        jax.ShapeDtypeStruct((N, LANES), F32),
        jax.ShapeDtypeStruct((B, G, S, LANES), BF16),
        jax.ShapeDtypeStruct((B, G, S // TK, dh + ONES_ROWS, TK), BF16),
        jax.ShapeDtypeStruct((B, G, S, dh), BF16),
        jax.ShapeDtypeStruct((B, G, S // KV_CHUNK, dh + ONES_ROWS, KV_CHUNK), BF16),
        jax.ShapeDtypeStruct((N, LANES), F32),
        jax.ShapeDtypeStruct((N, RET_QK_WIDTH), BF16),
        jax.ShapeDtypeStruct((B, S // RET_CHUNK, RET_QK_WIDTH, RET_CHUNK), BF16),
        jax.ShapeDtypeStruct((N, RET_V_WIDTH), BF16),
        jax.ShapeDtypeStruct((N, RET_V_WIDTH), BF16),
    ]
    out_specs = [
        pl.BlockSpec((1, G, R, dh, tm), lambda i: (i // spt, 0, 0, 0, i % spt)),
        pl.BlockSpec((tm, LANES), row), pl.BlockSpec((tm, LANES), row), pl.BlockSpec((1, G, tm, LANES), hm),
        pl.BlockSpec((1, G, tm // TK, dh + ONES_ROWS, TK), lambda i: (i // spt, 0, i % spt, 0, 0)),
        hm_spec,
        pl.BlockSpec((1, G, tm // KV_CHUNK, dh + ONES_ROWS, KV_CHUNK), lambda i: (i // spt, 0, i % spt, 0, 0)),
        pl.BlockSpec((tm, LANES), row),
        pl.BlockSpec((tm, RET_QK_WIDTH), row),
        pl.BlockSpec((1, tm // RET_CHUNK, RET_QK_WIDTH, RET_CHUNK), lambda i: (i // spt, i % spt, 0, 0)),
        pl.BlockSpec((tm, RET_V_WIDTH), row),
        pl.BlockSpec((tm, RET_V_WIDTH), row),
    ]
    const = lambda i: (0, 0)
    in_specs = [
        pl.BlockSpec((tm, D), row),
        pl.BlockSpec((1, D), const),
        pl.BlockSpec((D, _W_IN_COLS), const, pipeline_mode=pl.Buffered(1)),
        pl.BlockSpec((_WT_ROWS, D), const, pipeline_mode=pl.Buffered(1)),
        pl.BlockSpec((tm, LANES), lambda i: (i % spt, 0)),
        pl.BlockSpec((tm, LANES), lambda i: (i % spt, 0)),
        pl.BlockSpec((dh, tm), lambda i: (0, i % spt)),
        pl.BlockSpec((dh, tm), lambda i: (0, i % spt)),
        pl.BlockSpec((dh, tm), const),
        pl.BlockSpec((3, LANES), const),
    ]
    return pl.pallas_call(
        functools.partial(_in_proj_kernel, tiles_per_seq=spt),
        grid=(N // tm,),
        in_specs=in_specs,
        out_specs=out_specs,
        out_shape=out_shape,
        compiler_params=pltpu.CompilerParams(
            dimension_semantics=("parallel",), vmem_limit_bytes=VMEM_LIMIT_BYTES),
        name="in_proj",
    )(x2, ln1, w_perm, w_perm_t, cos, sin, cos_t, sin_t, qn_t, kn)


def _compress_kernel(k_ref, v_ref, pek_ref, pev_ref, wk1_ref, wk2_ref, wv1_ref, wv2_ref, ko_ref, vo_ref):
    n_seg = k_ref.shape[0] // CMP_STRIDE
    G = NSA_KV_HEADS
    assert G == 2
    first_group = lax.broadcasted_iota(jnp.int32, (1, LANES), 1) < HEAD_DIM

    def run(x_ref, pe_ref, w1_ref):
        acc = {}
        for p in range(CMP_STRIDE // 2):
            xa = x_ref[pl.ds(2 * p, n_seg, stride=CMP_STRIDE), :]
            xb = x_ref[pl.ds(2 * p + 1, n_seg, stride=CMP_STRIDE), :]
            xa_r = pltpu.roll(xa, HEAD_DIM, 1)
            xb_r = pltpu.roll(xb, HEAD_DIM, 1)
            for half in range(2):
                l0 = half * CMP_STRIDE + 2 * p
                pe_a = pe_ref[l0:l0 + 1, :]
                pe_b = pe_ref[l0 + 1:l0 + 2, :]
                w = w1_ref[l0 * HEAD_DIM:(l0 + 2) * HEAD_DIM, :]
                pairs = [jnp.where(first_group, xa + pe_a, xb_r + pe_b),
                         jnp.where(first_group, xa_r + pe_a, xb + pe_b)]
                for g in range(G):
                    d = _dot(pairs[g].astype(BF16), w)
                    acc[half, g] = d if p == 0 else acc[half, g] + d
        out = []
        for g in range(G):
            h = acc[0, g] + pltpu.roll(acc[1, g], n_seg - 1, 0)
            out.append((h * _sigmoid(h)).astype(BF16))
        return out

    hk = run(k_ref, pek_ref, wk1_ref)
    hv = run(v_ref, pev_ref, wv1_ref)
    for g in range(G):
        ko_ref[0, g] = _dot(hk[g], wk2_ref[...]).astype(BF16)
        vo_ref[:, g * n_seg:(g + 1) * n_seg] = _dot(hv[g], wv2_ref[...]).T[0:HEAD_DIM, :].astype(BF16)


def _compress(kc, vc, pek, pev, wk1, wk2, wv1, wv2, B, S):
    G = NSA_KV_HEADS
    n_seg = S // CMP_STRIDE
    width = kc.shape[1]
    const2 = lambda b: (0, 0)
    tok_spec = pl.BlockSpec((S, width), lambda b: (b, 0))
    return pl.pallas_call(
        _compress_kernel,
        grid=(B,),
        in_specs=[tok_spec, tok_spec,
                  pl.BlockSpec(pek.shape, const2), pl.BlockSpec(pev.shape, const2),
                  pl.BlockSpec(wk1.shape, const2), pl.BlockSpec(wk2.shape, const2),
                  pl.BlockSpec(wv1.shape, const2), pl.BlockSpec(wv2.shape, const2)],
        out_specs=[pl.BlockSpec((1, G, n_seg, HEAD_DIM), lambda b: (b, 0, 0, 0)),
                   pl.BlockSpec((HEAD_DIM, G * n_seg), lambda b: (0, b))],
        out_shape=[jax.ShapeDtypeStruct((B, G, n_seg, HEAD_DIM), BF16),
                   jax.ShapeDtypeStruct((HEAD_DIM, B * G * n_seg), BF16)],
        compiler_params=pltpu.CompilerParams(
            dimension_semantics=("parallel",), vmem_limit_bytes=VMEM_LIMIT_BYTES),
        name="compress",
    )(kc, vc, pek, pev, wk1, wk2, wv1, wv2)


def _nsa_kernel(q_ref, kcmp_ref, vcmp_ref, ks_ref, vs_ref, kw_ref, vw_ref, gate_ref,
                o_ref, m_ref, acc_ref, s0_ref):
    G, R = NSA_KV_HEADS, NSA_GROUP
    n_cp = kcmp_ref.shape[2]
    S = ks_ref.shape[2]
    n_cmp = (S - CMP_BLOCK) // CMP_STRIDE + 1
    n_blk = S // SEL_BLOCK
    qi = pl.program_id(1)
    q0 = qi * TQ
    t_q = q0 + lax.broadcasted_iota(jnp.int32, (1, TQ), 1)
    qs = [jnp.concatenate([q_ref[0, g, r] for r in range(R)], axis=1) for g in range(G)]

    def heads(x):
        return [x[:, r * TQ:(r + 1) * TQ] for r in range(R)]

    def masked_exp(s_heads, mask):
        es = []
        for s in s_heads:
            s = jnp.where(mask, s, NEG_INF)
            es.append(jnp.exp2(s - jnp.max(s, axis=0, keepdims=True)))
        return es

    c_idx = lax.broadcasted_iota(jnp.int32, (n_cp, TQ), 0)
    cmask = ((c_idx * CMP_STRIDE + (CMP_BLOCK - 1)) <= t_q) & (c_idx < n_cmp)
    jj = lax.broadcasted_iota(jnp.int32, (n_blk, n_cp), 0)
    cc = lax.broadcasted_iota(jnp.int32, (n_blk, n_cp), 1)
    overlap = ((cc * CMP_STRIDE < (jj + 1) * SEL_BLOCK) & (cc * CMP_STRIDE + CMP_BLOCK > jj * SEL_BLOCK)
               & (cc < n_cmp)).astype(BF16)
    jb = lax.broadcasted_iota(jnp.int32, (n_blk, TQ), 0)
    jb_f = jb.astype(F32)
    cur = jnp.right_shift(t_q, int(math.log2(SEL_BLOCK)))
    forced = (jb == 0) | (jb == cur) | (jb == cur - 1)
    valid = jb <= cur

    n_sub = TQ // KV_CHUNK
    n_wc = (KV_CHUNK + WINDOW) // KV_CHUNK
    win_c0, win_mask, win_q = [], [], []
    for h in range(n_sub):
        c0 = jnp.maximum(q0 // KV_CHUNK + h - WINDOW // KV_CHUNK, 0)
        kpos = c0 * KV_CHUNK + lax.broadcasted_iota(jnp.int32, (n_wc * KV_CHUNK, KV_CHUNK), 0)
        diff = t_q[:, h * KV_CHUNK:(h + 1) * KV_CHUNK] - kpos
        win_c0.append(c0)
        win_mask.append((diff >= 0) & (diff < WINDOW))
        win_q.append([jnp.concatenate([qs[g][:, r * TQ + h * KV_CHUNK:r * TQ + (h + 1) * KV_CHUNK]
                                       for r in range(R)], axis=1) for g in range(G)])

    def window_keys(g, h):
        return kw_ref[0, g, pl.ds(pl.multiple_of(win_c0[h] * KV_CHUNK, KV_CHUNK), n_wc * KV_CHUNK), :]

    s_cmp = [_dot(kcmp_ref[0, g], qs[g]) for g in range(G)]
    s_win = [[_dot(window_keys(g, h), win_q[h][g]) for h in range(n_sub)] for g in range(G)]
    any_cmp = t_q >= CMP_BLOCK - 1
    p_cmp, o_cmp, o_win = [], [], []
    for g in range(G):
        es = masked_exp(heads(s_cmp[g]), cmask)
        p_cmp.append([e * jnp.where(any_cmp, 1.0 / jnp.sum(e, axis=0, keepdims=True), 0.0) for e in es])

    sels = []
    for g in range(G):
        p_c = p_cmp[g]
        p_sum = p_c[0]
        for r in range(1, R):
            p_sum = p_sum + p_c[r]
        p_hi = p_sum.astype(BF16)
        p_lo = (p_sum - p_hi.astype(F32)).astype(BF16)
        p_slc = _dot(overlap, p_hi) + _dot(overlap, p_lo)
        work = jnp.where(valid & jnp.logical_not(forced), p_slc, -1.0)
        picked = forced
        for _ in range(min(SEL_TOPK, n_blk) - SEL_FORCED):
            best = jnp.max(work, axis=0, keepdims=True)
            first = jnp.min(jnp.where(work == best, jb_f, float(n_blk)), axis=0, keepdims=True)
            hit = jb_f == first
            picked = picked | hit
            work = jnp.where(hit, -2.0, work)
        bias = jnp.where(picked, 0.0, -SEL_BIAS).astype(BF16)
        sels.append(jnp.concatenate([qs[g], jnp.concatenate([bias] * R, axis=1),
                                     jnp.zeros((LANES - HEAD_DIM - n_blk, R * TQ), BF16)], axis=0))

    m_ref[...] = jnp.full(m_ref.shape, NEG_INF, F32)
    acc_ref[...] = jnp.zeros(acc_ref.shape, F32)

    def scores(g, kt):
        return _dot(ks_ref[0, g, pl.ds(pl.multiple_of(kt * TK, TK), TK), :], sels[g])

    def accumulate(g, kt, s):
        m_old = m_ref[g]
        m_new = jnp.maximum(m_old, jnp.max(s, axis=0, keepdims=True))
        alpha = jnp.exp2(m_old - m_new)
        p = jnp.exp2(s - m_new)
        acc_ref[g] = alpha * acc_ref[g] + _dot(vs_ref[0, g, kt], p.astype(BF16))
        m_ref[g] = m_new

    tri = (lax.broadcasted_iota(jnp.int32, (KV_CHUNK, KV_CHUNK), 0)
           <= lax.broadcasted_iota(jnp.int32, (KV_CHUNK, KV_CHUNK), 1))

    def accumulate_diagonal(g, s):
        v = vs_ref[0, g, qi]
        for h in range(n_sub):
            lo, n_k = h * KV_CHUNK, (h + 1) * KV_CHUNK
            cols = [slice(r * TQ + lo, r * TQ + n_k) for r in range(R)]
            s_h = jnp.concatenate([jnp.concatenate(([s[0:lo, c]] if h else []) + [jnp.where(tri, s[lo:n_k, c], NEG_INF)],
                                                   axis=0) for c in cols], axis=1)
            m_old = jnp.concatenate([m_ref[g, :, c] for c in cols], axis=1)
            m_new = jnp.maximum(m_old, jnp.max(s_h, axis=0, keepdims=True))
            alpha = jnp.exp2(m_old - m_new)
            pv = _dot(v[:, 0:n_k], jnp.exp2(s_h - m_new).astype(BF16))
            for r, c in enumerate(cols):
                sub = slice(r * KV_CHUNK, (r + 1) * KV_CHUNK)
                acc_ref[g, :, c] = alpha[:, sub] * acc_ref[g, :, c] + pv[:, sub]

    assert G == 2
    s0_ref[...] = scores(0, 0)

    for g in range(G):
        o_cmp.append(_dot(vcmp_ref[:, g * n_cp:(g + 1) * n_cp],
                          jnp.concatenate(p_cmp[g], axis=1).astype(BF16)))
    for g in range(G):
        sub = []
        for h in range(n_sub):
            s_heads = [s_win[g][h][:, r * KV_CHUNK:(r + 1) * KV_CHUNK] for r in range(R)]
            e_w = jnp.concatenate(masked_exp(s_heads, win_mask[h]), axis=1).astype(BF16)
            v_w = jnp.concatenate([vw_ref[0, g, win_c0[h] + j] for j in range(n_wc)], axis=1)
            sub.append(_dot(v_w, e_w))
        o_win.append(jnp.concatenate([sub[h][:, r * KV_CHUNK:(r + 1) * KV_CHUNK]
                                      for r in range(R) for h in range(n_sub)], axis=1))

    def interior_tile(kt, carry):
        s1 = scores(1, kt)
        accumulate(0, kt, s0_ref[...])
        s0_ref[...] = scores(0, kt + 1)
        accumulate(1, kt, s1)
        return carry

    lax.fori_loop(0, qi, interior_tile, 0)
    s1 = scores(1, qi)
    accumulate_diagonal(0, s0_ref[...])
    accumulate_diagonal(1, s1)

    gates = _sigmoid(gate_ref[0].T)
    outs = []
    for g in range(G):
        o_sel = acc_ref[g]
        for r, (oc, os_, ow) in enumerate(zip(heads(o_cmp[g]), heads(o_sel), heads(o_win[g]))):
            g0 = g * HEAD_DIM + r
            g_sel = gates[g0 + R:g0 + R + 1] * (1.0 / os_[HEAD_DIM:HEAD_DIM + 1])
            g_win = gates[g0 + 2 * R:g0 + 2 * R + 1] * (1.0 / ow[HEAD_DIM:HEAD_DIM + 1])
            outs.append(gates[g0:g0 + 1] * oc + g_sel * os_[0:HEAD_DIM] + g_win * ow[0:HEAD_DIM])
    o_ref[0] = jnp.concatenate(outs, axis=0).T.astype(BF16)


def _nsa(q, kcmp, vcmp, ks, vs, kw, vw, gates):
    B, G, R, dh, S = q.shape
    n_cp = kcmp.shape[2]
    assert TQ == TK, "the key sweep treats exactly one tile per query tile as the diagonal"
    k_spec = lambda a: pl.BlockSpec((1,) + a.shape[1:], lambda b, i: (b, 0, 0, 0))
    vt_spec = lambda a: pl.BlockSpec((1,) + a.shape[1:], lambda b, i: (b, 0, 0, 0, 0))
    return pl.pallas_call(
        _nsa_kernel,
        grid=(B, S // TQ),
        in_specs=[
            pl.BlockSpec((1, G, R, dh, TQ), lambda b, i: (b, 0, 0, 0, i)),
            pl.BlockSpec((1, G, n_cp, dh), lambda b, i: (b, 0, 0, 0)),
            pl.BlockSpec((dh, G * n_cp), lambda b, i: (0, b)),
            k_spec(ks), vt_spec(vs), k_spec(kw), vt_spec(vw),
            pl.BlockSpec((1, TQ, LANES), lambda b, i: (b, i, 0)),
        ],
        out_specs=pl.BlockSpec((1, TQ, G * R * dh), lambda b, i: (b, i, 0)),
        out_shape=jax.ShapeDtypeStruct((B, S, G * R * dh), BF16),
        scratch_shapes=[pltpu.VMEM((G, 1, R * TQ), F32), pltpu.VMEM((G, dh + ONES_ROWS, R * TQ), F32),
                        pltpu.VMEM((TK, R * TQ), F32)],
        compiler_params=pltpu.CompilerParams(
            dimension_semantics=("parallel", "arbitrary"), vmem_limit_bytes=VMEM_LIMIT_BYTES),
        name="nsa_attention",
    )(q, kcmp, vcmp, ks, vs, kw, vw, gates)


def _retention_kernel(q_ref, kt_ref, v_ref, g_ref, w_ref, o_ref, state_ref):
    C = RET_CHUNK
    NB = q_ref.shape[0]

    @pl.when(pl.program_id(1) == 0)
    def _():
        state_ref[...] = jnp.zeros(state_ref.shape, F32)

    i_col = lax.broadcasted_iota(jnp.int32, (C, 1), 0)
    i_row = lax.broadcasted_iota(jnp.int32, (1, C), 1)
    d_int = i_col - i_row
    log_gamma = [math.log(1.0 - 2.0 ** (-5.0 - h)) for h in range(RET_HEADS)]
    dmat = [jnp.where(d_int >= 0, jnp.exp(lg * jnp.maximum(d_int, 0).astype(F32)), 0.0) for lg in log_gamma]
    xi = [jnp.exp(lg * (i_col + 1).astype(F32)) for lg in log_gamma]
    zeta = [jnp.exp(lg * (C - 1 - i_row).astype(F32)) for lg in log_gamma]
    gamma_c = [math.exp(lg * C) for lg in log_gamma]

    units = [(n, h) for n in range(NB) for h in range(RET_HEADS)]
    q, kt, v, st = {}, {}, {}, {}
    for n, h in units:
        q[n, h] = q_ref[n, :, h * RET_QK_DIM:(h + 1) * RET_QK_DIM]
        kt[n, h] = kt_ref[n, 0, h * RET_QK_DIM:(h + 1) * RET_QK_DIM, :]
        v[n, h] = v_ref[n, :, h * RET_V_DIM:(h + 1) * RET_V_DIM]
        st[n, h] = state_ref[n, h]
    inner = {u: _dot(q[u], kt[u]) for u in units}
    cross = {u: _dot(q[u], st[u].astype(BF16)) for u in units}
    kv = {u: _dot((kt[u].astype(F32) * zeta[u[1]]).astype(BF16), v[u]) for u in units}
    for n, h in units:
        u = (n, h)
        vsl = slice(h * RET_V_DIM, (h + 1) * RET_V_DIM)
        y = _dot((inner[u] * dmat[h]).astype(BF16), v[u]) + cross[u] * xi[h]
        state_ref[n, h] = gamma_c[h] * st[u] + kv[u]

        mu = jnp.mean(y, axis=-1, keepdims=True)
        yc = y - mu
        var = jnp.mean(yc * yc, axis=-1, keepdims=True)
        yn = yc * lax.rsqrt(var + EPS) * w_ref[:, vsl]
        gate = g_ref[n, :, vsl].astype(F32)
        o_ref[n, :, vsl] = (gate * _sigmoid(gate) * yn).astype(BF16)


def _retention(rq, rkt, rv, rg, w):
    B, S, _ = rq.shape
    C = RET_CHUNK
    nb = math.gcd(RET_BATCH, B)
    q_spec = pl.BlockSpec((nb, C, RET_QK_WIDTH), lambda b, c: (b, c, 0))
    kt_spec = pl.BlockSpec((nb, 1, RET_QK_WIDTH, C), lambda b, c: (b, c, 0, 0))
    v_spec = pl.BlockSpec((nb, C, RET_V_WIDTH), lambda b, c: (b, c, 0))
    return pl.pallas_call(
        _retention_kernel,
        grid=(B // nb, S // C),
        in_specs=[q_spec, kt_spec, v_spec, v_spec, pl.BlockSpec((1, RET_V_WIDTH), lambda b, c: (0, 0))],
        out_specs=v_spec,
        out_shape=jax.ShapeDtypeStruct((B, S, RET_V_WIDTH), BF16),
        scratch_shapes=[pltpu.VMEM((nb, RET_HEADS, RET_QK_DIM, RET_V_DIM), F32)],
        compiler_params=pltpu.CompilerParams(
            dimension_semantics=("parallel", "arbitrary"), vmem_limit_bytes=VMEM_LIMIT_BYTES),
        name="retention",
    )(rq, rkt, rv, rg, w)


def _out_ffn_kernel(x_ref, a_ref, r_ref, wo_ref, ln_ref, wu_ref, wd_ref, o_ref):
    na = a_ref.shape[1]
    mix = _dot(a_ref[...], wo_ref[0:na, :]) + _dot(r_ref[...], wo_ref[na:, :])
    h = x_ref[...] + mix
    ms = jnp.mean(h * h, axis=-1, keepdims=True)
    hn = (h * lax.rsqrt(ms + EPS) * ln_ref[...]).astype(BF16)
    d_ff = wu_ref.shape[1]
    acc = None
    for f in range(d_ff // FF_CHUNK):
        cols = slice(f * FF_CHUNK, (f + 1) * FF_CHUNK)
        u = jnp.maximum(_dot(hn, wu_ref[:, cols]), 0.0)
        d = _dot((u * u).astype(BF16), wd_ref[cols, :])
        acc = d if acc is None else acc + d
    o_ref[...] = h + acc


def _out_ffn(x2, o_nsa, o_ret, w_out, ln2, w_up, w_down):
    N, D = x2.shape
    tm = TM_FFN
    row = lambda i: (i, 0)
    const = lambda i: (0, 0)
    resident = functools.partial(pl.BlockSpec, index_map=const, pipeline_mode=pl.Buffered(1))
    return pl.pallas_call(
        _out_ffn_kernel,
        grid=(N // tm,),
        in_specs=[
            pl.BlockSpec((tm, D), row),
            pl.BlockSpec((tm, o_nsa.shape[1]), row),
            pl.BlockSpec((tm, o_ret.shape[1]), row),
            resident(w_out.shape),
            pl.BlockSpec((1, D), const),
            resident(w_up.shape),
            resident(w_down.shape),
        ],
        out_specs=pl.BlockSpec((tm, D), row),
        out_shape=jax.ShapeDtypeStruct((N, D), F32),
        compiler_params=pltpu.CompilerParams(
            dimension_semantics=("parallel",), vmem_limit_bytes=VMEM_LIMIT_BYTES),
        name="out_ffn",
    )(x2, o_nsa, o_ret, w_out, ln2, w_up, w_down)


def _rope_tables(S):
    half = HEAD_DIM // 2
    inv = ROPE_THETA ** (-jnp.arange(half, dtype=F32) / half)
    ang = jnp.arange(S).astype(F32)[:, None] * inv[None, :]
    cos, sin = jnp.cos(ang), jnp.sin(ang)
    reps = LANES // HEAD_DIM
    cos_h = jnp.concatenate([cos, cos], axis=-1)
    sin_h = jnp.concatenate([-sin, sin], axis=-1)
    return jnp.tile(cos_h, (1, reps)), jnp.tile(sin_h, (1, reps)), cos_h.T, sin_h.T


def _layer(h, ln1_w, w_in, q_norm_w, k_norm_w, cmp_pe_k, cmp_pe_v, cmp_wk1, cmp_wk2,
           cmp_wv1, cmp_wv2, ret_norm_w, w_out, ln2_w, w_up, w_down):
    B, S, D = h.shape
    N = B * S
    G = NSA_KV_HEADS
    x2 = h.reshape(N, D)

    w_perm, w_perm_t = _permute_w_in(w_in)
    cos_l, sin_l, cos_t, sin_t = _rope_tables(S)
    reps = LANES // HEAD_DIM
    qn_t = jnp.broadcast_to(q_norm_w[:, None], (HEAD_DIM, TM_PROJ))
    kn = jnp.tile(k_norm_w, (1, reps))

    (q, kc, vc, ks, vs, kw, vw, gates, rq, rk, rv, rg) = _in_proj(
        x2, ln1_w[None, :], w_perm.astype(BF16), w_perm_t.astype(BF16), cos_l, sin_l, cos_t, sin_t, qn_t, kn, B, S)

    kcmp, vcmp = _compress(
        kc, vc, jnp.tile(cmp_pe_k, (1, G)), jnp.tile(cmp_pe_v, (1, G)),
        cmp_wk1.astype(BF16), cmp_wk2.astype(BF16), cmp_wv1.astype(BF16),
        jnp.pad(cmp_wv2, ((0, 0), (0, LANES - HEAD_DIM))).astype(BF16), B, S)

    o_nsa = _nsa(q, kcmp, vcmp, ks, vs, kw, vw, gates.reshape(B, S, LANES))
    o_ret = _retention(rq.reshape(B, S, -1), rk, rv.reshape(B, S, -1),
                       rg.reshape(B, S, -1), ret_norm_w.reshape(1, RET_V_WIDTH))

    out = _out_ffn(x2, o_nsa.reshape(N, -1), o_ret.reshape(N, -1), w_out.astype(BF16),
                   ln2_w[None, :], w_up.astype(BF16), w_down.astype(BF16))
    return out.reshape(B, S, D)


def kernel(x, ln1_w, w_in, q_norm_w, k_norm_w, cmp_pe_k, cmp_pe_v, cmp_wk1, cmp_wk2, cmp_wv1, cmp_wv2,
           ret_norm_w, w_out, ln2_w, w_up, w_down):
    h = x
    for l in range(ln1_w.shape[0]):
        h = _layer(h, ln1_w[l], w_in[l], q_norm_w[l], k_norm_w[l], cmp_pe_k[l], cmp_pe_v[l],
                   cmp_wk1[l], cmp_wk2[l], cmp_wv1[l], cmp_wv2[l], ret_norm_w[l], w_out[l],
                   ln2_w[l], w_up[l], w_down[l])
    return h
```

```python
import functools
import math

import jax
import jax.numpy as jnp
import numpy as np
from jax import lax
from jax.experimental import pallas as pl
from jax.experimental.pallas import tpu as pltpu

F32 = jnp.float32
BF16 = jnp.bfloat16

NSA_HEADS = 8
NSA_KV_HEADS = 2
NSA_GROUP = NSA_HEADS // NSA_KV_HEADS
HEAD_DIM = 64
CMP_BLOCK = 32
CMP_STRIDE = 16
CMP_HIDDEN = 256
SEL_BLOCK = 64
SEL_TOPK = 8
SEL_FORCED = 3
WINDOW = 256
RET_HEADS = 4
RET_QK_DIM = 64
RET_V_DIM = 128
RET_CHUNK = 128
ROPE_THETA = 10000.0
EPS = 1e-6
NEG_INF = -1.0e30
SEL_BIAS = 2.0 ** 100

NSA_Q_DIM = NSA_HEADS * HEAD_DIM
NSA_KV_DIM = NSA_KV_HEADS * HEAD_DIM
NSA_GATE_DIM = 3 * NSA_HEADS
RET_QK_WIDTH = RET_HEADS * RET_QK_DIM
RET_V_WIDTH = RET_HEADS * RET_V_DIM

LANES = 128
VMEM_LIMIT_BYTES = 56 * 1024 * 1024

TM_PROJ = 512
TQ = 256
TK = 256
KV_CHUNK = 128
ONES_ROWS = 16
TM_FFN = 1024
FF_CHUNK = 512
RET_BATCH = 16


def _dot(a, b):
    return jnp.dot(a, b, preferred_element_type=F32)


def _dot_nt(a, b):
    return lax.dot_general(a, b, (((1,), (1,)), ((), ())), preferred_element_type=F32)


def _sigmoid(x):
    return 1.0 / (1.0 + jnp.exp(-x))


_K0 = 0
_V0 = _K0 + 3 * NSA_KV_DIM
_RQ0 = _V0 + NSA_KV_DIM + LANES
_RV0 = _RQ0 + RET_QK_WIDTH
_RG0 = _RV0 + RET_V_WIDTH
_W_IN_COLS = _RG0 + RET_V_WIDTH
_TQ0 = 0
_TV0 = _TQ0 + NSA_Q_DIM
_TRK0 = _TV0 + 2 * NSA_KV_DIM
_WT_ROWS = _TRK0 + RET_QK_WIDTH


def _permute_w_in(w_in):
    sizes = [NSA_Q_DIM] + [NSA_KV_DIM] * 6 + [NSA_GATE_DIM, RET_QK_WIDTH, RET_QK_WIDTH, RET_V_WIDTH, RET_V_WIDTH]
    off = [int(o) for o in np.concatenate([[0], np.cumsum(sizes)])]
    (q, kc, vc, ks, vs, kw, vw, gate, rq, rk, rv, rg) = [w_in[:, off[i]:off[i + 1]] for i in range(12)]
    d = w_in.shape[0]
    gate = gate.reshape(d, 3, NSA_KV_HEADS, NSA_GROUP).transpose(0, 2, 1, 3).reshape(d, NSA_KV_HEADS, 3 * NSA_GROUP)
    gate = jnp.pad(gate, ((0, 0), (0, 0), (0, HEAD_DIM - 3 * NSA_GROUP))).reshape(d, LANES)
    w = jnp.concatenate([kc, ks, kw, vc, gate, rq, rv, rg], axis=1)
    wt = jnp.concatenate([q, vs, vw, rk], axis=1).T
    assert w.shape[1] == _W_IN_COLS and wt.shape[0] == _WT_ROWS
    return w, wt


def _in_proj_kernel(x_ref, ln_ref, w_ref, wt_ref, cos_ref, sin_ref, cost_ref, sint_ref, qnt_ref, kn_ref,
                    q_ref, kc_ref, vc_ref, ks_ref, vs_ref, kw_ref, vw_ref, gate_ref,
                    rq_ref, rk_ref, rv_ref, rg_ref, *, tiles_per_seq):
    x = x_ref[...]
    tm = x.shape[0]
    ms = jnp.mean(x * x, axis=-1, keepdims=True)
    xn = (x * lax.rsqrt(ms + EPS) * ln_ref[...]).astype(BF16)
    cos = cos_ref[...]
    sin = sin_ref[...]
    lane = lax.broadcasted_iota(jnp.int32, (1, LANES), 1)
    low_half = (lane & (HEAD_DIM // 2)) == 0
    first_head = lane < HEAD_DIM

    def proj(a, b):
        return _dot(xn, w_ref[:, a:b])

    def rope(t):
        swapped = jnp.where(low_half, pltpu.roll(t, LANES - HEAD_DIM // 2, 1), pltpu.roll(t, HEAD_DIM // 2, 1))
        return t * cos + swapped * sin

    def head_norm(t, w):
        t2 = t * t
        s0 = jnp.sum(jnp.where(first_head, t2, 0.0), axis=-1, keepdims=True)
        s1 = jnp.sum(jnp.where(first_head, 0.0, t2), axis=-1, keepdims=True)
        msq = jnp.where(first_head, s0, s1) * (1.0 / HEAD_DIM)
        return t * lax.rsqrt(msq + EPS) * w

    def tiles(sec):
        return [sec[:, c * LANES:(c + 1) * LANES] for c in range(sec.shape[1] // LANES)]

    def split_heads(t, fill):
        return [jnp.where(first_head, t, fill), jnp.where(first_head, pltpu.roll(t, HEAD_DIM, 1), fill)]

    def proj_t(a, b):
        return _dot_nt(wt_ref[a:b, :], xn)

    cos_t = cost_ref[...]
    sin_t = sint_ref[...]

    def rope_t(t):
        half = HEAD_DIM // 2
        return t * cos_t + jnp.concatenate([t[half:], t[:half]], axis=0) * sin_t

    q_t = proj_t(_TQ0, _TV0)
    qn_t = qnt_ref[...]
    for h in range(NSA_HEADS):
        t = q_t[h * HEAD_DIM:(h + 1) * HEAD_DIM]
        t = t * lax.rsqrt(jnp.mean(t * t, axis=0, keepdims=True) + EPS) * qn_t
        t = rope_t(t) * (HEAD_DIM ** -0.5 * math.log2(math.e))
        q_ref[0, h // NSA_GROUP, h % NSA_GROUP] = t.astype(BF16)

    def store_values_t(v, out_ref):
        chunk = out_ref.shape[-1]
        ones = jnp.ones((ONES_ROWS, chunk), BF16)
        for c in range(tm // chunk):
            for g in range(NSA_KV_HEADS):
                out_ref[0, g, c, 0:HEAD_DIM, :] = v[g * HEAD_DIM:(g + 1) * HEAD_DIM,
                                                    c * chunk:(c + 1) * chunk].astype(BF16)
                out_ref[0, g, c, HEAD_DIM:HEAD_DIM + ONES_ROWS, :] = ones

    v_t = proj_t(_TV0, _TRK0)
    store_values_t(v_t[0:NSA_KV_DIM], vs_ref)
    store_values_t(v_t[NSA_KV_DIM:2 * NSA_KV_DIM], vw_ref)

    rk_t = proj_t(_TRK0, _WT_ROWS)
    for h in range(RET_HEADS):
        rows = slice(h * RET_QK_DIM, (h + 1) * RET_QK_DIM)
        t = (rope_t(rk_t[rows]) * (RET_QK_DIM ** -0.5)).astype(BF16)
        for j in range(tm // RET_CHUNK):
            rk_ref[0, j, rows, :] = t[:, j * RET_CHUNK:(j + 1) * RET_CHUNK]

    ksec = tiles(proj(_K0, _V0))
    kc = rope(head_norm(ksec[0], kn_ref[0:1, :]))
    ks = rope(head_norm(ksec[1], kn_ref[1:2, :]))
    kw = rope(head_norm(ksec[2], kn_ref[2:3, :])).astype(BF16)
    s0 = (pl.program_id(0) % tiles_per_seq) * tm
    blk = jnp.right_shift(s0 + lax.broadcasted_iota(jnp.int32, (tm, 1), 0), int(math.log2(SEL_BLOCK)))
    onehot = (lane - HEAD_DIM == blk).astype(F32)
    ks_aug = split_heads(ks, onehot)
    kc_ref[...] = kc
    for g in range(NSA_KV_HEADS):
        ks_ref[0, g] = ks_aug[g].astype(BF16)
        kw_ref[0, g] = kw[:, g * HEAD_DIM:(g + 1) * HEAD_DIM]

    vsec = tiles(proj(_V0, _RQ0))
    vc_ref[...] = vsec[0]
    gate_ref[...] = vsec[1]

    for c, t in enumerate(tiles(proj(_RQ0, _RV0))):
        rq_ref[:, c * LANES:(c + 1) * LANES] = rope(t).astype(BF16)
    rv_ref[...] = proj(_RV0, _RG0).astype(BF16)
    rg_ref[...] = proj(_RG0, _W_IN_COLS).astype(BF16)


def _in_proj(x2, ln1, w_perm, w_perm_t, cos, sin, cos_t, sin_t, qn_t, kn, B, S):
    N, D = x2.shape
    tm = TM_PROJ
    spt = S // tm
    G, R, dh = NSA_KV_HEADS, NSA_GROUP, HEAD_DIM

    def row(i):
        return (i, 0)

    def hm(i):
        return (i // spt, 0, i % spt, 0)

    hm_spec = pl.BlockSpec((1, G, tm, dh), hm)
    out_shape = [
        jax.ShapeDtypeStruct((B, G, R, dh, S), BF16),
        jax.ShapeDtypeStruct((N, LANES), F32),
        jax.ShapeDtypeStruct((N, LANES), F32),
        jax.ShapeDtypeStruct((B, G, S, LANES), BF16),
        jax.ShapeDtypeStruct((B, G, S // TK, dh + ONES_ROWS, TK), BF16),
        jax.ShapeDtypeStruct((B, G, S, dh), BF16),
        jax.ShapeDtypeStruct((B, G, S // KV_CHUNK, dh + ONES_ROWS, KV_CHUNK), BF16),
        jax.ShapeDtypeStruct((N, LANES), F32),
        jax.ShapeDtypeStruct((N, RET_QK_WIDTH), BF16),
        jax.ShapeDtypeStruct((B, S // RET_CHUNK, RET_QK_WIDTH, RET_CHUNK), BF16),
        jax.ShapeDtypeStruct((N, RET_V_WIDTH), BF16),
        jax.ShapeDtypeStruct((N, RET_V_WIDTH), BF16),
    ]
    out_specs = [
        pl.BlockSpec((1, G, R, dh, tm), lambda i: (i // spt, 0, 0, 0, i % spt)),
        pl.BlockSpec((tm, LANES), row), pl.BlockSpec((tm, LANES), row), pl.BlockSpec((1, G, tm, LANES), hm),
        pl.BlockSpec((1, G, tm // TK, dh + ONES_ROWS, TK), lambda i: (i // spt, 0, i % spt, 0, 0)),
        hm_spec,
        pl.BlockSpec((1, G, tm // KV_CHUNK, dh + ONES_ROWS, KV_CHUNK), lambda i: (i // spt, 0, i % spt, 0, 0)),
        pl.BlockSpec((tm, LANES), row),
        pl.BlockSpec((tm, RET_QK_WIDTH), row),
        pl.BlockSpec((1, tm // RET_CHUNK, RET_QK_WIDTH, RET_CHUNK), lambda i: (i // spt, i % spt, 0, 0)),
        pl.BlockSpec((tm, RET_V_WIDTH), row),
        pl.BlockSpec((tm, RET_V_WIDTH), row),
    ]
    const = lambda i: (0, 0)
    in_specs = [
        pl.BlockSpec((tm, D), row),
        pl.BlockSpec((1, D), const),
        pl.BlockSpec((D, _W_IN_COLS), const, pipeline_mode=pl.Buffered(1)),
        pl.BlockSpec((_WT_ROWS, D), const, pipeline_mode=pl.Buffered(1)),
        pl.BlockSpec((tm, LANES), lambda i: (i % spt, 0)),
        pl.BlockSpec((tm, LANES), lambda i: (i % spt, 0)),
        pl.BlockSpec((dh, tm), lambda i: (0, i % spt)),
        pl.BlockSpec((dh, tm), lambda i: (0, i % spt)),
        pl.BlockSpec((dh, tm), const),
        pl.BlockSpec((3, LANES), const),
    ]
    return pl.pallas_call(
        functools.partial(_in_proj_kernel, tiles_per_seq=spt),
        grid=(N // tm,),
        in_specs=in_specs,
        out_specs=out_specs,
        out_shape=out_shape,
        compiler_params=pltpu.CompilerParams(
            dimension_semantics=("parallel",), vmem_limit_bytes=VMEM_LIMIT_BYTES),
        name="in_proj",
    )(x2, ln1, w_perm, w_perm_t, cos, sin, cos_t, sin_t, qn_t, kn)


def _compress_kernel(k_ref, v_ref, pek_ref, pev_ref, wk1_ref, wk2_ref, wv1_ref, wv2_ref, ko_ref, vo_ref):
    n_seg = k_ref.shape[0] // CMP_STRIDE
    G = NSA_KV_HEADS
    assert G == 2
    first_group = lax.broadcasted_iota(jnp.int32, (1, LANES), 1) < HEAD_DIM

    def run(x_ref, pe_ref, w1_ref):
        acc = {}
        for p in range(CMP_STRIDE // 2):
            xa = x_ref[pl.ds(2 * p, n_seg, stride=CMP_STRIDE), :]
            xb = x_ref[pl.ds(2 * p + 1, n_seg, stride=CMP_STRIDE), :]
            xa_r = pltpu.roll(xa, HEAD_DIM, 1)
            xb_r = pltpu.roll(xb, HEAD_DIM, 1)
            for half in range(2):
                l0 = half * CMP_STRIDE + 2 * p
                pe_a = pe_ref[l0:l0 + 1, :]
                pe_b = pe_ref[l0 + 1:l0 + 2, :]
                w = w1_ref[l0 * HEAD_DIM:(l0 + 2) * HEAD_DIM, :]
                pairs = [jnp.where(first_group, xa + pe_a, xb_r + pe_b),
                         jnp.where(first_group, xa_r + pe_a, xb + pe_b)]
                for g in range(G):
                    d = _dot(pairs[g].astype(BF16), w)
                    acc[half, g] = d if p == 0 else acc[half, g] + d
        out = []
        for g in range(G):
            h = acc[0, g] + pltpu.roll(acc[1, g], n_seg - 1, 0)
            out.append((h * _sigmoid(h)).astype(BF16))
        return out

    hk = run(k_ref, pek_ref, wk1_ref)
    hv = run(v_ref, pev_ref, wv1_ref)
    for g in range(G):
        ko_ref[0, g] = _dot(hk[g], wk2_ref[...]).astype(BF16)
        vo_ref[:, g * n_seg:(g + 1) * n_seg] = _dot(hv[g], wv2_ref[...]).T[0:HEAD_DIM, :].astype(BF16)


def _compress(kc, vc, pek, pev, wk1, wk2, wv1, wv2, B, S):
    G = NSA_KV_HEADS
    n_seg = S // CMP_STRIDE
    width = kc.shape[1]
    const2 = lambda b: (0, 0)
    tok_spec = pl.BlockSpec((S, width), lambda b: (b, 0))
    return pl.pallas_call(
        _compress_kernel,
        grid=(B,),
        in_specs=[tok_spec, tok_spec,
                  pl.BlockSpec(pek.shape, const2), pl.BlockSpec(pev.shape, const2),
                  pl.BlockSpec(wk1.shape, const2), pl.BlockSpec(wk2.shape, const2),
                  pl.BlockSpec(wv1.shape, const2), pl.BlockSpec(wv2.shape, const2)],
        out_specs=[pl.BlockSpec((1, G, n_seg, HEAD_DIM), lambda b: (b, 0, 0, 0)),
                   pl.BlockSpec((HEAD_DIM, G * n_seg), lambda b: (0, b))],
        out_shape=[jax.ShapeDtypeStruct((B, G, n_seg, HEAD_DIM), BF16),
                   jax.ShapeDtypeStruct((HEAD_DIM, B * G * n_seg), BF16)],
        compiler_params=pltpu.CompilerParams(
            dimension_semantics=("parallel",), vmem_limit_bytes=VMEM_LIMIT_BYTES),
        name="compress",
    )(kc, vc, pek, pev, wk1, wk2, wv1, wv2)


def _nsa_kernel(q_ref, kcmp_ref, vcmp_ref, ks_ref, vs_ref, kw_ref, vw_ref, gate_ref,
                o_ref, m_ref, acc_ref, s0_ref):
    G, R = NSA_KV_HEADS, NSA_GROUP
    n_cp = kcmp_ref.shape[2]
    S = ks_ref.shape[2]
    n_cmp = (S - CMP_BLOCK) // CMP_STRIDE + 1
    n_blk = S // SEL_BLOCK
    qi = pl.program_id(1)
    q0 = qi * TQ
    t_q = q0 + lax.broadcasted_iota(jnp.int32, (1, TQ), 1)
    qs = [jnp.concatenate([q_ref[0, g, r] for r in range(R)], axis=1) for g in range(G)]

    def heads(x):
        return [x[:, r * TQ:(r + 1) * TQ] for r in range(R)]

    def masked_exp(s_heads, mask):
        es = []
        for s in s_heads:
            s = jnp.where(mask, s, NEG_INF)
            es.append(jnp.exp2(s - jnp.max(s, axis=0, keepdims=True)))
        return es

    c_idx = lax.broadcasted_iota(jnp.int32, (n_cp, TQ), 0)
    cmask = ((c_idx * CMP_STRIDE + (CMP_BLOCK - 1)) <= t_q) & (c_idx < n_cmp)
    jj = lax.broadcasted_iota(jnp.int32, (n_blk, n_cp), 0)
    cc = lax.broadcasted_iota(jnp.int32, (n_blk, n_cp), 1)
    overlap = ((cc * CMP_STRIDE < (jj + 1) * SEL_BLOCK) & (cc * CMP_STRIDE + CMP_BLOCK > jj * SEL_BLOCK)
               & (cc < n_cmp)).astype(BF16)
    jb = lax.broadcasted_iota(jnp.int32, (n_blk, TQ), 0)
    jb_f = jb.astype(F32)
    cur = jnp.right_shift(t_q, int(math.log2(SEL_BLOCK)))
    forced = (jb == 0) | (jb == cur) | (jb == cur - 1)
    valid = jb <= cur

    n_sub = TQ // KV_CHUNK
    n_wc = (KV_CHUNK + WINDOW) // KV_CHUNK
    win_c0, win_mask, win_q = [], [], []
    for h in range(n_sub):
        c0 = jnp.maximum(q0 // KV_CHUNK + h - WINDOW // KV_CHUNK, 0)
        kpos = c0 * KV_CHUNK + lax.broadcasted_iota(jnp.int32, (n_wc * KV_CHUNK, KV_CHUNK), 0)
        diff = t_q[:, h * KV_CHUNK:(h + 1) * KV_CHUNK] - kpos
        win_c0.append(c0)
        win_mask.append((diff >= 0) & (diff < WINDOW))
        win_q.append([jnp.concatenate([qs[g][:, r * TQ + h * KV_CHUNK:r * TQ + (h + 1) * KV_CHUNK]
                                       for r in range(R)], axis=1) for g in range(G)])

    def window_keys(g, h):
        return kw_ref[0, g, pl.ds(pl.multiple_of(win_c0[h] * KV_CHUNK, KV_CHUNK), n_wc * KV_CHUNK), :]

    s_cmp = [_dot(kcmp_ref[0, g], qs[g]) for g in range(G)]
    s_win = [[_dot(window_keys(g, h), win_q[h][g]) for h in range(n_sub)] for g in range(G)]
    any_cmp = t_q >= CMP_BLOCK - 1
    p_cmp, o_cmp, o_win = [], [], []
    for g in range(G):
        es = masked_exp(heads(s_cmp[g]), cmask)
        p_cmp.append([e * jnp.where(any_cmp, 1.0 / jnp.sum(e, axis=0, keepdims=True), 0.0) for e in es])

    sels = []
    for g in range(G):
        p_c = p_cmp[g]
        p_sum = p_c[0]
        for r in range(1, R):
            p_sum = p_sum + p_c[r]
        p_hi = p_sum.astype(BF16)
        p_lo = (p_sum - p_hi.astype(F32)).astype(BF16)
        p_slc = _dot(overlap, p_hi) + _dot(overlap, p_lo)
        work = jnp.where(valid & jnp.logical_not(forced), p_slc, -1.0)
        picked = forced
        for _ in range(min(SEL_TOPK, n_blk) - SEL_FORCED):
            best = jnp.max(work, axis=0, keepdims=True)
            first = jnp.min(jnp.where(work == best, jb_f, float(n_blk)), axis=0, keepdims=True)
            hit = jb_f == first
            picked = picked | hit
            work = jnp.where(hit, -2.0, work)
        bias = jnp.where(picked, 0.0, -SEL_BIAS).astype(BF16)
        sels.append(jnp.concatenate([qs[g], jnp.concatenate([bias] * R, axis=1),
                                     jnp.zeros((LANES - HEAD_DIM - n_blk, R * TQ), BF16)], axis=0))

    m_ref[...] = jnp.full(m_ref.shape, NEG_INF, F32)
    acc_ref[...] = jnp.zeros(acc_ref.shape, F32)

    def scores(g, kt):
        return _dot(ks_ref[0, g, pl.ds(pl.multiple_of(kt * TK, TK), TK), :], sels[g])

    def accumulate(g, kt, s):
        v = vs_ref[0, g, kt]
        for r in range(R):
            sl = slice(r * TQ, (r + 1) * TQ)
            m_old = m_ref[g, :, sl]
            m_new = jnp.maximum(m_old, jnp.max(s[:, sl], axis=0, keepdims=True))
            alpha = jnp.exp2(m_old - m_new)
            p = jnp.exp2(s[:, sl] - m_new)
            acc_ref[g, :, sl] = alpha * acc_ref[g, :, sl] + _dot(v, p.astype(BF16))
            m_ref[g, :, sl] = m_new

    tri = (lax.broadcasted_iota(jnp.int32, (KV_CHUNK, KV_CHUNK), 0)
           <= lax.broadcasted_iota(jnp.int32, (KV_CHUNK, KV_CHUNK), 1))

    def accumulate_diagonal(g, s):
        v = vs_ref[0, g, qi]
        for h in range(n_sub):
            lo, n_k = h * KV_CHUNK, (h + 1) * KV_CHUNK
            cols = [slice(r * TQ + lo, r * TQ + n_k) for r in range(R)]
            s_h = jnp.concatenate([jnp.concatenate(([s[0:lo, c]] if h else []) + [jnp.where(tri, s[lo:n_k, c], NEG_INF)],
                                                   axis=0) for c in cols], axis=1)
            m_old = jnp.concatenate([m_ref[g, :, c] for c in cols], axis=1)
            m_new = jnp.maximum(m_old, jnp.max(s_h, axis=0, keepdims=True))
            alpha = jnp.exp2(m_old - m_new)
            pv = _dot(v[:, 0:n_k], jnp.exp2(s_h - m_new).astype(BF16))
            for r, c in enumerate(cols):
                sub = slice(r * KV_CHUNK, (r + 1) * KV_CHUNK)
                acc_ref[g, :, c] = alpha[:, sub] * acc_ref[g, :, c] + pv[:, sub]

    assert G == 2
    s0_ref[...] = scores(0, 0)

    for g in range(G):
        o_cmp.append(_dot(vcmp_ref[:, g * n_cp:(g + 1) * n_cp],
                          jnp.concatenate(p_cmp[g], axis=1).astype(BF16)))
    for g in range(G):
        sub = []
        for h in range(n_sub):
            s_heads = [s_win[g][h][:, r * KV_CHUNK:(r + 1) * KV_CHUNK] for r in range(R)]
            e_w = jnp.concatenate(masked_exp(s_heads, win_mask[h]), axis=1).astype(BF16)
            v_w = jnp.concatenate([vw_ref[0, g, win_c0[h] + j] for j in range(n_wc)], axis=1)
            sub.append(_dot(v_w, e_w))
        o_win.append(jnp.concatenate([sub[h][:, r * KV_CHUNK:(r + 1) * KV_CHUNK]
                                      for r in range(R) for h in range(n_sub)], axis=1))

    def interior_tile(kt, carry):
        s1 = scores(1, kt)
        accumulate(0, kt, s0_ref[...])
        s0_ref[...] = scores(0, kt + 1)
        accumulate(1, kt, s1)
        return carry

    lax.fori_loop(0, qi, interior_tile, 0)
    s1 = scores(1, qi)
    accumulate_diagonal(0, s0_ref[...])
    accumulate_diagonal(1, s1)

    gates = _sigmoid(gate_ref[0].T)
    outs = []
    for g in range(G):
        o_sel = acc_ref[g]
        for r, (oc, os_, ow) in enumerate(zip(heads(o_cmp[g]), heads(o_sel), heads(o_win[g]))):
            g0 = g * HEAD_DIM + r
            g_sel = gates[g0 + R:g0 + R + 1] * (1.0 / os_[HEAD_DIM:HEAD_DIM + 1])
            g_win = gates[g0 + 2 * R:g0 + 2 * R + 1] * (1.0 / ow[HEAD_DIM:HEAD_DIM + 1])
            outs.append(gates[g0:g0 + 1] * oc + g_sel * os_[0:HEAD_DIM] + g_win * ow[0:HEAD_DIM])
    o_ref[0] = jnp.concatenate(outs, axis=0).T.astype(BF16)


def _nsa(q, kcmp, vcmp, ks, vs, kw, vw, gates):
    B, G, R, dh, S = q.shape
    n_cp = kcmp.shape[2]
    assert TQ == TK, "the key sweep treats exactly one tile per query tile as the diagonal"
    k_spec = lambda a: pl.BlockSpec((1,) + a.shape[1:], lambda b, i: (b, 0, 0, 0))
    vt_spec = lambda a: pl.BlockSpec((1,) + a.shape[1:], lambda b, i: (b, 0, 0, 0, 0))
    return pl.pallas_call(
        _nsa_kernel,
        grid=(B, S // TQ),
        in_specs=[
            pl.BlockSpec((1, G, R, dh, TQ), lambda b, i: (b, 0, 0, 0, i)),
            pl.BlockSpec((1, G, n_cp, dh), lambda b, i: (b, 0, 0, 0)),
            pl.BlockSpec((dh, G * n_cp), lambda b, i: (0, b)),
            k_spec(ks), vt_spec(vs), k_spec(kw), vt_spec(vw),
            pl.BlockSpec((1, TQ, LANES), lambda b, i: (b, i, 0)),
        ],
        out_specs=pl.BlockSpec((1, TQ, G * R * dh), lambda b, i: (b, i, 0)),
        out_shape=jax.ShapeDtypeStruct((B, S, G * R * dh), BF16),
        scratch_shapes=[pltpu.VMEM((G, 1, R * TQ), F32), pltpu.VMEM((G, dh + ONES_ROWS, R * TQ), F32),
                        pltpu.VMEM((TK, R * TQ), F32)],
        compiler_params=pltpu.CompilerParams(
            dimension_semantics=("parallel", "arbitrary"), vmem_limit_bytes=VMEM_LIMIT_BYTES),
        name="nsa_attention",
    )(q, kcmp, vcmp, ks, vs, kw, vw, gates)


def _retention_kernel(q_ref, kt_ref, v_ref, g_ref, w_ref, o_ref, state_ref):
    C = RET_CHUNK
    NB = q_ref.shape[0]

    @pl.when(pl.program_id(1) == 0)
    def _():
        state_ref[...] = jnp.zeros(state_ref.shape, F32)

    i_col = lax.broadcasted_iota(jnp.int32, (C, 1), 0)
    i_row = lax.broadcasted_iota(jnp.int32, (1, C), 1)
    d_int = i_col - i_row
    log_gamma = [math.log(1.0 - 2.0 ** (-5.0 - h)) for h in range(RET_HEADS)]
    dmat = [jnp.where(d_int >= 0, jnp.exp(lg * jnp.maximum(d_int, 0).astype(F32)), 0.0) for lg in log_gamma]
    xi = [jnp.exp(lg * (i_col + 1).astype(F32)) for lg in log_gamma]
    zeta = [jnp.exp(lg * (C - 1 - i_row).astype(F32)) for lg in log_gamma]
    gamma_c = [math.exp(lg * C) for lg in log_gamma]

    units = [(n, h) for n in range(NB) for h in range(RET_HEADS)]
    q, kt, v, st = {}, {}, {}, {}
    for n, h in units:
        q[n, h] = q_ref[n, :, h * RET_QK_DIM:(h + 1) * RET_QK_DIM]
        kt[n, h] = kt_ref[n, 0, h * RET_QK_DIM:(h + 1) * RET_QK_DIM, :]
        v[n, h] = v_ref[n, :, h * RET_V_DIM:(h + 1) * RET_V_DIM]
        st[n, h] = state_ref[n, h]
    inner = {u: _dot(q[u], kt[u]) for u in units}
    cross = {u: _dot(q[u], st[u].astype(BF16)) for u in units}
    kv = {u: _dot((kt[u].astype(F32) * zeta[u[1]]).astype(BF16), v[u]) for u in units}
    for n, h in units:
        u = (n, h)
        vsl = slice(h * RET_V_DIM, (h + 1) * RET_V_DIM)
        y = _dot((inner[u] * dmat[h]).astype(BF16), v[u]) + cross[u] * xi[h]
        state_ref[n, h] = gamma_c[h] * st[u] + kv[u]

        mu = jnp.mean(y, axis=-1, keepdims=True)
        yc = y - mu
        var = jnp.mean(yc * yc, axis=-1, keepdims=True)
        yn = yc * lax.rsqrt(var + EPS) * w_ref[:, vsl]
        gate = g_ref[n, :, vsl].astype(F32)
        o_ref[n, :, vsl] = (gate * _sigmoid(gate) * yn).astype(BF16)


def _retention(rq, rkt, rv, rg, w):
    B, S, _ = rq.shape
    C = RET_CHUNK
    nb = math.gcd(RET_BATCH, B)
    q_spec = pl.BlockSpec((nb, C, RET_QK_WIDTH), lambda b, c: (b, c, 0))
    kt_spec = pl.BlockSpec((nb, 1, RET_QK_WIDTH, C), lambda b, c: (b, c, 0, 0))
    v_spec = pl.BlockSpec((nb, C, RET_V_WIDTH), lambda b, c: (b, c, 0))
    return pl.pallas_call(
        _retention_kernel,
        grid=(B // nb, S // C),
        in_specs=[q_spec, kt_spec, v_spec, v_spec, pl.BlockSpec((1, RET_V_WIDTH), lambda b, c: (0, 0))],
        out_specs=v_spec,
        out_shape=jax.ShapeDtypeStruct((B, S, RET_V_WIDTH), BF16),
        scratch_shapes=[pltpu.VMEM((nb, RET_HEADS, RET_QK_DIM, RET_V_DIM), F32)],
        compiler_params=pltpu.CompilerParams(
            dimension_semantics=("parallel", "arbitrary"), vmem_limit_bytes=VMEM_LIMIT_BYTES),
        name="retention",
    )(rq, rkt, rv, rg, w)


def _out_ffn_kernel(x_ref, a_ref, r_ref, wo_ref, ln_ref, wu_ref, wd_ref, o_ref):
    na = a_ref.shape[1]
    mix = _dot(a_ref[...], wo_ref[0:na, :]) + _dot(r_ref[...], wo_ref[na:, :])
    h = x_ref[...] + mix
    ms = jnp.mean(h * h, axis=-1, keepdims=True)
    hn = (h * lax.rsqrt(ms + EPS) * ln_ref[...]).astype(BF16)
    d_ff = wu_ref.shape[1]
    acc = None
    for f in range(d_ff // FF_CHUNK):
        cols = slice(f * FF_CHUNK, (f + 1) * FF_CHUNK)
        u = jnp.maximum(_dot(hn, wu_ref[:, cols]), 0.0)
        d = _dot((u * u).astype(BF16), wd_ref[cols, :])
        acc = d if acc is None else acc + d
    o_ref[...] = h + acc


def _out_ffn(x2, o_nsa, o_ret, w_out, ln2, w_up, w_down):
    N, D = x2.shape
    tm = TM_FFN
    row = lambda i: (i, 0)
    const = lambda i: (0, 0)
    resident = functools.partial(pl.BlockSpec, index_map=const, pipeline_mode=pl.Buffered(1))
    return pl.pallas_call(
        _out_ffn_kernel,
        grid=(N // tm,),
        in_specs=[
            pl.BlockSpec((tm, D), row),
            pl.BlockSpec((tm, o_nsa.shape[1]), row),
            pl.BlockSpec((tm, o_ret.shape[1]), row),
            resident(w_out.shape),
            pl.BlockSpec((1, D), const),
            resident(w_up.shape),
            resident(w_down.shape),
        ],
        out_specs=pl.BlockSpec((tm, D), row),
        out_shape=jax.ShapeDtypeStruct((N, D), F32),
        compiler_params=pltpu.CompilerParams(
            dimension_semantics=("parallel",), vmem_limit_bytes=VMEM_LIMIT_BYTES),
        name="out_ffn",
    )(x2, o_nsa, o_ret, w_out, ln2, w_up, w_down)


def _rope_tables(S):
    half = HEAD_DIM // 2
    inv = ROPE_THETA ** (-jnp.arange(half, dtype=F32) / half)
    ang = jnp.arange(S).astype(F32)[:, None] * inv[None, :]
    cos, sin = jnp.cos(ang), jnp.sin(ang)
    reps = LANES // HEAD_DIM
    cos_h = jnp.concatenate([cos, cos], axis=-1)
    sin_h = jnp.concatenate([-sin, sin], axis=-1)
    return jnp.tile(cos_h, (1, reps)), jnp.tile(sin_h, (1, reps)), cos_h.T, sin_h.T


def _layer(h, ln1_w, w_in, q_norm_w, k_norm_w, cmp_pe_k, cmp_pe_v, cmp_wk1, cmp_wk2,
           cmp_wv1, cmp_wv2, ret_norm_w, w_out, ln2_w, w_up, w_down):
    B, S, D = h.shape
    N = B * S
    G = NSA_KV_HEADS
    x2 = h.reshape(N, D)

    w_perm, w_perm_t = _permute_w_in(w_in)
    cos_l, sin_l, cos_t, sin_t = _rope_tables(S)
    reps = LANES // HEAD_DIM
    qn_t = jnp.broadcast_to(q_norm_w[:, None], (HEAD_DIM, TM_PROJ))
    kn = jnp.tile(k_norm_w, (1, reps))

    (q, kc, vc, ks, vs, kw, vw, gates, rq, rk, rv, rg) = _in_proj(
        x2, ln1_w[None, :], w_perm.astype(BF16), w_perm_t.astype(BF16), cos_l, sin_l, cos_t, sin_t, qn_t, kn, B, S)

    kcmp, vcmp = _compress(
        kc, vc, jnp.tile(cmp_pe_k, (1, G)), jnp.tile(cmp_pe_v, (1, G)),
        cmp_wk1.astype(BF16), cmp_wk2.astype(BF16), cmp_wv1.astype(BF16),
        jnp.pad(cmp_wv2, ((0, 0), (0, LANES - HEAD_DIM))).astype(BF16), B, S)

    o_nsa = _nsa(q, kcmp, vcmp, ks, vs, kw, vw, gates.reshape(B, S, LANES))
    o_ret = _retention(rq.reshape(B, S, -1), rk, rv.reshape(B, S, -1),
                       rg.reshape(B, S, -1), ret_norm_w.reshape(1, RET_V_WIDTH))

    out = _out_ffn(x2, o_nsa.reshape(N, -1), o_ret.reshape(N, -1), w_out.astype(BF16),
                   ln2_w[None, :], w_up.astype(BF16), w_down.astype(BF16))
    return out.reshape(B, S, D)


def kernel(x, ln1_w, w_in, q_norm_w, k_norm_w, cmp_pe_k, cmp_pe_v, cmp_wk1, cmp_wk2, cmp_wv1, cmp_wv2,
           ret_norm_w, w_out, ln2_w, w_up, w_down):
    h = x
    for l in range(ln1_w.shape[0]):
        h = _layer(h, ln1_w[l], w_in[l], q_norm_w[l], k_norm_w[l], cmp_pe_k[l], cmp_pe_v[l],
                   cmp_wk1[l], cmp_wk2[l], cmp_wv1[l], cmp_wv2[l], ret_norm_w[l], w_out[l],
                   ln2_w[l], w_up[l], w_down[l])
    return h
```
